```python
import jax, jax.numpy as jnp
from jax import lax
import numpy as np

D_MODEL = 1024
BATCH = 4
SEQ = 8192
DEPTH = 1

N_META = 16
ROPE_THETA = 10000.0
EPS = 1e-6
BLOCK_Q = 128
MLA_HEADS = 8
MLA_Q_RANK = 256
MLA_KV_RANK = 128
MLA_NOPE_DIM = 64
MLA_ROPE_DIM = 32
MLA_QK_DIM = MLA_NOPE_DIM + MLA_ROPE_DIM
MLA_V_DIM = 64
DSA_HEADS = 8
DSA_HEAD_DIM = 64
IDX_HEADS = 8
IDX_DIM = 64
TOPK_MAX = 256
MIX_WIDTH = MLA_HEADS * MLA_V_DIM + DSA_HEADS * DSA_HEAD_DIM
IN_SIZES = (MLA_Q_RANK, MLA_KV_RANK, MLA_ROPE_DIM,
            DSA_HEADS * DSA_HEAD_DIM, DSA_HEADS * DSA_HEAD_DIM, DSA_HEADS * DSA_HEAD_DIM,
            IDX_HEADS * IDX_DIM, IDX_DIM, IDX_HEADS)
C_IN = sum(IN_SIZES)
D_FF_RAW = -(-8 * D_MODEL // 3)
D_FF = -(-D_FF_RAW // 256) * 256

kernel_name = 'hybrid_mla_dsa_meta_layer'


def rmsnorm(x, g):
    x32 = x.astype(jnp.float32)
    y = x32 * lax.rsqrt(jnp.mean(x32 * x32, axis=-1, keepdims=True) + EPS)
    return (y * g.astype(jnp.float32)).astype(x.dtype)


def rope(x, pos):
    half = x.shape[-1] // 2
    inv = jnp.power(ROPE_THETA, -jnp.arange(half, dtype=jnp.float32) / half)
    ang = pos.astype(jnp.float32)[:, None] * inv[None, :]
    cos = jnp.cos(ang)[:, None, :]
    sin = jnp.sin(ang)[:, None, :]
    x32 = x.astype(jnp.float32)
    x1, x2 = x32[..., :half], x32[..., half:]
    return jnp.concatenate([x1 * cos - x2 * sin, x2 * cos + x1 * sin], axis=-1).astype(x.dtype)


def to_blocks(a):
    b, tp = a.shape[0], a.shape[1]
    return a.reshape((b, tp // BLOCK_Q, BLOCK_Q) + a.shape[2:]).swapaxes(0, 1)


def from_blocks(a):
    a = a.swapaxes(0, 1)
    return a.reshape((a.shape[0], a.shape[1] * a.shape[2]) + a.shape[3:])


def causal_block_attention(q, k, v):
    tp = q.shape[1]
    scale = q.shape[-1] ** -0.5
    kpos = jnp.arange(tp)

    def one_block(args):
        qb, blk = args
        qpos = blk * BLOCK_Q + jnp.arange(BLOCK_Q)
        s = jnp.einsum('bqhd,bkhd->bhqk', qb, k, preferred_element_type=jnp.float32) * scale
        s = jnp.where((kpos[None, :] <= qpos[:, None])[None, None], s, -jnp.inf)
        p = jax.nn.softmax(s, axis=-1).astype(v.dtype)
        return jnp.einsum('bhqk,bkhd->bqhd', p, v)

    out = lax.map(one_block, (to_blocks(q), jnp.arange(tp // BLOCK_Q)))
    return from_blocks(out)


def indexed_sparse_attention(q, k, v, q_idx, k_idx, w_idx, topk):
    tp = q.shape[1]
    scale = q.shape[-1] ** -0.5
    idx_scale = IDX_DIM ** -0.5
    kpos = jnp.arange(tp)

    def one_block(args):
        qb, qib, wib, blk = args
        qpos = blk * BLOCK_Q + jnp.arange(BLOCK_Q)
        causal = kpos[None, :] <= qpos[:, None]
        logits = jnp.einsum('bqhd,bkd->bqhk', qib, k_idx, preferred_element_type=jnp.float32) * idx_scale
        score = jnp.einsum('bqh,bqhk->bqk', wib.astype(jnp.float32), jax.nn.relu(logits))
        score = jnp.where(causal[None], score, -jnp.inf)
        _, sel = lax.top_k(score, topk)
        valid = sel <= qpos[None, :, None]
        k_sel = jax.vmap(lambda kb, ib: kb[ib])(k, sel)
        v_sel = jax.vmap(lambda vb, ib: vb[ib])(v, sel)
        s = jnp.einsum('bqhd,bqkhd->bhqk', qb, k_sel, preferred_element_type=jnp.float32) * scale
        s = jnp.where(valid[:, None], s, -jnp.inf)
        p = jax.nn.softmax(s, axis=-1).astype(v.dtype)
        return jnp.einsum('bhqk,bqkhd->bqhd', p, v_sel)

    out = lax.map(one_block, (to_blocks(q), to_blocks(q_idx), to_blocks(w_idx), jnp.arange(tp // BLOCK_Q)))
    return from_blocks(out)


def hybrid_mixer(u, pos, topk, w_in, q_norm_g, w_uq, kv_norm_g, w_ukv, w_o):
    b, tp, _ = u.shape
    proj = u @ w_in
    splits = [int(v) for v in np.cumsum(IN_SIZES)[:-1]]
    c_q, c_kv, k_r, q_s, k_s, v_s, q_i, k_i, w_i = jnp.split(proj, splits, axis=-1)

    q = (rmsnorm(c_q, q_norm_g) @ w_uq).reshape(b, tp, MLA_HEADS, MLA_QK_DIM)
    q_mla = jnp.concatenate([q[..., :MLA_NOPE_DIM], rope(q[..., MLA_NOPE_DIM:], pos)], axis=-1)
    kv = (rmsnorm(c_kv, kv_norm_g) @ w_ukv).reshape(b, tp, MLA_HEADS, MLA_NOPE_DIM + MLA_V_DIM)
    k_rope = rope(k_r[:, :, None, :], pos)
    k_mla = jnp.concatenate([kv[..., :MLA_NOPE_DIM],
                             jnp.broadcast_to(k_rope, (b, tp, MLA_HEADS, MLA_ROPE_DIM))], axis=-1)
    v_mla = kv[..., MLA_NOPE_DIM:]
    o_mla = causal_block_attention(q_mla, k_mla, v_mla)

    q_s = rope(q_s.reshape(b, tp, DSA_HEADS, DSA_HEAD_DIM), pos)
    k_s = rope(k_s.reshape(b, tp, DSA_HEADS, DSA_HEAD_DIM), pos)
    v_s = v_s.reshape(b, tp, DSA_HEADS, DSA_HEAD_DIM)
    q_i = rope(q_i.reshape(b, tp, IDX_HEADS, IDX_DIM), pos)
    k_i = rope(k_i[:, :, None, :], pos)[:, :, 0, :]
    w_i = w_i * (IDX_HEADS ** -0.5)
    o_dsa = indexed_sparse_attention(q_s, k_s, v_s, q_i, k_i, w_i, topk)

    o = jnp.concatenate([o_mla.reshape(b, tp, MLA_HEADS * MLA_V_DIM),
                         o_dsa.reshape(b, tp, DSA_HEADS * DSA_HEAD_DIM)], axis=-1)
    return o @ w_o


def swiglu(u, w_gate, w_up, w_down):
    return (jax.nn.silu(u @ w_gate) * (u @ w_up)) @ w_down


def setup_inputs(seed: int = 0) -> dict:
    key = jax.random.key(seed)
    ks = jax.random.split(key, 16)
    f32 = jnp.float32

    def w(k, shape, fan_in):
        return jax.random.normal(k, shape, f32) * (fan_in ** -0.5)

    def g(k, shape):
        return 1.0 + 0.02 * jax.random.normal(k, shape, f32)

    return {
        'x': jax.random.normal(ks[0], (BATCH, SEQ, D_MODEL), f32),
        'meta_tokens': jax.random.normal(ks[1], (N_META, D_MODEL), f32),
        'attn_norm_g': g(ks[2], (DEPTH, D_MODEL)),
        'w_in': w(ks[3], (DEPTH, D_MODEL, C_IN), D_MODEL),
        'mla_q_norm_g': g(ks[4], (DEPTH, MLA_Q_RANK)),
        'w_uq': w(ks[5], (DEPTH, MLA_Q_RANK, MLA_HEADS * MLA_QK_DIM), MLA_Q_RANK),
        'mla_kv_norm_g': g(ks[6], (DEPTH, MLA_KV_RANK)),
        'w_ukv': w(ks[7], (DEPTH, MLA_KV_RANK, MLA_HEADS * (MLA_NOPE_DIM + MLA_V_DIM)), MLA_KV_RANK),
        'w_o': w(ks[8], (DEPTH, MIX_WIDTH, D_MODEL), MIX_WIDTH),
        'ffn_norm_g': g(ks[9], (DEPTH, D_MODEL)),
        'w_gate': w(ks[10], (DEPTH, D_MODEL, D_FF), D_MODEL),
        'w_up': w(ks[11], (DEPTH, D_MODEL, D_FF), D_MODEL),
        'w_down': w(ks[12], (DEPTH, D_FF, D_MODEL), D_FF),
        'final_norm_g': g(ks[13], (D_MODEL,)),
    }


def reference(x, meta_tokens, attn_norm_g, w_in, mla_q_norm_g, w_uq, mla_kv_norm_g, w_ukv, w_o,
              ffn_norm_g, w_gate, w_up, w_down, final_norm_g):
    b, seq, _ = x.shape
    topk = min(TOPK_MAX, seq // 4)
    meta = jnp.broadcast_to(meta_tokens[None].astype(x.dtype), (b, N_META, D_MODEL))
    h = jnp.concatenate([meta, x], axis=1)
    t = h.shape[1]
    tp = -(-t // BLOCK_Q) * BLOCK_Q
    h = jnp.pad(h, ((0, 0), (0, tp - t), (0, 0)))
    pos = jnp.arange(tp, dtype=jnp.int32)
    for l in range(DEPTH):
        h = h + hybrid_mixer(rmsnorm(h, attn_norm_g[l]), pos, topk, w_in[l], mla_q_norm_g[l], w_uq[l],
                             mla_kv_norm_g[l], w_ukv[l], w_o[l])
        h = h + swiglu(rmsnorm(h, ffn_norm_g[l]), w_gate[l], w_up[l], w_down[l])
    h = rmsnorm(h, final_norm_g)
    return h[:, N_META:t]
```

```python
import functools

import numpy as np
import jax
import jax.numpy as jnp
from jax import lax
from jax.experimental import pallas as pl
from jax.experimental.pallas import tpu as pltpu

F32 = jnp.float32
BF16 = jnp.bfloat16
I32 = jnp.int32

D_MODEL = 1024
N_META = 16
ROPE_THETA = 10000.0
EPS = 1e-6
BLOCK_Q = 128
MLA_HEADS = 8
MLA_Q_RANK = 256
MLA_KV_RANK = 128
MLA_NOPE_DIM = 64
MLA_ROPE_DIM = 32
MLA_QK_DIM = MLA_NOPE_DIM + MLA_ROPE_DIM
MLA_V_DIM = 64
DSA_HEADS = 8
DSA_HEAD_DIM = 64
IDX_HEADS = 8
IDX_DIM = 64
TOPK_MAX = 256
D_FF = 2816

LANES = 128
N_PAIRS = 4
NEG_BIG = -1e30
INT_MIN = -2 ** 31
VMEM_LIMIT = 58 * 1024 * 1024
ROW_PAD = 256

_OFF_CQ, _OFF_CKV, _OFF_KR, _OFF_QS, _OFF_KS, _OFF_VS, _OFF_QI, _OFF_KI, _OFF_WI = (
    0, 256, 384, 416, 928, 1440, 1952, 2464, 2528)
_C_IN = 2536
_P_CQ, _P_CKV, _P_QS, _P_KS, _P_VS, _P_QI, _P_KI, _P_KR, _P_WI, _P_END = (
    0, 256, 384, 896, 1408, 1920, 2432, 2560, 2688, 2816)


def _pair_rope_cols(base, hd=64):
    half = hd // 2
    cols = []
    for p in range(N_PAIRS):
        a, b = 2 * p, 2 * p + 1
        cols += [base + a * hd + d for d in range(half)]
        cols += [base + b * hd + d for d in range(half)]
        cols += [base + a * hd + half + d for d in range(half)]
        cols += [base + b * hd + half + d for d in range(half)]
    return cols


def _w_in_cols():
    z = _C_IN
    cols = list(range(_OFF_CQ, _OFF_CQ + 256)) + list(range(_OFF_CKV, _OFF_CKV + 128))
    cols += _pair_rope_cols(_OFF_QS) + _pair_rope_cols(_OFF_KS)
    cols += list(range(_OFF_VS, _OFF_VS + 512))
    cols += _pair_rope_cols(_OFF_QI)
    ki1 = [_OFF_KI + d for d in range(32)]
    ki2 = [_OFF_KI + 32 + d for d in range(32)]
    cols += ki1 + ki1 + ki2 + ki2
    kr1 = [_OFF_KR + d for d in range(16)]
    kr2 = [_OFF_KR + 16 + d for d in range(16)]
    cols += kr1 + kr1 + [z] * 32 + kr2 + kr2 + [z] * 32
    cols += [_OFF_WI + d for d in range(IDX_HEADS)] + [z] * (LANES - IDX_HEADS)
    assert len(cols) == _P_END
    return np.asarray(cols, np.int32)


def _w_uq_cols():
    z = MLA_HEADS * MLA_QK_DIM
    cols = []
    for p in range(N_PAIRS):
        a, b = 2 * p, 2 * p + 1
        cols += [a * MLA_QK_DIM + d for d in range(64)] + [b * MLA_QK_DIM + d for d in range(64)]
        cols += [a * MLA_QK_DIM + 64 + d for d in range(16)] + [b * MLA_QK_DIM + 64 + d for d in range(16)]
        cols += [z] * 32
        cols += [a * MLA_QK_DIM + 80 + d for d in range(16)] + [b * MLA_QK_DIM + 80 + d for d in range(16)]
        cols += [z] * 32
    return np.asarray(cols, np.int32)


def _w_ukv_cols():
    cols = [h * 128 + d for h in range(MLA_HEADS) for d in range(64)]
    cols += [h * 128 + 64 + d for h in range(MLA_HEADS) for d in range(64)]
    return np.asarray(cols, np.int32)


def _take_cols(w, cols):
    wz = jnp.concatenate([w, jnp.zeros((w.shape[0], 1), w.dtype)], axis=1)
    return jnp.take(wz, jnp.asarray(cols), axis=1)


def _rope_tables(tp):
    pos = jnp.arange(tp, dtype=jnp.int32).astype(F32)

    def cs(half):
        inv = jnp.power(ROPE_THETA, -jnp.arange(half, dtype=F32) / half)
        ang = pos[:, None] * inv[None, :]
        return jnp.cos(ang), jnp.sin(ang)

    c32, s32 = cs(32)
    c16, s16 = cs(16)
    z = jnp.zeros((tp, 32), F32)
    c64t = jnp.concatenate([c32, c32, c32, c32], axis=1)
    s64t = jnp.concatenate([-s32, -s32, s32, s32], axis=1)
    c32t = jnp.concatenate([c16, c16, z, c16, c16, z], axis=1)
    s32t = jnp.concatenate([-s16, -s16, z, s16, s16, z], axis=1)
    return c64t, s64t, c32t, s32t


def _rms(x, g):
    return x * lax.rsqrt(jnp.mean(x * x, axis=-1, keepdims=True) + EPS) * g


def _rope_slab(x, c, s):
    return x * c + pltpu.roll(x, 64, axis=1) * s


def _proj_kernel(h_ref, g_ref, win_ref, gq_ref, wuq_ref, gkv_ref, wukv_ref,
                 c64_ref, s64_ref, c32_ref, s32_ref,
                 qmla_ref, kmla_ref, vmla_ref, qs_ref, ks_ref, vs_ref, qi_ref, ki_ref, wi_ref):
    u = _rms(h_ref[0], g_ref[...]).astype(BF16)
    proj = jnp.dot(u, win_ref[...], preferred_element_type=F32)
    c64, s64, c32, s32 = c64_ref[...], s64_ref[...], c32_ref[...], s32_ref[...]

    cq = _rms(proj[:, _P_CQ:_P_CQ + 256], gq_ref[...]).astype(BF16)
    q = jnp.dot(cq, wuq_ref[...], preferred_element_type=F32)
    ckv = _rms(proj[:, _P_CKV:_P_CKV + 128], gkv_ref[...]).astype(BF16)
    kv = jnp.dot(ckv, wukv_ref[...], preferred_element_type=F32)
    kr = _rope_slab(proj[:, _P_KR:_P_KR + 128], c32, s32).astype(BF16)
    q_scale = MLA_QK_DIM ** -0.5
    s_scale = DSA_HEAD_DIM ** -0.5
    i_scale = IDX_DIM ** -0.5
    for p in range(N_PAIRS):
        qn = q[:, p * 256:p * 256 + 128] * q_scale
        qr = _rope_slab(q[:, p * 256 + 128:p * 256 + 256], c32, s32) * q_scale
        qmla_ref[0, p, :, 0:128] = qn.astype(BF16)
        qmla_ref[0, p, :, 128:256] = qr.astype(BF16)
        kmla_ref[0, p, :, 0:128] = kv[:, p * 128:(p + 1) * 128].astype(BF16)
        kmla_ref[0, p, :, 128:256] = kr
        vmla_ref[0, p] = kv[:, 512 + p * 128:512 + (p + 1) * 128].astype(BF16)
        sl = slice(p * 128, (p + 1) * 128)
        qs_ref[0, p] = (_rope_slab(proj[:, _P_QS:_P_KS][:, sl], c64, s64) * s_scale).astype(BF16)
        ks_ref[0, p] = _rope_slab(proj[:, _P_KS:_P_VS][:, sl], c64, s64).astype(BF16)
        vs_ref[0, p] = proj[:, _P_VS:_P_QI][:, sl].astype(BF16)
        qi_ref[0, p] = (_rope_slab(proj[:, _P_QI:_P_KI][:, sl], c64, s64) * i_scale).astype(BF16)
    ki_ref[0] = _rope_slab(proj[:, _P_KI:_P_KR], c64, s64).astype(BF16)
    wi_ref[0] = proj[:, _P_WI:_P_END] * (IDX_HEADS ** -0.5)


def _proj_call(h, g, win, gq, wuq, gkv, wukv, tables, *, tm, interpret=False):
    b, tp, d = h.shape
    nt = tp // tm
    const = lambda shape: pl.BlockSpec(shape, lambda bi, i: (0,) * len(shape))
    tab = pl.BlockSpec((tm, LANES), lambda bi, i: (i, 0))
    pair = lambda w: pl.BlockSpec((1, N_PAIRS, tm, w), lambda bi, i: (bi, 0, i, 0))
    row = lambda w: pl.BlockSpec((1, tm, w), lambda bi, i: (bi, i, 0))
    sds = jax.ShapeDtypeStruct
    out_shape = (
        sds((b, N_PAIRS, tp, 256), BF16), sds((b, N_PAIRS, tp, 256), BF16), sds((b, N_PAIRS, tp, 128), BF16),
        sds((b, N_PAIRS, tp, 128), BF16), sds((b, N_PAIRS, tp, 128), BF16), sds((b, N_PAIRS, tp, 128), BF16),
        sds((b, N_PAIRS, tp, 128), BF16), sds((b, tp, 128), BF16), sds((b, tp, 128), F32))
    return pl.pallas_call(
        _proj_kernel,
        grid=(b, nt),
        in_specs=[row(d), const((1, d)), const(win.shape), const((1, MLA_Q_RANK)), const(wuq.shape),
                  const((1, MLA_KV_RANK)), const(wukv.shape), tab, tab, tab, tab],
        out_specs=(pair(256), pair(256), pair(128), pair(128), pair(128), pair(128), pair(128),
                   row(128), row(128)),
        out_shape=out_shape,
        compiler_params=pltpu.CompilerParams(
            dimension_semantics=("arbitrary", "arbitrary"), vmem_limit_bytes=VMEM_LIMIT),
        interpret=interpret,
        name="proj",
    )(h, g, win, gq, wuq, gkv, wukv, *tables)


def _flash_update(s, v, m_scr, l_scr, acc_scr):
    tk = s.shape[1]
    m_prev = m_scr[...]
    m_new = jnp.maximum(m_prev, jnp.max(s, axis=-1, keepdims=True))
    alpha = jnp.exp(m_prev - m_new)
    p = jnp.exp(s - jnp.concatenate([m_new] * (tk // LANES), axis=1))
    l_scr[...] = alpha * l_scr[...] + jnp.sum(p, axis=-1, keepdims=True)
    acc_scr[...] = alpha * acc_scr[...] + jnp.dot(p.astype(BF16), v, preferred_element_type=F32)
    m_scr[...] = m_new


def _flash_init(m_scr, l_scr, acc_scr):
    m_scr[...] = jnp.full(m_scr.shape, NEG_BIG, F32)
    l_scr[...] = jnp.zeros(l_scr.shape, F32)
    acc_scr[...] = jnp.zeros(acc_scr.shape, F32)


def _flash_finish(o_ref, l_scr, acc_scr, tq):
    lane = lax.broadcasted_iota(I32, (tq, LANES), 1)
    oa = acc_scr[0:tq] / l_scr[0:tq]
    ob = acc_scr[tq:2 * tq] / l_scr[tq:2 * tq]
    o_ref[0] = jnp.where(lane < 64, oa, ob).astype(o_ref.dtype)


def _causal_ok(tq, width, row0, col0):
    r = lax.broadcasted_iota(I32, (2 * tq, width), 0) & (tq - 1)
    c = lax.broadcasted_iota(I32, (2 * tq, width), 1)
    return (c + col0) <= (r + row0)


_NT = (((1,), (1,)), ((), ()))


def _mla_kernel(q_ref, k_ref, v_ref, o_ref, qm_scr, m_scr, l_scr, acc_scr, *, tq):
    i = pl.program_id(2)
    q = q_ref[0, 0]
    lane = lax.broadcasted_iota(I32, (tq, 256), 1)
    in_a = (lane < 64) | ((lane >= 128) & (lane < 144)) | ((lane >= 192) & (lane < 208))
    in_b = ((lane >= 64) & (lane < 128)) | ((lane >= 144) & (lane < 160)) | ((lane >= 208) & (lane < 224))
    zero = jnp.zeros_like(q)
    qm_scr[0:tq] = jnp.where(in_a, q, zero)
    qm_scr[tq:2 * tq] = jnp.where(in_b, q, zero)
    _flash_init(m_scr, l_scr, acc_scr)

    def step(start, width, masked):
        k = k_ref[0, 0, pl.ds(start, width), :]
        v = v_ref[0, 0, pl.ds(start, width), :]
        s = lax.dot_general(qm_scr[...], k, _NT, preferred_element_type=F32)
        if masked:
            s = jnp.where(_causal_ok(tq, width, 0, 0), s, NEG_BIG)
        _flash_update(s, v, m_scr, l_scr, acc_scr)

    def wide(j, carry):
        step(pl.multiple_of(j * (2 * tq), 2 * tq), 2 * tq, False)
        return carry

    lax.fori_loop(0, i // 2, wide, 0)

    @pl.when(i % 2 == 1)
    def _():
        step(pl.multiple_of((i - 1) * tq, tq), tq, False)

    step(pl.multiple_of(i * tq, tq), tq, True)
    _flash_finish(o_ref, l_scr, acc_scr, tq)


def _mla_call(qmla, kmla, vmla, *, tq, interpret=False):
    b, _, tp, _ = qmla.shape
    nq = tp // tq
    return pl.pallas_call(
        functools.partial(_mla_kernel, tq=tq),
        grid=(b, N_PAIRS, nq),
        in_specs=[pl.BlockSpec((1, 1, tq, 256), lambda bi, p, i: (bi, p, i, 0)),
                  pl.BlockSpec((1, 1, tp, 256), lambda bi, p, i: (bi, p, 0, 0)),
                  pl.BlockSpec((1, 1, tp, 128), lambda bi, p, i: (bi, p, 0, 0))],
        out_specs=pl.BlockSpec((1, tq, LANES), lambda bi, p, i: (bi, i, p)),
        out_shape=jax.ShapeDtypeStruct((b, tp, N_PAIRS * LANES), BF16),
        scratch_shapes=[pltpu.VMEM((2 * tq, 256), BF16), pltpu.VMEM((2 * tq, LANES), F32),
                        pltpu.VMEM((2 * tq, LANES), F32), pltpu.VMEM((2 * tq, LANES), F32)],
        compiler_params=pltpu.CompilerParams(
            dimension_semantics=("arbitrary", "arbitrary", "arbitrary"), vmem_limit_bytes=VMEM_LIMIT),
        interpret=interpret,
        name="mla_attn",
    )(qmla, kmla, vmla)


def _dsa_kernel(qi_ref, wi_ref, qs_ref, ki_ref, ks_ref, vs_ref, o_ref,
                key_scr, qim_scr, wb_scr, thr_scr, cut_scr, qm_scr, m_scr, l_scr, acc_scr, *, tq, topk):
    i = pl.program_id(1)
    p = pl.program_id(2)
    n128 = (i + 1) * (tq // LANES)
    lane = lax.broadcasted_iota(I32, (tq, LANES), 1)
    row = lax.broadcasted_iota(I32, (tq, LANES), 0)
    in_a = (lane < 32) | ((lane >= 64) & (lane < 96))
    rg = 128

    @pl.when(p == 0)
    def _select():
        for pp in range(N_PAIRS):
            qp = qi_ref[0, pp]
            zero = jnp.zeros_like(qp)
            qim_scr[(2 * pp) * tq:(2 * pp + 1) * tq] = jnp.where(in_a, qp, zero)
            qim_scr[(2 * pp + 1) * tq:(2 * pp + 2) * tq] = jnp.where(in_a, zero, qp)
        w = wi_ref[0]
        for hh in range(IDX_HEADS):
            wb_scr[hh] = jnp.broadcast_to(w[:, hh:hh + 1], (tq, LANES))

        def score_chunk(c, carry):
            start = pl.multiple_of(c * tq, tq)
            kc = ki_ref[0, pl.ds(start, tq), :]
            lg = lax.dot_general(qim_scr[...], kc, _NT, preferred_element_type=F32)
            sc = jnp.zeros((tq, tq), F32)
            for hh in range(IDX_HEADS):
                wb = jnp.concatenate([wb_scr[hh]] * (tq // LANES), axis=1)
                sc = sc + wb * jnp.maximum(lg[hh * tq:(hh + 1) * tq], 0.0)
            bits = pltpu.bitcast(sc, I32)
            key = bits ^ ((bits >> 31) & 0x7FFFFFFF)
            key = key - (key >> 31)
            r2 = lax.broadcasted_iota(I32, (tq, tq), 0) + i * tq
            c2 = lax.broadcasted_iota(I32, (tq, tq), 1) + c * tq
            key = jnp.where(c2 <= r2, key, INT_MIN)
            for t in range(tq // LANES):
                key_scr[c * (tq // LANES) + t] = key[:, t * LANES:(t + 1) * LANES]
            return carry

        lax.fori_loop(0, i + 1, score_chunk, 0)

        def count(pred, r0):
            def body(c, acc):
                return acc + pred(key_scr[c, pl.ds(r0, rg), :], c).astype(I32)
            acc = lax.fori_loop(0, n128, body, jnp.zeros((rg, LANES), I32))
            return jnp.sum(acc, axis=-1, keepdims=True)

        for g in range(tq // rg):
            r0 = g * rg

            def bit_step(it, t):
                cand = t + lax.shift_left(jnp.int32(1), (31 - it).astype(I32))
                cnt = count(lambda kk, c: kk >= cand, r0)
                return jnp.where(cnt >= topk, cand, t)

            thr = lax.fori_loop(0, 32, bit_step, jnp.full((rg, LANES), INT_MIN, I32))
            thr_scr[r0:r0 + rg] = thr
            n_gt = count(lambda kk, c: kk > thr, r0)
            n_ge = count(lambda kk, c: kk >= thr, r0)
            need = topk - n_gt
            cut_scr[r0:r0 + rg] = jnp.full((rg, LANES), 2 ** 30, I32)
            excess = jnp.max(jnp.where((n_ge > topk) & (thr[:, :1] > INT_MIN), 1, 0))

            @pl.when(excess > 0)
            def _ties():
                col = lax.broadcasted_iota(I32, (rg, LANES), 1)

                def idx_step(it, x):
                    cand = x + lax.shift_left(jnp.int32(1), (13 - it).astype(I32))
                    cnt = count(lambda kk, c: (kk == thr) & ((col + c * LANES) < cand), r0)
                    return jnp.where(cnt < need, cand, x)

                cut_scr[r0:r0 + rg] = lax.fori_loop(0, 14, idx_step, jnp.zeros((rg, LANES), I32))

        thr_all = thr_scr[...]
        cut_all = cut_scr[...]
        tie_ok = thr_all > INT_MIN

        def to_bias(c, carry):
            kk = key_scr[c]
            col = lane + c * LANES
            sel = (kk > thr_all) | ((kk == thr_all) & (col <= cut_all) & tie_ok)
            key_scr[c] = pltpu.bitcast(jnp.where(sel, 0.0, NEG_BIG).astype(F32), I32)
            return carry

        lax.fori_loop(0, n128, to_bias, 0)

    q = qs_ref[0, p]
    zero = jnp.zeros_like(q)
    qm_scr[0:tq] = jnp.where(in_a, q, zero)
    qm_scr[tq:2 * tq] = jnp.where(in_a, zero, q)
    _flash_init(m_scr, l_scr, acc_scr)

    def att_chunk(c, carry):
        start = pl.multiple_of(c * tq, tq)
        k = ks_ref[0, p, pl.ds(start, tq), :]
        v = vs_ref[0, p, pl.ds(start, tq), :]
        s = lax.dot_general(qm_scr[...], k, _NT, preferred_element_type=F32)
        bias = jnp.concatenate(
            [pltpu.bitcast(key_scr[c * (tq // LANES) + t], F32) for t in range(tq // LANES)], axis=1)
        s = s + jnp.concatenate([bias, bias], axis=0)
        _flash_update(s, v, m_scr, l_scr, acc_scr)
        return carry

    lax.fori_loop(0, i + 1, att_chunk, 0)
    _flash_finish(o_ref, l_scr, acc_scr, tq)


def _dsa_call(qi, wi, qs, ki, ks, vs, *, tq, topk, interpret=False):
    b, _, tp, _ = qs.shape
    nq = tp // tq
    once = pl.Buffered(1)
    return pl.pallas_call(
        functools.partial(_dsa_kernel, tq=tq, topk=topk),
        grid=(b, nq, N_PAIRS),
        in_specs=[pl.BlockSpec((1, N_PAIRS, tq, LANES), lambda bi, i, p: (bi, 0, i, 0)),
                  pl.BlockSpec((1, tq, LANES), lambda bi, i, p: (bi, i, 0)),
                  pl.BlockSpec((1, N_PAIRS, tq, LANES), lambda bi, i, p: (bi, 0, i, 0)),
                  pl.BlockSpec((1, tp, LANES), lambda bi, i, p: (bi, 0, 0), pipeline_mode=once),
                  pl.BlockSpec((1, N_PAIRS, tp, LANES), lambda bi, i, p: (bi, 0, 0, 0), pipeline_mode=once),
                  pl.BlockSpec((1, N_PAIRS, tp, LANES), lambda bi, i, p: (bi, 0, 0, 0), pipeline_mode=once)],
        out_specs=pl.BlockSpec((1, tq, LANES), lambda bi, i, p: (bi, i, p)),
        out_shape=jax.ShapeDtypeStruct((b, tp, N_PAIRS * LANES), BF16),
        scratch_shapes=[pltpu.VMEM((tp // LANES, tq, LANES), I32),
                        pltpu.VMEM((IDX_HEADS * tq, LANES), BF16),
                        pltpu.VMEM((IDX_HEADS, tq, LANES), F32),
                        pltpu.VMEM((tq, LANES), I32),
                        pltpu.VMEM((tq, LANES), I32),
                        pltpu.VMEM((2 * tq, LANES), BF16),
                        pltpu.VMEM((2 * tq, LANES), F32), pltpu.VMEM((2 * tq, LANES), F32),
                        pltpu.VMEM((2 * tq, LANES), F32)],
        compiler_params=pltpu.CompilerParams(
            dimension_semantics=("arbitrary", "arbitrary", "arbitrary"), vmem_limit_bytes=VMEM_LIMIT),
        interpret=interpret,
        name="dsa_attn",
    )(qi, wi, qs, ki, ks, vs)


def _ffn_kernel(h_ref, oa_ref, ob_ref, wo_ref, g2_ref, wg_ref, wu_ref, wd_ref, gf_ref, out_ref):
    o = jnp.concatenate([oa_ref[0], ob_ref[0]], axis=1)
    h1 = h_ref[0] + jnp.dot(o, wo_ref[...], preferred_element_type=F32)
    u = _rms(h1, g2_ref[...]).astype(BF16)
    gate = jnp.dot(u, wg_ref[...], preferred_element_type=F32)
    up = jnp.dot(u, wu_ref[...], preferred_element_type=F32)
    act = (gate * jax.nn.sigmoid(gate) * up).astype(BF16)
    h2 = h1 + jnp.dot(act, wd_ref[...], preferred_element_type=F32)
    out_ref[0] = _rms(h2, gf_ref[...])


def _ffn_call(h, oa, ob, wo, g2, wg, wu, wd, gf, *, tm, interpret=False):
    b, tp, d = h.shape
    once = pl.Buffered(1)
    const = lambda shape: pl.BlockSpec(shape, lambda bi, i: (0,) * len(shape), pipeline_mode=once)
    row = lambda w: pl.BlockSpec((1, tm, w), lambda bi, i: (bi, i, 0))
    return pl.pallas_call(
        _ffn_kernel,
        grid=(b, tp // tm),
        in_specs=[row(d), row(oa.shape[-1]), row(ob.shape[-1]), const(wo.shape), const((1, d)),
                  const(wg.shape), const(wu.shape), const(wd.shape), const((1, d))],
        out_specs=row(d),
        out_shape=jax.ShapeDtypeStruct((b, tp, d), F32),
        compiler_params=pltpu.CompilerParams(
            dimension_semantics=("arbitrary", "arbitrary"), vmem_limit_bytes=VMEM_LIMIT),
        interpret=interpret,
        name="ffn",
    )(h, oa, ob, wo, g2, wg, wu, wd, gf)


def _layer(h, attn_norm_g, w_in, mla_q_norm_g, w_uq, mla_kv_norm_g, w_ukv, w_o, ffn_norm_g,
           w_gate, w_up, w_down, out_g, topk, *, tm_proj, tq_mla, tq_dsa, tm_ffn, interpret=False):
    tp = h.shape[1]
    win = _take_cols(w_in, _w_in_cols()).astype(BF16)
    wuq = _take_cols(w_uq, _w_uq_cols()).astype(BF16)
    wukv = jnp.take(w_ukv, jnp.asarray(_w_ukv_cols()), axis=1).astype(BF16)
    tables = _rope_tables(tp)
    qmla, kmla, vmla, qs, ks, vs, qi, ki, wi = _proj_call(
        h, attn_norm_g[None], win, mla_q_norm_g[None], wuq, mla_kv_norm_g[None], wukv, tables,
        tm=tm_proj, interpret=interpret)
    o_mla = _mla_call(qmla, kmla, vmla, tq=tq_mla, interpret=interpret)
    o_dsa = _dsa_call(qi, wi, qs, ki, ks, vs, tq=tq_dsa, topk=topk, interpret=interpret)
    return _ffn_call(h, o_mla, o_dsa, w_o.astype(BF16), ffn_norm_g[None], w_gate.astype(BF16),
                     w_up.astype(BF16), w_down.astype(BF16), out_g[None], tm=tm_ffn, interpret=interpret)


def kernel(x, meta_tokens, attn_norm_g, w_in, mla_q_norm_g, w_uq, mla_kv_norm_g, w_ukv, w_o,
           ffn_norm_g, w_gate, w_up, w_down, final_norm_g):
    b, seq, d = x.shape
    depth = w_in.shape[0]
    assert depth == 1, "the final norm is fused into the layer's last kernel"
    topk = min(TOPK_MAX, seq // 4)
    meta = jnp.broadcast_to(meta_tokens[None].astype(x.dtype), (b, N_META, d))
    t = N_META + seq
    tp = -(-t // ROW_PAD) * ROW_PAD
    h = jnp.concatenate([meta, x, jnp.zeros((b, tp - t, d), x.dtype)], axis=1)
    out = _layer(h, attn_norm_g[0], w_in[0], mla_q_norm_g[0], w_uq[0], mla_kv_norm_g[0], w_ukv[0],
                 w_o[0], ffn_norm_g[0], w_gate[0], w_up[0], w_down[0], final_norm_g, topk,
                 tm_proj=384, tq_mla=ROW_PAD, tq_dsa=ROW_PAD, tm_ffn=384)
    return out[:, N_META:t]
```

```python
import functools

import numpy as np
import jax
import jax.numpy as jnp
from jax import lax
from jax.experimental import pallas as pl
from jax.experimental.pallas import tpu as pltpu

F32 = jnp.float32
BF16 = jnp.bfloat16
I32 = jnp.int32

D_MODEL = 1024
N_META = 16
ROPE_THETA = 10000.0
EPS = 1e-6
MLA_HEADS = 8
MLA_Q_RANK = 256
MLA_KV_RANK = 128
MLA_NOPE_DIM = 64
MLA_ROPE_DIM = 32
MLA_QK_DIM = MLA_NOPE_DIM + MLA_ROPE_DIM
MLA_V_DIM = 64
DSA_HEADS = 8
DSA_HEAD_DIM = 64
IDX_HEADS = 8
IDX_DIM = 64
TOPK_MAX = 256
D_FF = 2816

LANES = 128
N_PAIRS = 4
NEG_BIG = -1e30
INT_MIN = -2 ** 31
VMEM_LIMIT = 58 * 1024 * 1024
TQ = 256
TK = 512
TM = 512

_OFF_CQ, _OFF_CKV, _OFF_KR, _OFF_QS, _OFF_KS, _OFF_VS, _OFF_QI, _OFF_KI, _OFF_WI = (
    0, 256, 384, 416, 928, 1440, 1952, 2464, 2528)
_C_IN = 2536
_P_CQ, _P_CKV, _P_QS, _P_KS, _P_VS, _P_QI, _P_KI, _P_KR, _P_WI, _P_END = (
    0, 256, 384, 896, 1408, 1920, 2432, 2560, 2688, 2816)


def _pair_rope_cols(base, hd=64):
    half = hd // 2
    cols = []
    for p in range(N_PAIRS):
        a, b = 2 * p, 2 * p + 1
        cols += [base + a * hd + d for d in range(half)]
        cols += [base + b * hd + d for d in range(half)]
        cols += [base + a * hd + half + d for d in range(half)]
        cols += [base + b * hd + half + d for d in range(half)]
    return cols


def _w_in_cols():
    z = _C_IN
    cols = list(range(_OFF_CQ, _OFF_CQ + 256)) + list(range(_OFF_CKV, _OFF_CKV + 128))
    cols += _pair_rope_cols(_OFF_QS) + _pair_rope_cols(_OFF_KS)
    cols += list(range(_OFF_VS, _OFF_VS + 512))
    cols += _pair_rope_cols(_OFF_QI)
    ki1 = [_OFF_KI + d for d in range(32)]
    ki2 = [_OFF_KI + 32 + d for d in range(32)]
    cols += ki1 + ki1 + ki2 + ki2
    kr1 = [_OFF_KR + d for d in range(16)]
    kr2 = [_OFF_KR + 16 + d for d in range(16)]
    cols += kr1 + kr1 + [z] * 32 + kr2 + kr2 + [z] * 32
    cols += [_OFF_WI + d for d in range(IDX_HEADS)] + [z] * (LANES - IDX_HEADS)
    assert len(cols) == _P_END
    return np.asarray(cols, np.int32)


def _w_uq_cols():
    z = MLA_HEADS * MLA_QK_DIM
    cols = []
    for p in range(N_PAIRS):
        a, b = 2 * p, 2 * p + 1
        cols += [a * MLA_QK_DIM + d for d in range(64)] + [b * MLA_QK_DIM + d for d in range(64)]
        cols += [a * MLA_QK_DIM + 64 + d for d in range(16)] + [b * MLA_QK_DIM + 64 + d for d in range(16)]
        cols += [z] * 32
        cols += [a * MLA_QK_DIM + 80 + d for d in range(16)] + [b * MLA_QK_DIM + 80 + d for d in range(16)]
        cols += [z] * 32
    return np.asarray(cols, np.int32)


def _w_ukv_cols():
    cols = [h * 128 + d for h in range(MLA_HEADS) for d in range(64)]
    cols += [h * 128 + 64 + d for h in range(MLA_HEADS) for d in range(64)]
    return np.asarray(cols, np.int32)


def _take_cols(w, cols):
    wz = jnp.concatenate([w, jnp.zeros((w.shape[0], 1), w.dtype)], axis=1)
    return jnp.take(wz, jnp.asarray(cols), axis=1)


def _rope_tables(tp):
    pos = jnp.arange(tp, dtype=jnp.int32).astype(F32)

    def cs(half):
        inv = jnp.power(ROPE_THETA, -jnp.arange(half, dtype=F32) / half)
        ang = pos[:, None] * inv[None, :]
        return jnp.cos(ang), jnp.sin(ang)

    c32, s32 = cs(32)
    c16, s16 = cs(16)
    z = jnp.zeros((tp, 32), F32)
    c64t = jnp.concatenate([c32, c32, c32, c32], axis=1)
    s64t = jnp.concatenate([-s32, -s32, s32, s32], axis=1)
    c32t = jnp.concatenate([c16, c16, z, c16, c16, z], axis=1)
    s32t = jnp.concatenate([-s16, -s16, z, s16, s16, z], axis=1)
    return c64t, s64t, c32t, s32t


def _rms(x, g):
    return x * lax.rsqrt(jnp.mean(x * x, axis=-1, keepdims=True) + EPS) * g


def _rope_slab(x, c, s):
    return x * c + pltpu.roll(x, 64, axis=1) * s


def _proj_kernel(h_ref, g_ref, win_ref, gq_ref, wuq_ref, gkv_ref, wukv_ref,
                 c64_ref, s64_ref, c32_ref, s32_ref,
                 qmla_ref, kmla_ref, vmla_ref, qs_ref, ks_ref, vs_ref, qi_ref, ki_ref, wi_ref):
    u = _rms(h_ref[0], g_ref[...]).astype(BF16)
    proj = jnp.dot(u, win_ref[...], preferred_element_type=F32)
    c64, s64, c32, s32 = c64_ref[...], s64_ref[...], c32_ref[...], s32_ref[...]

    cq = _rms(proj[:, _P_CQ:_P_CQ + 256], gq_ref[...]).astype(BF16)
    q = jnp.dot(cq, wuq_ref[...], preferred_element_type=F32)
    ckv = _rms(proj[:, _P_CKV:_P_CKV + 128], gkv_ref[...]).astype(BF16)
    kv = jnp.dot(ckv, wukv_ref[...], preferred_element_type=F32)
    kr = _rope_slab(proj[:, _P_KR:_P_KR + 128], c32, s32).astype(BF16)
    q_scale = MLA_QK_DIM ** -0.5
    s_scale = DSA_HEAD_DIM ** -0.5
    i_scale = IDX_DIM ** -0.5
    for p in range(N_PAIRS):
        qn = q[:, p * 256:p * 256 + 128] * q_scale
        qr = _rope_slab(q[:, p * 256 + 128:p * 256 + 256], c32, s32) * q_scale
        qmla_ref[0, p, :, 0:128] = qn.astype(BF16)
        qmla_ref[0, p, :, 128:256] = qr.astype(BF16)
        kmla_ref[0, p, :, 0:128] = kv[:, p * 128:(p + 1) * 128].astype(BF16)
        kmla_ref[0, p, :, 128:256] = kr
        vmla_ref[0, p] = kv[:, 512 + p * 128:512 + (p + 1) * 128].astype(BF16)
        sl = slice(p * 128, (p + 1) * 128)
        qs_ref[0, p] = (_rope_slab(proj[:, _P_QS:_P_KS][:, sl], c64, s64) * s_scale).astype(BF16)
        ks_ref[0, p] = _rope_slab(proj[:, _P_KS:_P_VS][:, sl], c64, s64).astype(BF16)
        vs_ref[0, p] = proj[:, _P_VS:_P_QI][:, sl].astype(BF16)
        qi_ref[0, p] = (_rope_slab(proj[:, _P_QI:_P_KI][:, sl], c64, s64) * i_scale).astype(BF16)
    ki_ref[0] = _rope_slab(proj[:, _P_KI:_P_KR], c64, s64).astype(BF16)
    wi_ref[0] = proj[:, _P_WI:_P_END] * (IDX_HEADS ** -0.5)


def _proj_call(h, g, win, gq, wuq, gkv, wukv, tables, *, tm, interpret=False):
    b, tp, d = h.shape
    nt = tp // tm
    const = lambda shape: pl.BlockSpec(shape, lambda bi, i: (0,) * len(shape))
    tab = pl.BlockSpec((tm, LANES), lambda bi, i: (i, 0))
    pair = lambda w: pl.BlockSpec((1, N_PAIRS, tm, w), lambda bi, i: (bi, 0, i, 0))
    row = lambda w: pl.BlockSpec((1, tm, w), lambda bi, i: (bi, i, 0))
    sds = jax.ShapeDtypeStruct
    out_shape = (
        sds((b, N_PAIRS, tp, 256), BF16), sds((b, N_PAIRS, tp, 256), BF16), sds((b, N_PAIRS, tp, 128), BF16),
        sds((b, N_PAIRS, tp, 128), BF16), sds((b, N_PAIRS, tp, 128), BF16), sds((b, N_PAIRS, tp, 128), BF16),
        sds((b, N_PAIRS, tp, 128), BF16), sds((b, tp, 128), BF16), sds((b, tp, 128), F32))
    return pl.pallas_call(
        _proj_kernel,
        grid=(b, nt),
        in_specs=[row(d), const((1, d)), const(win.shape), const((1, MLA_Q_RANK)), const(wuq.shape),
                  const((1, MLA_KV_RANK)), const(wukv.shape), tab, tab, tab, tab],
        out_specs=(pair(256), pair(256), pair(128), pair(128), pair(128), pair(128), pair(128),
                   row(128), row(128)),
        out_shape=out_shape,
        compiler_params=pltpu.CompilerParams(
            dimension_semantics=("arbitrary", "arbitrary"), vmem_limit_bytes=VMEM_LIMIT),
        interpret=interpret,
        name="proj",
    )(h, g, win, gq, wuq, gkv, wukv, *tables)


_NT = (((1,), (1,)), ((), ()))


def _stage_scores(qm_scr, k, s_ref):
    s_ref[...] = lax.dot_general(qm_scr[...], k, _NT, preferred_element_type=F32)


def _stage_softmax(s, p_ref, a_ref, m_scr, l_scr):
    m_prev = m_scr[...]
    m_new = jnp.maximum(m_prev, jnp.max(s, axis=-1, keepdims=True))
    alpha = jnp.exp(m_prev - m_new)
    p = jnp.exp(s - jnp.concatenate([m_new] * (s.shape[1] // LANES), axis=1))
    l_scr[...] = alpha * l_scr[...] + jnp.sum(p, axis=-1, keepdims=True)
    m_scr[...] = m_new
    p_ref[...] = p.astype(BF16)
    a_ref[...] = alpha


def _stage_pv(p_ref, a_ref, v, acc_scr):
    acc_scr[...] = a_ref[...] * acc_scr[...] + jnp.dot(p_ref[...], v, preferred_element_type=F32)


def _flash_init(m_scr, l_scr, acc_scr, p1, a1):
    m_scr[...] = jnp.full(m_scr.shape, NEG_BIG, F32)
    l_scr[...] = jnp.zeros(l_scr.shape, F32)
    acc_scr[...] = jnp.zeros(acc_scr.shape, F32)
    p1[...] = jnp.zeros(p1.shape, BF16)
    a1[...] = jnp.ones(a1.shape, F32)


def _flash_finish(o_ref, l_scr, acc_scr, tq):
    lane = lax.broadcasted_iota(I32, (tq, LANES), 1)
    oa = acc_scr[0:tq] / l_scr[0:tq]
    ob = acc_scr[tq:2 * tq] / l_scr[tq:2 * tq]
    o_ref[0] = jnp.where(lane < 64, oa, ob).astype(o_ref.dtype)


def _chunk_counts(i, tq):
    n_chunks = ((i + 1) * tq + TK - 1) // TK
    return n_chunks, (n_chunks + 1) // 2


_FLASH_SCRATCH = lambda tq: [
    pltpu.VMEM((2 * tq, TK), F32), pltpu.VMEM((2 * tq, TK), F32),
    pltpu.VMEM((2 * tq, TK), BF16), pltpu.VMEM((2 * tq, TK), BF16),
    pltpu.VMEM((2 * tq, LANES), F32), pltpu.VMEM((2 * tq, LANES), F32),
    pltpu.VMEM((2 * tq, LANES), F32), pltpu.VMEM((2 * tq, LANES), F32),
    pltpu.VMEM((2 * tq, LANES), F32)]


def _mla_kernel(q_ref, k_ref, v_ref, o_ref, qm_scr, s0, s1, p0, p1, a0, a1, m_scr, l_scr, acc_scr,
                *, tq, n_q_blocks):
    i = pl.program_id(2)

    @pl.when(i >= n_q_blocks)
    def _pad_rows():
        o_ref[...] = jnp.zeros(o_ref.shape, o_ref.dtype)

    @pl.when(i < n_q_blocks)
    def _attend():
        q = q_ref[0, 0]
        lane = lax.broadcasted_iota(I32, (tq, 256), 1)
        in_a = (lane < 64) | ((lane >= 128) & (lane < 144)) | ((lane >= 192) & (lane < 208))
        in_b = ((lane >= 64) & (lane < 128)) | ((lane >= 144) & (lane < 160)) | ((lane >= 208) & (lane < 224))
        zero = jnp.zeros_like(q)
        qm_scr[0:tq] = jnp.where(in_a, q, zero)
        qm_scr[tq:2 * tq] = jnp.where(in_b, q, zero)
        _flash_init(m_scr, l_scr, acc_scr, p1, a1)
        n_chunks, n_pairs = _chunk_counts(i, tq)

        def rows(c):
            cc = jnp.clip(c, 0, n_chunks - 1)
            return pl.ds(pl.multiple_of(cc * TK, TK), TK)

        k_at = lambda c: k_ref[0, 0, rows(c), :]
        v_at = lambda c: v_ref[0, 0, rows(c), :]

        def masked(s_ref, c):
            r = (lax.broadcasted_iota(I32, (2 * tq, TK), 0) & (tq - 1)) + i * tq
            col = lax.broadcasted_iota(I32, (2 * tq, TK), 1) + c * TK
            return jnp.where(col <= r, s_ref[...], NEG_BIG)

        _stage_scores(qm_scr, k_at(0), s0)

        def pair(j, carry):
            c = 2 * j
            _stage_pv(p1, a1, v_at(c - 1), acc_scr)
            _stage_softmax(s0[...], p0, a0, m_scr, l_scr)
            _stage_scores(qm_scr, k_at(c + 1), s1)
            _stage_pv(p0, a0, v_at(c), acc_scr)
            _stage_softmax(s1[...], p1, a1, m_scr, l_scr)
            _stage_scores(qm_scr, k_at(c + 2), s0)
            return carry

        lax.fori_loop(0, n_pairs - 1, pair, 0)
        c = 2 * (n_pairs - 1)
        _stage_pv(p1, a1, v_at(c - 1), acc_scr)
        _stage_softmax(masked(s0, c), p0, a0, m_scr, l_scr)
        _stage_scores(qm_scr, k_at(c + 1), s1)
        _stage_pv(p0, a0, v_at(c), acc_scr)
        _stage_softmax(masked(s1, c + 1), p1, a1, m_scr, l_scr)
        _stage_pv(p1, a1, v_at(c + 1), acc_scr)
        _flash_finish(o_ref, l_scr, acc_scr, tq)


def _mla_call(qmla, kmla, vmla, *, tq, n_q_blocks, interpret=False):
    b, _, tp, _ = qmla.shape
    return pl.pallas_call(
        functools.partial(_mla_kernel, tq=tq, n_q_blocks=n_q_blocks),
        grid=(b, N_PAIRS, tp // tq),
        in_specs=[pl.BlockSpec((1, 1, tq, 256), lambda bi, p, i: (bi, p, i, 0)),
                  pl.BlockSpec((1, 1, tp, 256), lambda bi, p, i: (bi, p, 0, 0)),
                  pl.BlockSpec((1, 1, tp, 128), lambda bi, p, i: (bi, p, 0, 0))],
        out_specs=pl.BlockSpec((1, tq, LANES), lambda bi, p, i: (bi, i, p)),
        out_shape=jax.ShapeDtypeStruct((b, tp, N_PAIRS * LANES), BF16),
        scratch_shapes=[pltpu.VMEM((2 * tq, 256), BF16)] + _FLASH_SCRATCH(tq),
        compiler_params=pltpu.CompilerParams(
            dimension_semantics=("arbitrary", "arbitrary", "arbitrary"), vmem_limit_bytes=VMEM_LIMIT),
        interpret=interpret,
        name="mla_attn",
    )(qmla, kmla, vmla)


def _dsa_kernel(qi_ref, wi_ref, qs_ref, ki_ref, ks_ref, vs_ref, o_ref,
                key_scr, qim_scr, wb_scr, thr_scr, cut_scr,
                qm_scr, s0, s1, p0, p1, a0, a1, m_scr, l_scr, acc_scr, *, tq, topk, n_q_blocks):
    i = pl.program_id(1)
    p = pl.program_id(2)
    sub = TK // LANES
    n_chunks, n_pairs = _chunk_counts(i, tq)
    n128 = n_chunks * sub
    lane = lax.broadcasted_iota(I32, (tq, LANES), 1)
    in_a = (lane < 32) | ((lane >= 64) & (lane < 96))

    @pl.when(i >= n_q_blocks)
    def _pad_rows():
        o_ref[...] = jnp.zeros(o_ref.shape, o_ref.dtype)

    @pl.when((i < n_q_blocks) & (p == 0))
    def _select():
        for pp in range(N_PAIRS):
            qp = qi_ref[0, pp]
            zero = jnp.zeros_like(qp)
            qim_scr[(2 * pp) * tq:(2 * pp + 1) * tq] = jnp.where(in_a, qp, zero)
            qim_scr[(2 * pp + 1) * tq:(2 * pp + 2) * tq] = jnp.where(in_a, zero, qp)
        w = wi_ref[0]
        for hh in range(IDX_HEADS):
            wb_scr[hh] = jnp.broadcast_to(w[:, hh:hh + 1], (tq, LANES))
        half = TK // 2

        def logits(c2):
            kc = ki_ref[0, pl.ds(pl.multiple_of(c2 * half, half), half), :]
            return lax.dot_general(qim_scr[...], kc, _NT, preferred_element_type=F32)

        def to_keys(lg, c2):
            sc = jnp.zeros((tq, half), F32)
            for hh in range(IDX_HEADS):
                wb = jnp.concatenate([wb_scr[hh]] * (half // LANES), axis=1)
                sc = sc + wb * jnp.maximum(lg[hh * tq:(hh + 1) * tq], 0.0)
            bits = pltpu.bitcast(sc, I32)
            key = bits ^ ((bits >> 31) & 0x7FFFFFFF)
            key = key - (key >> 31)
            r2 = lax.broadcasted_iota(I32, (tq, half), 0) + i * tq
            c2i = lax.broadcasted_iota(I32, (tq, half), 1) + c2 * half
            key = jnp.where(c2i <= r2, key, INT_MIN)
            for t in range(half // LANES):
                key_scr[c2 * (half // LANES) + t] = key[:, t * LANES:(t + 1) * LANES]

        def score_chunk(c, carry):
            lg0 = logits(2 * c)
            lg1 = logits(2 * c + 1)
            to_keys(lg0, 2 * c)
            to_keys(lg1, 2 * c + 1)
            return carry

        lax.fori_loop(0, n_chunks, score_chunk, 0)

        rg = tq // 2
        lane_g = lax.broadcasted_iota(I32, (rg, LANES), 1)

        def count(pred, r0):
            def body(c4, acc):
                for u in range(sub):
                    c = c4 * sub + u
                    acc = acc + pred(key_scr[c, r0:r0 + rg, :], c).astype(I32)
                return acc
            acc = lax.fori_loop(0, n_chunks, body, jnp.zeros((rg, LANES), I32))
            return jnp.sum(acc, axis=-1, keepdims=True)

        for r0 in range(0, tq, rg):
            def bit_step(it, t):
                cand = t + lax.shift_left(jnp.int32(1), (31 - it).astype(I32))
                cnt = count(lambda kk, c: kk >= cand, r0)
                return jnp.where(cnt >= topk, cand, t)

            thr = lax.fori_loop(0, 32, bit_step, jnp.full((rg, LANES), INT_MIN, I32))
            thr_scr[r0:r0 + rg] = thr
            n_gt = count(lambda kk, c: kk > thr, r0)
            n_ge = count(lambda kk, c: kk >= thr, r0)
            need = topk - n_gt
            cut_scr[r0:r0 + rg] = jnp.full((rg, LANES), 2 ** 30, I32)
            excess = jnp.max(jnp.where((n_ge > topk) & (thr[:, 0:1] > INT_MIN), 1, 0))

            @pl.when(excess > 0)
            def _ties():
                def idx_step(it, x):
                    cand = x + lax.shift_left(jnp.int32(1), (13 - it).astype(I32))
                    cnt = count(lambda kk, c: (kk == thr) & ((lane_g + c * LANES) < cand), r0)
                    return jnp.where(cnt < need, cand, x)

                cut_scr[r0:r0 + rg] = lax.fori_loop(0, 14, idx_step, jnp.zeros((rg, LANES), I32))

        def to_bias(c, carry):
            kk = key_scr[c]
            thr = thr_scr[...]
            sel = (kk > thr) | ((kk == thr) & ((lane + c * LANES) <= cut_scr[...]) & (thr > INT_MIN))
            key_scr[c] = pltpu.bitcast(jnp.where(sel, 0.0, NEG_BIG).astype(F32), I32)
            return carry

        lax.fori_loop(0, n128, to_bias, 0)

        def fill(c, carry):
            key_scr[c] = pltpu.bitcast(jnp.full((tq, LANES), NEG_BIG, F32), I32)
            return carry

        lax.fori_loop(n128, 2 * n_pairs * sub, fill, 0)

    @pl.when(i < n_q_blocks)
    def _attend():
        q = qs_ref[0, p]
        zero = jnp.zeros_like(q)
        qm_scr[0:tq] = jnp.where(in_a, q, zero)
        qm_scr[tq:2 * tq] = jnp.where(in_a, zero, q)
        _flash_init(m_scr, l_scr, acc_scr, p1, a1)

        def rows(c):
            cc = jnp.clip(c, 0, n_chunks - 1)
            return pl.ds(pl.multiple_of(cc * TK, TK), TK)

        k_at = lambda c: ks_ref[0, p, rows(c), :]
        v_at = lambda c: vs_ref[0, p, rows(c), :]

        def biased(s_ref, c):
            bias = jnp.concatenate([pltpu.bitcast(key_scr[c * sub + t], F32) for t in range(sub)], axis=1)
            return s_ref[...] + jnp.concatenate([bias, bias], axis=0)

        _stage_scores(qm_scr, k_at(0), s0)

        def pair(j, carry):
            c = 2 * j
            _stage_pv(p1, a1, v_at(c - 1), acc_scr)
            _stage_softmax(biased(s0, c), p0, a0, m_scr, l_scr)
            _stage_scores(qm_scr, k_at(c + 1), s1)
            _stage_pv(p0, a0, v_at(c), acc_scr)
            _stage_softmax(biased(s1, c + 1), p1, a1, m_scr, l_scr)
            _stage_scores(qm_scr, k_at(c + 2), s0)
            return carry

        lax.fori_loop(0, n_pairs, pair, 0)
        _stage_pv(p1, a1, v_at(2 * n_pairs - 1), acc_scr)
        _flash_finish(o_ref, l_scr, acc_scr, tq)


def _dsa_call(qi, wi, qs, ki, ks, vs, *, tq, topk, n_q_blocks, interpret=False):
    b, _, tp, _ = qs.shape
    once = pl.Buffered(1)
    max_pairs = (tp // TK + 1) // 2
    return pl.pallas_call(
        functools.partial(_dsa_kernel, tq=tq, topk=topk, n_q_blocks=n_q_blocks),
        grid=(b, tp // tq, N_PAIRS),
        in_specs=[pl.BlockSpec((1, N_PAIRS, tq, LANES), lambda bi, i, p: (bi, 0, i, 0)),
                  pl.BlockSpec((1, tq, LANES), lambda bi, i, p: (bi, i, 0)),
                  pl.BlockSpec((1, N_PAIRS, tq, LANES), lambda bi, i, p: (bi, 0, i, 0)),
                  pl.BlockSpec((1, tp, LANES), lambda bi, i, p: (bi, 0, 0), pipeline_mode=once),
                  pl.BlockSpec((1, N_PAIRS, tp, LANES), lambda bi, i, p: (bi, 0, 0, 0), pipeline_mode=once),
                  pl.BlockSpec((1, N_PAIRS, tp, LANES), lambda bi, i, p: (bi, 0, 0, 0), pipeline_mode=once)],
        out_specs=pl.BlockSpec((1, tq, LANES), lambda bi, i, p: (bi, i, p)),
        out_shape=jax.ShapeDtypeStruct((b, tp, N_PAIRS * LANES), BF16),
        scratch_shapes=[pltpu.VMEM((2 * max_pairs * (TK // LANES), tq, LANES), I32),
                        pltpu.VMEM((IDX_HEADS * tq, LANES), BF16),
                        pltpu.VMEM((IDX_HEADS, tq, LANES), F32),
                        pltpu.VMEM((tq, LANES), I32),
                        pltpu.VMEM((tq, LANES), I32),
                        pltpu.VMEM((2 * tq, LANES), BF16)] + _FLASH_SCRATCH(tq),
        compiler_params=pltpu.CompilerParams(
            dimension_semantics=("arbitrary", "arbitrary", "arbitrary"), vmem_limit_bytes=VMEM_LIMIT),
        interpret=interpret,
        name="dsa_attn",
    )(qi, wi, qs, ki, ks, vs)


def _ffn_kernel(h_ref, oa_ref, ob_ref, wo_ref, g2_ref, wg_ref, wu_ref, wd_ref, gf_ref, out_ref):
    o = jnp.concatenate([oa_ref[0], ob_ref[0]], axis=1)
    h1 = h_ref[0] + jnp.dot(o, wo_ref[...], preferred_element_type=F32)
    u = _rms(h1, g2_ref[...]).astype(BF16)
    gate = jnp.dot(u, wg_ref[...], preferred_element_type=F32)
    up = jnp.dot(u, wu_ref[...], preferred_element_type=F32)
    act = (gate * jax.nn.sigmoid(gate) * up).astype(BF16)
    h2 = h1 + jnp.dot(act, wd_ref[...], preferred_element_type=F32)
    out_ref[0] = _rms(h2, gf_ref[...])


def _ffn_call(h, oa, ob, wo, g2, wg, wu, wd, gf, *, tm, interpret=False):
    b, tp, d = h.shape
    once = pl.Buffered(1)
    const = lambda shape: pl.BlockSpec(shape, lambda bi, i: (0,) * len(shape), pipeline_mode=once)
    row = lambda w: pl.BlockSpec((1, tm, w), lambda bi, i: (bi, i, 0))
    return pl.pallas_call(
        _ffn_kernel,
        grid=(b, tp // tm),
        in_specs=[row(d), row(oa.shape[-1]), row(ob.shape[-1]), const(wo.shape), const((1, d)),
                  const(wg.shape), const(wu.shape), const(wd.shape), const((1, d))],
        out_specs=row(d),
        out_shape=jax.ShapeDtypeStruct((b, tp, d), F32),
        compiler_params=pltpu.CompilerParams(
            dimension_semantics=("arbitrary", "arbitrary"), vmem_limit_bytes=VMEM_LIMIT),
        interpret=interpret,
        name="ffn",
    )(h, oa, ob, wo, g2, wg, wu, wd, gf)


def _layer(h, t_real, attn_norm_g, w_in, mla_q_norm_g, w_uq, mla_kv_norm_g, w_ukv, w_o, ffn_norm_g,
           w_gate, w_up, w_down, out_g, topk, *, tm, tq, interpret=False):
    tp = h.shape[1]
    n_q_blocks = -(-t_real // tq)
    win = _take_cols(w_in, _w_in_cols()).astype(BF16)
    wuq = _take_cols(w_uq, _w_uq_cols()).astype(BF16)
    wukv = jnp.take(w_ukv, jnp.asarray(_w_ukv_cols()), axis=1).astype(BF16)
    tables = _rope_tables(tp)
    qmla, kmla, vmla, qs, ks, vs, qi, ki, wi = _proj_call(
        h, attn_norm_g[None], win, mla_q_norm_g[None], wuq, mla_kv_norm_g[None], wukv, tables,
        tm=tm, interpret=interpret)
    o_mla = _mla_call(qmla, kmla, vmla, tq=tq, n_q_blocks=n_q_blocks, interpret=interpret)
    o_dsa = _dsa_call(qi, wi, qs, ki, ks, vs, tq=tq, topk=topk, n_q_blocks=n_q_blocks, interpret=interpret)
    return _ffn_call(h, o_mla, o_dsa, w_o.astype(BF16), ffn_norm_g[None], w_gate.astype(BF16),
                     w_up.astype(BF16), w_down.astype(BF16), out_g[None], tm=tm, interpret=interpret)


def kernel(x, meta_tokens, attn_norm_g, w_in, mla_q_norm_g, w_uq, mla_kv_norm_g, w_ukv, w_o,
           ffn_norm_g, w_gate, w_up, w_down, final_norm_g):
    b, seq, d = x.shape
    depth = w_in.shape[0]
    assert depth == 1, "the final norm is fused into the layer's last kernel"
    topk = min(TOPK_MAX, seq // 4)
    meta = jnp.broadcast_to(meta_tokens[None].astype(x.dtype), (b, N_META, d))
    t = N_META + seq
    tp = -(-t // TK) * TK
    h = jnp.concatenate([meta, x, jnp.zeros((b, tp - t, d), x.dtype)], axis=1)
    out = _layer(h, t, attn_norm_g[0], w_in[0], mla_q_norm_g[0], w_uq[0], mla_kv_norm_g[0], w_ukv[0],
                 w_o[0], ffn_norm_g[0], w_gate[0], w_up[0], w_down[0], final_norm_g, topk,
                 tm=TM, tq=TQ)
    return out[:, N_META:t]
```

```python
import functools

import numpy as np
import jax
import jax.numpy as jnp
from jax import lax
from jax.experimental import pallas as pl
from jax.experimental.pallas import tpu as pltpu

F32 = jnp.float32
BF16 = jnp.bfloat16
I32 = jnp.int32
I16 = jnp.int16

D_MODEL = 1024
N_META = 16
ROPE_THETA = 10000.0
EPS = 1e-6
MLA_HEADS = 8
MLA_Q_RANK = 256
MLA_KV_RANK = 128
MLA_NOPE_DIM = 64
MLA_ROPE_DIM = 32
MLA_QK_DIM = MLA_NOPE_DIM + MLA_ROPE_DIM
MLA_V_DIM = 64
DSA_HEADS = 8
DSA_HEAD_DIM = 64
IDX_HEADS = 8
IDX_DIM = 64
TOPK_MAX = 256
D_FF = 2816

LANES = 128
N_PAIRS = 4
NEG_BIG = -1e30
INT_MIN = -2 ** 31
INT16_MIN = -2 ** 15
VMEM_LIMIT = 58 * 1024 * 1024
TQ = 256
TK = 512
TM = 512

_OFF_CQ, _OFF_CKV, _OFF_KR, _OFF_QS, _OFF_KS, _OFF_VS, _OFF_QI, _OFF_KI, _OFF_WI = (
    0, 256, 384, 416, 928, 1440, 1952, 2464, 2528)
_C_IN = 2536
_P_CQ, _P_CKV, _P_QS, _P_KS, _P_VS, _P_QI, _P_KI, _P_KR, _P_WI, _P_END = (
    0, 256, 384, 896, 1408, 1920, 2432, 2560, 2688, 2816)


def _pair_rope_cols(base, hd=64):
    half = hd // 2
    cols = []
    for p in range(N_PAIRS):
        a, b = 2 * p, 2 * p + 1
        cols += [base + a * hd + d for d in range(half)]
        cols += [base + b * hd + d for d in range(half)]
        cols += [base + a * hd + half + d for d in range(half)]
        cols += [base + b * hd + half + d for d in range(half)]
    return cols


def _w_in_cols():
    z = _C_IN
    cols = list(range(_OFF_CQ, _OFF_CQ + 256)) + list(range(_OFF_CKV, _OFF_CKV + 128))
    cols += _pair_rope_cols(_OFF_QS) + _pair_rope_cols(_OFF_KS)
    cols += list(range(_OFF_VS, _OFF_VS + 512))
    cols += _pair_rope_cols(_OFF_QI)
    ki1 = [_OFF_KI + d for d in range(32)]
    ki2 = [_OFF_KI + 32 + d for d in range(32)]
    cols += ki1 + ki1 + ki2 + ki2
    kr1 = [_OFF_KR + d for d in range(16)]
    kr2 = [_OFF_KR + 16 + d for d in range(16)]
    cols += kr1 + kr1 + [z] * 32 + kr2 + kr2 + [z] * 32
    cols += [_OFF_WI + d for d in range(IDX_HEADS)] + [z] * (LANES - IDX_HEADS)
    assert len(cols) == _P_END
    return np.asarray(cols, np.int32)


def _w_uq_cols():
    z = MLA_HEADS * MLA_QK_DIM
    cols = []
    for p in range(N_PAIRS):
        a, b = 2 * p, 2 * p + 1
        cols += [a * MLA_QK_DIM + d for d in range(64)] + [b * MLA_QK_DIM + d for d in range(64)]
        cols += [a * MLA_QK_DIM + 64 + d for d in range(16)] + [b * MLA_QK_DIM + 64 + d for d in range(16)]
        cols += [z] * 32
        cols += [a * MLA_QK_DIM + 80 + d for d in range(16)] + [b * MLA_QK_DIM + 80 + d for d in range(16)]
        cols += [z] * 32
    return np.asarray(cols, np.int32)


def _w_ukv_cols():
    cols = [h * 128 + d for h in range(MLA_HEADS) for d in range(64)]
    cols += [h * 128 + 64 + d for h in range(MLA_HEADS) for d in range(64)]
    return np.asarray(cols, np.int32)


def _take_cols(w, cols):
    wz = jnp.concatenate([w, jnp.zeros((w.shape[0], 1), w.dtype)], axis=1)
    return jnp.take(wz, jnp.asarray(cols), axis=1)


def _rope_tables(tp):
    pos = jnp.arange(tp, dtype=jnp.int32).astype(F32)

    def cs(half):
        inv = jnp.power(ROPE_THETA, -jnp.arange(half, dtype=F32) / half)
        ang = pos[:, None] * inv[None, :]
        return jnp.cos(ang), jnp.sin(ang)

    c32, s32 = cs(32)
    c16, s16 = cs(16)
    z = jnp.zeros((tp, 32), F32)
    c64t = jnp.concatenate([c32, c32, c32, c32], axis=1)
    s64t = jnp.concatenate([-s32, -s32, s32, s32], axis=1)
    c32t = jnp.concatenate([c16, c16, z, c16, c16, z], axis=1)
    s32t = jnp.concatenate([-s16, -s16, z, s16, s16, z], axis=1)
    return c64t, s64t, c32t, s32t


def _rms(x, g):
    return x * lax.rsqrt(jnp.mean(x * x, axis=-1, keepdims=True) + EPS) * g


def _rope_slab(x, c, s):
    return x * c + pltpu.roll(x, 64, axis=1) * s


def _proj_kernel(h_ref, g_ref, win_ref, gq_ref, wuq_ref, gkv_ref, wukv_ref,
                 c64_ref, s64_ref, c32_ref, s32_ref,
                 qmla_ref, kmla_ref, vmla_ref, qs_ref, ks_ref, vs_ref, qi_ref, ki_ref, wi_ref):
    u = _rms(h_ref[0], g_ref[...]).astype(BF16)
    proj = jnp.dot(u, win_ref[...], preferred_element_type=F32)
    c64, s64, c32, s32 = c64_ref[...], s64_ref[...], c32_ref[...], s32_ref[...]

    cq = _rms(proj[:, _P_CQ:_P_CQ + 256], gq_ref[...]).astype(BF16)
    q = jnp.dot(cq, wuq_ref[...], preferred_element_type=F32)
    ckv = _rms(proj[:, _P_CKV:_P_CKV + 128], gkv_ref[...]).astype(BF16)
    kv = jnp.dot(ckv, wukv_ref[...], preferred_element_type=F32)
    kr = _rope_slab(proj[:, _P_KR:_P_KR + 128], c32, s32).astype(BF16)
    q_scale = MLA_QK_DIM ** -0.5
    s_scale = DSA_HEAD_DIM ** -0.5
    i_scale = IDX_DIM ** -0.5
    for p in range(N_PAIRS):
        qn = q[:, p * 256:p * 256 + 128] * q_scale
        qr = _rope_slab(q[:, p * 256 + 128:p * 256 + 256], c32, s32) * q_scale
        qmla_ref[0, p, :, 0:128] = qn.astype(BF16)
        qmla_ref[0, p, :, 128:256] = qr.astype(BF16)
        kmla_ref[0, p, :, 0:128] = kv[:, p * 128:(p + 1) * 128].astype(BF16)
        kmla_ref[0, p, :, 128:256] = kr
        vmla_ref[0, p] = kv[:, 512 + p * 128:512 + (p + 1) * 128].astype(BF16)
        sl = slice(p * 128, (p + 1) * 128)
        qs_ref[0, p] = (_rope_slab(proj[:, _P_QS:_P_KS][:, sl], c64, s64) * s_scale).astype(BF16)
        ks_ref[0, p] = _rope_slab(proj[:, _P_KS:_P_VS][:, sl], c64, s64).astype(BF16)
        vs_ref[0, p] = proj[:, _P_VS:_P_QI][:, sl].astype(BF16)
        qi_ref[0, p] = (_rope_slab(proj[:, _P_QI:_P_KI][:, sl], c64, s64) * i_scale).astype(BF16)
    ki_ref[0] = _rope_slab(proj[:, _P_KI:_P_KR], c64, s64).astype(BF16)
    wi_ref[0] = proj[:, _P_WI:_P_END] * (IDX_HEADS ** -0.5)


def _proj_call(h, g, win, gq, wuq, gkv, wukv, tables, *, tm, interpret=False):
    b, tp, d = h.shape
    nt = tp // tm
    const = lambda shape: pl.BlockSpec(shape, lambda bi, i: (0,) * len(shape))
    tab = pl.BlockSpec((tm, LANES), lambda bi, i: (i, 0))
    pair = lambda w: pl.BlockSpec((1, N_PAIRS, tm, w), lambda bi, i: (bi, 0, i, 0))
    row = lambda w: pl.BlockSpec((1, tm, w), lambda bi, i: (bi, i, 0))
    sds = jax.ShapeDtypeStruct
    out_shape = (
        sds((b, N_PAIRS, tp, 256), BF16), sds((b, N_PAIRS, tp, 256), BF16), sds((b, N_PAIRS, tp, 128), BF16),
        sds((b, N_PAIRS, tp, 128), BF16), sds((b, N_PAIRS, tp, 128), BF16), sds((b, N_PAIRS, tp, 128), BF16),
        sds((b, N_PAIRS, tp, 128), BF16), sds((b, tp, 128), BF16), sds((b, tp, 128), F32))
    return pl.pallas_call(
        _proj_kernel,
        grid=(b, nt),
        in_specs=[row(d), const((1, d)), const(win.shape), const((1, MLA_Q_RANK)), const(wuq.shape),
                  const((1, MLA_KV_RANK)), const(wukv.shape), tab, tab, tab, tab],
        out_specs=(pair(256), pair(256), pair(128), pair(128), pair(128), pair(128), pair(128),
                   row(128), row(128)),
        out_shape=out_shape,
        compiler_params=pltpu.CompilerParams(
            dimension_semantics=("arbitrary", "arbitrary"), vmem_limit_bytes=VMEM_LIMIT),
        interpret=interpret,
        name="proj",
    )(h, g, win, gq, wuq, gkv, wukv, *tables)


_NT = (((1,), (1,)), ((), ()))


def _stage_scores(qm_scr, k, s_ref):
    s_ref[...] = lax.dot_general(qm_scr[...], k, _NT, preferred_element_type=F32)


def _stage_softmax(s, p_ref, a_ref, m_scr, l_scr):
    m_prev = m_scr[...]
    m_new = jnp.maximum(m_prev, jnp.max(s, axis=-1, keepdims=True))
    alpha = jnp.exp(m_prev - m_new)
    p = jnp.exp(s - jnp.concatenate([m_new] * (s.shape[1] // LANES), axis=1))
    l_scr[...] = alpha * l_scr[...] + jnp.sum(p, axis=-1, keepdims=True)
    m_scr[...] = m_new
    p_ref[...] = p.astype(BF16)
    a_ref[...] = alpha


def _stage_pv(p_ref, a_ref, v, acc_scr):
    acc_scr[...] = a_ref[...] * acc_scr[...] + jnp.dot(p_ref[...], v, preferred_element_type=F32)


def _flash_init(m_scr, l_scr, acc_scr, p1, a1):
    m_scr[...] = jnp.full(m_scr.shape, NEG_BIG, F32)
    l_scr[...] = jnp.zeros(l_scr.shape, F32)
    acc_scr[...] = jnp.zeros(acc_scr.shape, F32)
    p1[...] = jnp.zeros(p1.shape, BF16)
    a1[...] = jnp.ones(a1.shape, F32)


def _flash_finish(o_ref, l_scr, acc_scr, tq):
    lane = lax.broadcasted_iota(I32, (tq, LANES), 1)
    oa = acc_scr[0:tq] / l_scr[0:tq]
    ob = acc_scr[tq:2 * tq] / l_scr[tq:2 * tq]
    o_ref[0] = jnp.where(lane < 64, oa, ob).astype(o_ref.dtype)


def _chunk_counts(i, tq):
    n_chunks = ((i + 1) * tq + TK - 1) // TK
    return n_chunks, (n_chunks + 1) // 2


_FLASH_SCRATCH = lambda tq: [
    pltpu.VMEM((2 * tq, TK), F32), pltpu.VMEM((2 * tq, TK), F32),
    pltpu.VMEM((2 * tq, TK), BF16), pltpu.VMEM((2 * tq, TK), BF16),
    pltpu.VMEM((2 * tq, LANES), F32), pltpu.VMEM((2 * tq, LANES), F32),
    pltpu.VMEM((2 * tq, LANES), F32), pltpu.VMEM((2 * tq, LANES), F32),
    pltpu.VMEM((2 * tq, LANES), F32)]


def _mla_kernel(q_ref, k_ref, v_ref, o_ref, qm_scr, s0, s1, p0, p1, a0, a1, m_scr, l_scr, acc_scr,
                *, tq, n_q_blocks):
    i = pl.program_id(2)

    @pl.when(i >= n_q_blocks)
    def _pad_rows():
        o_ref[...] = jnp.zeros(o_ref.shape, o_ref.dtype)

    @pl.when(i < n_q_blocks)
    def _attend():
        q = q_ref[0, 0]
        lane = lax.broadcasted_iota(I32, (tq, 256), 1)
        in_a = (lane < 64) | ((lane >= 128) & (lane < 144)) | ((lane >= 192) & (lane < 208))
        in_b = ((lane >= 64) & (lane < 128)) | ((lane >= 144) & (lane < 160)) | ((lane >= 208) & (lane < 224))
        zero = jnp.zeros_like(q)
        qm_scr[0:tq] = jnp.where(in_a, q, zero)
        qm_scr[tq:2 * tq] = jnp.where(in_b, q, zero)
        _flash_init(m_scr, l_scr, acc_scr, p1, a1)
        n_chunks, n_pairs = _chunk_counts(i, tq)

        def rows(c):
            cc = jnp.clip(c, 0, n_chunks - 1)
            return pl.ds(pl.multiple_of(cc * TK, TK), TK)

        k_at = lambda c: k_ref[0, 0, rows(c), :]
        v_at = lambda c: v_ref[0, 0, rows(c), :]

        def masked(s_ref, c):
            r = (lax.broadcasted_iota(I32, (2 * tq, TK), 0) & (tq - 1)) + i * tq
            col = lax.broadcasted_iota(I32, (2 * tq, TK), 1) + c * TK
            return jnp.where(col <= r, s_ref[...], NEG_BIG)

        _stage_scores(qm_scr, k_at(0), s0)

        def pair(j, carry):
            c = 2 * j
            _stage_pv(p1, a1, v_at(c - 1), acc_scr)
            _stage_softmax(s0[...], p0, a0, m_scr, l_scr)
            _stage_scores(qm_scr, k_at(c + 1), s1)
            _stage_pv(p0, a0, v_at(c), acc_scr)
            _stage_softmax(s1[...], p1, a1, m_scr, l_scr)
            _stage_scores(qm_scr, k_at(c + 2), s0)
            return carry

        lax.fori_loop(0, n_pairs - 1, pair, 0)
        c = 2 * (n_pairs - 1)
        _stage_pv(p1, a1, v_at(c - 1), acc_scr)
        _stage_softmax(masked(s0, c), p0, a0, m_scr, l_scr)
        _stage_scores(qm_scr, k_at(c + 1), s1)
        _stage_pv(p0, a0, v_at(c), acc_scr)
        _stage_softmax(masked(s1, c + 1), p1, a1, m_scr, l_scr)
        _stage_pv(p1, a1, v_at(c + 1), acc_scr)
        _flash_finish(o_ref, l_scr, acc_scr, tq)


def _mla_call(qmla, kmla, vmla, *, tq, n_q_blocks, interpret=False):
    b, _, tp, _ = qmla.shape
    return pl.pallas_call(
        functools.partial(_mla_kernel, tq=tq, n_q_blocks=n_q_blocks),
        grid=(b, N_PAIRS, tp // tq),
        in_specs=[pl.BlockSpec((1, 1, tq, 256), lambda bi, p, i: (bi, p, i, 0)),
                  pl.BlockSpec((1, 1, tp, 256), lambda bi, p, i: (bi, p, 0, 0)),
                  pl.BlockSpec((1, 1, tp, 128), lambda bi, p, i: (bi, p, 0, 0))],
        out_specs=pl.BlockSpec((1, tq, LANES), lambda bi, p, i: (bi, i, p)),
        out_shape=jax.ShapeDtypeStruct((b, tp, N_PAIRS * LANES), BF16),
        scratch_shapes=[pltpu.VMEM((2 * tq, 256), BF16)] + _FLASH_SCRATCH(tq),
        compiler_params=pltpu.CompilerParams(
            dimension_semantics=("arbitrary", "arbitrary", "arbitrary"), vmem_limit_bytes=VMEM_LIMIT),
        interpret=interpret,
        name="mla_attn",
    )(qmla, kmla, vmla)


def _dsa_kernel(qi_ref, wi_ref, qs_ref, ki_ref, ks_ref, vs_ref, o_ref,
                key_scr, hi_scr, lo_scr, qim_scr, wb_scr, thr_scr, cut_scr,
                qm_scr, s0, s1, p0, p1, a0, a1, m_scr, l_scr, acc_scr, *, tq, topk, n_q_blocks):
    i = pl.program_id(1)
    p = pl.program_id(2)
    sub = TK // LANES
    n_chunks, n_pairs = _chunk_counts(i, tq)
    n128 = n_chunks * sub
    lane = lax.broadcasted_iota(I32, (tq, LANES), 1)
    in_a = (lane < 32) | ((lane >= 64) & (lane < 96))

    @pl.when(i >= n_q_blocks)
    def _pad_rows():
        o_ref[...] = jnp.zeros(o_ref.shape, o_ref.dtype)

    @pl.when((i < n_q_blocks) & (p == 0))
    def _select():
        for pp in range(N_PAIRS):
            qp = qi_ref[0, pp]
            zero = jnp.zeros_like(qp)
            qim_scr[(2 * pp) * tq:(2 * pp + 1) * tq] = jnp.where(in_a, qp, zero)
            qim_scr[(2 * pp + 1) * tq:(2 * pp + 2) * tq] = jnp.where(in_a, zero, qp)
        w = wi_ref[0]
        for hh in range(IDX_HEADS):
            wb_scr[hh] = jnp.broadcast_to(w[:, hh:hh + 1], (tq, LANES))
        half = TK // 2

        def logits(c2):
            kc = ki_ref[0, pl.ds(pl.multiple_of(c2 * half, half), half), :]
            return lax.dot_general(qim_scr[...], kc, _NT, preferred_element_type=F32)

        def to_keys(lg, c2):
            sc = jnp.zeros((tq, half), F32)
            for hh in range(IDX_HEADS):
                wb = jnp.concatenate([wb_scr[hh]] * (half // LANES), axis=1)
                sc = sc + wb * jnp.maximum(lg[hh * tq:(hh + 1) * tq], 0.0)
            bits = pltpu.bitcast(sc, I32)
            key = bits ^ ((bits >> 31) & 0x7FFFFFFF)
            key = key - (key >> 31)
            r2 = lax.broadcasted_iota(I32, (tq, half), 0) + i * tq
            c2i = lax.broadcasted_iota(I32, (tq, half), 1) + c2 * half
            key = jnp.where(c2i <= r2, key, INT_MIN)
            hi = (key >> 16).astype(I16)
            lo = ((key & 0xFFFF) - 2 ** 15).astype(I16)
            for t in range(half // LANES):
                cc = c2 * (half // LANES) + t
                key_scr[cc] = key[:, t * LANES:(t + 1) * LANES]
                hi_scr[cc] = hi[:, t * LANES:(t + 1) * LANES]
                lo_scr[cc] = lo[:, t * LANES:(t + 1) * LANES]

        def score_chunk(c, carry):
            lg0 = logits(2 * c)
            lg1 = logits(2 * c + 1)
            to_keys(lg0, 2 * c)
            to_keys(lg1, 2 * c + 1)
            return carry

        lax.fori_loop(0, n_chunks, score_chunk, 0)

        def count16(ref, pred):
            def body(c4, acc):
                for u in range(sub):
                    acc = acc + pred(ref[c4 * sub + u]).astype(I16)
                return acc
            acc = lax.fori_loop(0, n_chunks, body, jnp.zeros((tq, LANES), I16))
            total = jnp.sum(acc.astype(I32), axis=-1, keepdims=True)
            return jnp.broadcast_to(total, (tq, LANES))

        def half_search(ref, above):
            def step(it, t):
                bit = lax.shift_left(jnp.int32(1), lax.convert_element_type(15 - it, I32))
                cand = t + jnp.full((tq, LANES), bit, I32).astype(I16)
                cnt = (count16(ref, lambda h: h >= cand) + above).astype(I16)
                return jnp.where(cnt >= topk, cand, t)
            return lax.fori_loop(0, 16, step, jnp.full((tq, LANES), INT16_MIN, I16))

        t_hi = half_search(hi_scr, jnp.zeros((tq, LANES), I32))
        above = count16(hi_scr, lambda h: h > t_hi)

        def drop_other_hi(c, carry):
            lo_scr[c] = jnp.where(hi_scr[c] == t_hi, lo_scr[c], INT16_MIN)
            return carry

        lax.fori_loop(0, n128, drop_other_hi, 0)
        t_lo = half_search(lo_scr, above)
        thr_scr[...] = (t_hi.astype(I32) << 16) | (t_lo.astype(I32) + 2 ** 15)

        rg = tq // 2
        lane_g = lax.broadcasted_iota(I32, (rg, LANES), 1)

        def count(pred, r0):
            def body(c4, acc):
                for u in range(sub):
                    c = c4 * sub + u
                    acc = acc + pred(key_scr[c, r0:r0 + rg, :], c).astype(I32)
                return acc
            acc = lax.fori_loop(0, n_chunks, body, jnp.zeros((rg, LANES), I32))
            return jnp.sum(acc, axis=-1, keepdims=True)

        for r0 in range(0, tq, rg):
            thr = thr_scr[r0:r0 + rg]
            n_gt = count(lambda kk, c: kk > thr, r0)
            n_ge = count(lambda kk, c: kk >= thr, r0)
            need = topk - n_gt
            cut_scr[r0:r0 + rg] = jnp.full((rg, LANES), 2 ** 30, I32)
            excess = jnp.max(jnp.where((n_ge > topk) & (thr[:, 0:1] > INT_MIN), 1, 0))

            @pl.when(excess > 0)
            def _ties():
                def idx_step(it, x):
                    cand = x + lax.shift_left(jnp.int32(1), lax.convert_element_type(13 - it, I32))
                    cnt = count(lambda kk, c: (kk == thr) & ((lane_g + c * LANES) < cand), r0)
                    return jnp.where(cnt < need, cand, x)

                cut_scr[r0:r0 + rg] = lax.fori_loop(0, 14, idx_step, jnp.zeros((rg, LANES), I32))

        def to_bias(c, carry):
            kk = key_scr[c]
            thr = thr_scr[...]
            sel = (kk > thr) | ((kk == thr) & ((lane + c * LANES) <= cut_scr[...]) & (thr > INT_MIN))
            key_scr[c] = pltpu.bitcast(jnp.where(sel, 0.0, NEG_BIG).astype(F32), I32)
            return carry

        lax.fori_loop(0, n128, to_bias, 0)

        def fill(c, carry):
            key_scr[c] = pltpu.bitcast(jnp.full((tq, LANES), NEG_BIG, F32), I32)
            return carry

        lax.fori_loop(n128, 2 * n_pairs * sub, fill, 0)

    @pl.when(i < n_q_blocks)
    def _attend():
        q = qs_ref[0, p]
        zero = jnp.zeros_like(q)
        qm_scr[0:tq] = jnp.where(in_a, q, zero)
        qm_scr[tq:2 * tq] = jnp.where(in_a, zero, q)
        _flash_init(m_scr, l_scr, acc_scr, p1, a1)

        def rows(c):
            cc = jnp.clip(c, 0, n_chunks - 1)
            return pl.ds(pl.multiple_of(cc * TK, TK), TK)

        k_at = lambda c: ks_ref[0, p, rows(c), :]
        v_at = lambda c: vs_ref[0, p, rows(c), :]

        def biased(s_ref, c):
            bias = jnp.concatenate([pltpu.bitcast(key_scr[c * sub + t], F32) for t in range(sub)], axis=1)
            return s_ref[...] + jnp.concatenate([bias, bias], axis=0)

        _stage_scores(qm_scr, k_at(0), s0)

        def pair(j, carry):
            c = 2 * j
            _stage_pv(p1, a1, v_at(c - 1), acc_scr)
            _stage_softmax(biased(s0, c), p0, a0, m_scr, l_scr)
            _stage_scores(qm_scr, k_at(c + 1), s1)
            _stage_pv(p0, a0, v_at(c), acc_scr)
            _stage_softmax(biased(s1, c + 1), p1, a1, m_scr, l_scr)
            _stage_scores(qm_scr, k_at(c + 2), s0)
            return carry

        lax.fori_loop(0, n_pairs, pair, 0)
        _stage_pv(p1, a1, v_at(2 * n_pairs - 1), acc_scr)
        _flash_finish(o_ref, l_scr, acc_scr, tq)


def _dsa_call(qi, wi, qs, ki, ks, vs, *, tq, topk, n_q_blocks, interpret=False):
    b, _, tp, _ = qs.shape
    once = pl.Buffered(1)
    max_pairs = (tp // TK + 1) // 2
    return pl.pallas_call(
        functools.partial(_dsa_kernel, tq=tq, topk=topk, n_q_blocks=n_q_blocks),
        grid=(b, tp // tq, N_PAIRS),
        in_specs=[pl.BlockSpec((1, N_PAIRS, tq, LANES), lambda bi, i, p: (bi, 0, i, 0)),
                  pl.BlockSpec((1, tq, LANES), lambda bi, i, p: (bi, i, 0)),
                  pl.BlockSpec((1, N_PAIRS, tq, LANES), lambda bi, i, p: (bi, 0, i, 0)),
                  pl.BlockSpec((1, tp, LANES), lambda bi, i, p: (bi, 0, 0), pipeline_mode=once),
                  pl.BlockSpec((1, N_PAIRS, tp, LANES), lambda bi, i, p: (bi, 0, 0, 0), pipeline_mode=once),
                  pl.BlockSpec((1, N_PAIRS, tp, LANES), lambda bi, i, p: (bi, 0, 0, 0), pipeline_mode=once)],
        out_specs=pl.BlockSpec((1, tq, LANES), lambda bi, i, p: (bi, i, p)),
        out_shape=jax.ShapeDtypeStruct((b, tp, N_PAIRS * LANES), BF16),
        scratch_shapes=[pltpu.VMEM((2 * max_pairs * (TK // LANES), tq, LANES), I32),
                        pltpu.VMEM((tp // LANES, tq, LANES), I16),
                        pltpu.VMEM((tp // LANES, tq, LANES), I16),
                        pltpu.VMEM((IDX_HEADS * tq, LANES), BF16),
                        pltpu.VMEM((IDX_HEADS, tq, LANES), F32),
                        pltpu.VMEM((tq, LANES), I32),
                        pltpu.VMEM((tq, LANES), I32),
                        pltpu.VMEM((2 * tq, LANES), BF16)] + _FLASH_SCRATCH(tq),
        compiler_params=pltpu.CompilerParams(
            dimension_semantics=("arbitrary", "arbitrary", "arbitrary"), vmem_limit_bytes=VMEM_LIMIT),
        interpret=interpret,
        name="dsa_attn",
    )(qi, wi, qs, ki, ks, vs)


def _ffn_kernel(h_ref, oa_ref, ob_ref, wo_ref, g2_ref, wg_ref, wu_ref, wd_ref, gf_ref, out_ref):
    o = jnp.concatenate([oa_ref[0], ob_ref[0]], axis=1)
    h1 = h_ref[0] + jnp.dot(o, wo_ref[...], preferred_element_type=F32)
    u = _rms(h1, g2_ref[...]).astype(BF16)
    gate = jnp.dot(u, wg_ref[...], preferred_element_type=F32)
    up = jnp.dot(u, wu_ref[...], preferred_element_type=F32)
    act = (gate * jax.nn.sigmoid(gate) * up).astype(BF16)
    h2 = h1 + jnp.dot(act, wd_ref[...], preferred_element_type=F32)
    out_ref[0] = _rms(h2, gf_ref[...])


def _ffn_call(h, oa, ob, wo, g2, wg, wu, wd, gf, *, tm, interpret=False):
    b, tp, d = h.shape
    once = pl.Buffered(1)
    const = lambda shape: pl.BlockSpec(shape, lambda bi, i: (0,) * len(shape), pipeline_mode=once)
    row = lambda w: pl.BlockSpec((1, tm, w), lambda bi, i: (bi, i, 0))
    return pl.pallas_call(
        _ffn_kernel,
        grid=(b, tp // tm),
        in_specs=[row(d), row(oa.shape[-1]), row(ob.shape[-1]), const(wo.shape), const((1, d)),
                  const(wg.shape), const(wu.shape), const(wd.shape), const((1, d))],
        out_specs=row(d),
        out_shape=jax.ShapeDtypeStruct((b, tp, d), F32),
        compiler_params=pltpu.CompilerParams(
            dimension_semantics=("arbitrary", "arbitrary"), vmem_limit_bytes=VMEM_LIMIT),
        interpret=interpret,
        name="ffn",
    )(h, oa, ob, wo, g2, wg, wu, wd, gf)


def _layer(h, t_real, attn_norm_g, w_in, mla_q_norm_g, w_uq, mla_kv_norm_g, w_ukv, w_o, ffn_norm_g,
           w_gate, w_up, w_down, out_g, topk, *, tm, tq, interpret=False):
    tp = h.shape[1]
    n_q_blocks = -(-t_real // tq)
    win = _take_cols(w_in, _w_in_cols()).astype(BF16)
    wuq = _take_cols(w_uq, _w_uq_cols()).astype(BF16)
    wukv = jnp.take(w_ukv, jnp.asarray(_w_ukv_cols()), axis=1).astype(BF16)
    tables = _rope_tables(tp)
    qmla, kmla, vmla, qs, ks, vs, qi, ki, wi = _proj_call(
        h, attn_norm_g[None], win, mla_q_norm_g[None], wuq, mla_kv_norm_g[None], wukv, tables,
        tm=tm, interpret=interpret)
    o_mla = _mla_call(qmla, kmla, vmla, tq=tq, n_q_blocks=n_q_blocks, interpret=interpret)
    o_dsa = _dsa_call(qi, wi, qs, ki, ks, vs, tq=tq, topk=topk, n_q_blocks=n_q_blocks, interpret=interpret)
    return _ffn_call(h, o_mla, o_dsa, w_o.astype(BF16), ffn_norm_g[None], w_gate.astype(BF16),
                     w_up.astype(BF16), w_down.astype(BF16), out_g[None], tm=tm, interpret=interpret)


def kernel(x, meta_tokens, attn_norm_g, w_in, mla_q_norm_g, w_uq, mla_kv_norm_g, w_ukv, w_o,
           ffn_norm_g, w_gate, w_up, w_down, final_norm_g):
    b, seq, d = x.shape
    depth = w_in.shape[0]
    assert depth == 1, "the final norm is fused into the layer's last kernel"
    topk = min(TOPK_MAX, seq // 4)
    meta = jnp.broadcast_to(meta_tokens[None].astype(x.dtype), (b, N_META, d))
    t = N_META + seq
    tp = -(-t // TK) * TK
    h = jnp.concatenate([meta, x, jnp.zeros((b, tp - t, d), x.dtype)], axis=1)
    out = _layer(h, t, attn_norm_g[0], w_in[0], mla_q_norm_g[0], w_uq[0], mla_kv_norm_g[0], w_ukv[0],
                 w_o[0], ffn_norm_g[0], w_gate[0], w_up[0], w_down[0], final_norm_g, topk,
                 tm=TM, tq=TQ)
    return out[:, N_META:t]
```

```python
import functools

import numpy as np
import jax
import jax.numpy as jnp
from jax import lax
from jax.experimental import pallas as pl
from jax.experimental.pallas import tpu as pltpu

F32 = jnp.float32
BF16 = jnp.bfloat16
I32 = jnp.int32

D_MODEL = 1024
N_META = 16
ROPE_THETA = 10000.0
EPS = 1e-6
MLA_HEADS = 8
MLA_Q_RANK = 256
MLA_KV_RANK = 128
MLA_NOPE_DIM = 64
MLA_ROPE_DIM = 32
MLA_QK_DIM = MLA_NOPE_DIM + MLA_ROPE_DIM
MLA_V_DIM = 64
DSA_HEADS = 8
DSA_HEAD_DIM = 64
IDX_HEADS = 8
IDX_DIM = 64
TOPK_MAX = 256
D_FF = 2816

LANES = 128
SUBLANES = 8
N_PAIRS = 4
NEG_BIG = -1e30
INT_MIN = -2 ** 31
VMEM_LIMIT = 58 * 1024 * 1024
TQ = 256
TK = 512
TM = TK
HK = TK // 2

_OFF_CQ, _OFF_CKV, _OFF_KR, _OFF_QS, _OFF_KS, _OFF_VS, _OFF_QI, _OFF_KI, _OFF_WI = (
    0, 256, 384, 416, 928, 1440, 1952, 2464, 2528)
_C_IN = 2536
_P_CQ, _P_CKV, _P_QS, _P_KS, _P_VS, _P_QI, _P_KI, _P_KR, _P_WI, _P_END = (
    0, 256, 384, 896, 1408, 1920, 2432, 2560, 2688, 2816)


def _pair_rope_cols(base, hd=64):
    half = hd // 2
    cols = []
    for p in range(N_PAIRS):
        a, b = 2 * p, 2 * p + 1
        cols += [base + a * hd + d for d in range(half)]
        cols += [base + b * hd + d for d in range(half)]
        cols += [base + a * hd + half + d for d in range(half)]
        cols += [base + b * hd + half + d for d in range(half)]
    return cols


def _w_in_cols():
    z = _C_IN
    cols = list(range(_OFF_CQ, _OFF_CQ + 256)) + list(range(_OFF_CKV, _OFF_CKV + 128))
    cols += _pair_rope_cols(_OFF_QS) + _pair_rope_cols(_OFF_KS)
    cols += list(range(_OFF_VS, _OFF_VS + 512))
    cols += _pair_rope_cols(_OFF_QI)
    ki1 = [_OFF_KI + d for d in range(32)]
    ki2 = [_OFF_KI + 32 + d for d in range(32)]
    cols += ki1 + ki1 + ki2 + ki2
    kr1 = [_OFF_KR + d for d in range(16)]
    kr2 = [_OFF_KR + 16 + d for d in range(16)]
    cols += kr1 + kr1 + [z] * 32 + kr2 + kr2 + [z] * 32
    cols += [_OFF_WI + d for d in range(IDX_HEADS)] + [z] * (LANES - IDX_HEADS)
    assert len(cols) == _P_END
    return np.asarray(cols, np.int32)


def _w_uq_cols():
    z = MLA_HEADS * MLA_QK_DIM
    cols = []
    for p in range(N_PAIRS):
        a, b = 2 * p, 2 * p + 1
        cols += [a * MLA_QK_DIM + d for d in range(64)] + [b * MLA_QK_DIM + d for d in range(64)]
        cols += [a * MLA_QK_DIM + 64 + d for d in range(16)] + [b * MLA_QK_DIM + 64 + d for d in range(16)]
        cols += [z] * 32
        cols += [a * MLA_QK_DIM + 80 + d for d in range(16)] + [b * MLA_QK_DIM + 80 + d for d in range(16)]
        cols += [z] * 32
    return np.asarray(cols, np.int32)


def _w_ukv_cols():
    cols = [h * 128 + d for h in range(MLA_HEADS) for d in range(64)]
    cols += [h * 128 + 64 + d for h in range(MLA_HEADS) for d in range(64)]
    return np.asarray(cols, np.int32)


def _take_cols(w, cols):
    wz = jnp.concatenate([w, jnp.zeros((w.shape[0], 1), w.dtype)], axis=1)
    return jnp.take(wz, jnp.asarray(cols), axis=1)


def _rope_tables(tp):
    pos = jnp.arange(tp, dtype=jnp.int32).astype(F32)

    def cs(half):
        inv = jnp.power(ROPE_THETA, -jnp.arange(half, dtype=F32) / half)
        ang = pos[:, None] * inv[None, :]
        return jnp.cos(ang), jnp.sin(ang)

    c32, s32 = cs(32)
    c16, s16 = cs(16)
    z = jnp.zeros((tp, 32), F32)
    c64t = jnp.concatenate([c32, c32, c32, c32], axis=1)
    s64t = jnp.concatenate([-s32, -s32, s32, s32], axis=1)
    c32t = jnp.concatenate([c16, c16, z, c16, c16, z], axis=1)
    s32t = jnp.concatenate([-s16, -s16, z, s16, s16, z], axis=1)
    return c64t, s64t, c32t, s32t


def _rms(x, g):
    return x * lax.rsqrt(jnp.mean(x * x, axis=-1, keepdims=True) + EPS) * g


def _rope_slab(x, c, s):
    return x * c + pltpu.roll(x, 64, axis=1) * s


def _proj_kernel(h_ref, g_ref, win_ref, gq_ref, wuq_ref, gkv_ref, wukv_ref,
                 c64_ref, s64_ref, c32_ref, s32_ref,
                 qmla_ref, kmla_ref, vmla_ref, qs_ref, ks_ref, vs_ref, qi_ref, ki_ref, wi_ref):
    u = _rms(h_ref[0], g_ref[...]).astype(BF16)
    proj = jnp.dot(u, win_ref[...], preferred_element_type=F32)
    c64, s64, c32, s32 = c64_ref[...], s64_ref[...], c32_ref[...], s32_ref[...]

    cq = _rms(proj[:, _P_CQ:_P_CQ + 256], gq_ref[...]).astype(BF16)
    q = jnp.dot(cq, wuq_ref[...], preferred_element_type=F32)
    ckv = _rms(proj[:, _P_CKV:_P_CKV + 128], gkv_ref[...]).astype(BF16)
    kv = jnp.dot(ckv, wukv_ref[...], preferred_element_type=F32)
    kr = _rope_slab(proj[:, _P_KR:_P_KR + 128], c32, s32).astype(BF16)
    q_scale = MLA_QK_DIM ** -0.5
    s_scale = DSA_HEAD_DIM ** -0.5
    i_scale = IDX_DIM ** -0.5
    for p in range(N_PAIRS):
        qn = q[:, p * 256:p * 256 + 128] * q_scale
        qr = _rope_slab(q[:, p * 256 + 128:p * 256 + 256], c32, s32) * q_scale
        qmla_ref[0, p, :, 0:128] = qn.astype(BF16)
        qmla_ref[0, p, :, 128:256] = qr.astype(BF16)
        kmla_ref[0, p, :, 0:128] = kv[:, p * 128:(p + 1) * 128].astype(BF16)
        kmla_ref[0, p, :, 128:256] = kr
        vmla_ref[0, p, 0] = kv[:, 512 + p * 128:512 + (p + 1) * 128].T.astype(BF16)
        sl = slice(p * 128, (p + 1) * 128)
        qs_ref[0, p] = (_rope_slab(proj[:, _P_QS:_P_KS][:, sl], c64, s64) * s_scale).astype(BF16)
        ks_ref[0, p] = _rope_slab(proj[:, _P_KS:_P_VS][:, sl], c64, s64).astype(BF16)
        vs_ref[0, p, 0] = proj[:, _P_VS:_P_QI][:, sl].T.astype(BF16)
        qi_ref[0, p] = (_rope_slab(proj[:, _P_QI:_P_KI][:, sl], c64, s64) * i_scale).astype(BF16)
    ki_ref[0] = _rope_slab(proj[:, _P_KI:_P_KR], c64, s64).astype(BF16)
    wi_ref[0] = proj[:, _P_WI:_P_END] * (IDX_HEADS ** -0.5)


def _proj_call(h, g, win, gq, wuq, gkv, wukv, tables, *, interpret=False):
    b, tp, d = h.shape
    tm = TM
    nt = tp // tm
    const = lambda shape: pl.BlockSpec(shape, lambda bi, i: (0,) * len(shape))
    tab = pl.BlockSpec((tm, LANES), lambda bi, i: (i, 0))
    pair = lambda w: pl.BlockSpec((1, N_PAIRS, tm, w), lambda bi, i: (bi, 0, i, 0))
    pair_t = pl.BlockSpec((1, N_PAIRS, 1, LANES, tm), lambda bi, i: (bi, 0, i, 0, 0))
    row = lambda w: pl.BlockSpec((1, tm, w), lambda bi, i: (bi, i, 0))
    sds = jax.ShapeDtypeStruct
    v_t = sds((b, N_PAIRS, nt, LANES, tm), BF16)
    out_shape = (
        sds((b, N_PAIRS, tp, 256), BF16), sds((b, N_PAIRS, tp, 256), BF16), v_t,
        sds((b, N_PAIRS, tp, 128), BF16), sds((b, N_PAIRS, tp, 128), BF16), v_t,
        sds((b, N_PAIRS, tp, 128), BF16), sds((b, tp, 128), BF16), sds((b, tp, 128), F32))
    return pl.pallas_call(
        _proj_kernel,
        grid=(b, nt),
        in_specs=[row(d), const((1, d)), const(win.shape), const((1, MLA_Q_RANK)), const(wuq.shape),
                  const((1, MLA_KV_RANK)), const(wukv.shape), tab, tab, tab, tab],
        out_specs=(pair(256), pair(256), pair_t, pair(128), pair(128), pair_t, pair(128),
                   row(128), row(128)),
        out_shape=out_shape,
        compiler_params=pltpu.CompilerParams(
            dimension_semantics=("arbitrary", "arbitrary"), vmem_limit_bytes=VMEM_LIMIT),
        interpret=interpret,
        name="proj",
    )(h, g, win, gq, wuq, gkv, wukv, *tables)


_NT = (((1,), (1,)), ((), ()))


def _stage_scores(k, qm_scr, s_ref):
    s_ref[...] = lax.dot_general(k, qm_scr[...], _NT, preferred_element_type=F32)


def _stage_softmax(s, p_ref, a_ref, m_scr, l_scr):
    m_prev = m_scr[...]
    m_new = jnp.maximum(m_prev, jnp.max(s, axis=0, keepdims=True))
    alpha = jnp.exp(m_prev - m_new)
    p = jnp.exp(s - m_new[0:1])
    l_scr[...] = alpha * l_scr[...] + jnp.sum(p, axis=0, keepdims=True)
    m_scr[...] = m_new
    p_ref[...] = p.astype(BF16)
    a_ref[...] = alpha


def _stage_pv(p_ref, a_ref, v_t, acc_scr):
    acc_scr[...] = a_ref[0:1] * acc_scr[...] + jnp.dot(v_t, p_ref[...], preferred_element_type=F32)


def _flash_init(m_scr, l_scr, acc_scr, p1, a1):
    m_scr[...] = jnp.full(m_scr.shape, NEG_BIG, F32)
    l_scr[...] = jnp.zeros(l_scr.shape, F32)
    acc_scr[...] = jnp.zeros(acc_scr.shape, F32)
    p1[...] = jnp.zeros(p1.shape, BF16)
    a1[...] = jnp.ones(a1.shape, F32)


def _flash_finish(o_ref, l_scr, acc_scr, tq):
    o = (acc_scr[...] / l_scr[0:1]).T
    lane = lax.broadcasted_iota(I32, (tq, LANES), 1)
    o_ref[0] = jnp.where(lane < 64, o[0:tq], o[tq:2 * tq]).astype(o_ref.dtype)


def _chunk_counts(i, tq):
    n_chunks = ((i + 1) * tq + TK - 1) // TK
    return n_chunks, (n_chunks + 1) // 2


_FLASH_SCRATCH = lambda tq: [
    pltpu.VMEM((TK, 2 * tq), F32), pltpu.VMEM((TK, 2 * tq), F32),
    pltpu.VMEM((TK, 2 * tq), BF16), pltpu.VMEM((TK, 2 * tq), BF16),
    pltpu.VMEM((SUBLANES, 2 * tq), F32), pltpu.VMEM((SUBLANES, 2 * tq), F32),
    pltpu.VMEM((SUBLANES, 2 * tq), F32), pltpu.VMEM((SUBLANES, 2 * tq), F32),
    pltpu.VMEM((LANES, 2 * tq), F32)]


def _mla_kernel(q_ref, k_ref, v_ref, o_ref, qm_scr, s0, s1, p0, p1, a0, a1, m_scr, l_scr, acc_scr,
                *, tq, n_q_blocks):
    i = pl.program_id(2)

    @pl.when(i >= n_q_blocks)
    def _pad_rows():
        o_ref[...] = jnp.zeros(o_ref.shape, o_ref.dtype)

    @pl.when(i < n_q_blocks)
    def _attend():
        q = q_ref[0, 0]
        lane = lax.broadcasted_iota(I32, (tq, 256), 1)
        in_a = (lane < 64) | ((lane >= 128) & (lane < 144)) | ((lane >= 192) & (lane < 208))
        in_b = ((lane >= 64) & (lane < 128)) | ((lane >= 144) & (lane < 160)) | ((lane >= 208) & (lane < 224))
        zero = jnp.zeros_like(q)
        qm_scr[0:tq] = jnp.where(in_a, q, zero)
        qm_scr[tq:2 * tq] = jnp.where(in_b, q, zero)
        _flash_init(m_scr, l_scr, acc_scr, p1, a1)
        n_chunks, n_pairs = _chunk_counts(i, tq)

        def clamp(c):
            return jnp.clip(c, 0, n_chunks - 1)

        k_at = lambda c: k_ref[0, 0, pl.ds(pl.multiple_of(clamp(c) * TK, TK), TK), :]
        v_at = lambda c: v_ref[0, 0, clamp(c)]

        def masked(s_ref, c):
            key = lax.broadcasted_iota(I32, (TK, 2 * tq), 0) + c * TK
            qry = (lax.broadcasted_iota(I32, (TK, 2 * tq), 1) & (tq - 1)) + i * tq
            return jnp.where(key <= qry, s_ref[...], NEG_BIG)

        _stage_scores(k_at(0), qm_scr, s0)

        def pair(j, carry):
            c = 2 * j
            _stage_pv(p1, a1, v_at(c - 1), acc_scr)
            _stage_softmax(s0[...], p0, a0, m_scr, l_scr)
            _stage_scores(k_at(c + 1), qm_scr, s1)
            _stage_pv(p0, a0, v_at(c), acc_scr)
            _stage_softmax(s1[...], p1, a1, m_scr, l_scr)
            _stage_scores(k_at(c + 2), qm_scr, s0)
            return carry

        lax.fori_loop(0, n_pairs - 1, pair, 0)
        c = 2 * (n_pairs - 1)
        _stage_pv(p1, a1, v_at(c - 1), acc_scr)
        _stage_softmax(masked(s0, c), p0, a0, m_scr, l_scr)
        _stage_scores(k_at(c + 1), qm_scr, s1)
        _stage_pv(p0, a0, v_at(c), acc_scr)
        _stage_softmax(masked(s1, c + 1), p1, a1, m_scr, l_scr)
        _stage_pv(p1, a1, v_at(c + 1), acc_scr)
        _flash_finish(o_ref, l_scr, acc_scr, tq)


def _mla_call(qmla, kmla, vmla_t, *, tq, n_q_blocks, interpret=False):
    b, _, tp, _ = qmla.shape
    return pl.pallas_call(
        functools.partial(_mla_kernel, tq=tq, n_q_blocks=n_q_blocks),
        grid=(b, N_PAIRS, tp // tq),
        in_specs=[pl.BlockSpec((1, 1, tq, 256), lambda bi, p, i: (bi, p, i, 0)),
                  pl.BlockSpec((1, 1, tp, 256), lambda bi, p, i: (bi, p, 0, 0)),
                  pl.BlockSpec((1, 1, tp // TK, LANES, TK), lambda bi, p, i: (bi, p, 0, 0, 0))],
        out_specs=pl.BlockSpec((1, tq, LANES), lambda bi, p, i: (bi, i, p)),
        out_shape=jax.ShapeDtypeStruct((b, tp, N_PAIRS * LANES), BF16),
        scratch_shapes=[pltpu.VMEM((2 * tq, 256), BF16)] + _FLASH_SCRATCH(tq),
        compiler_params=pltpu.CompilerParams(
            dimension_semantics=("arbitrary", "arbitrary", "arbitrary"), vmem_limit_bytes=VMEM_LIMIT),
        interpret=interpret,
        name="mla_attn",
    )(qmla, kmla, vmla_t)


def _dsa_kernel(qi_ref, wi_ref, qs_ref, ki_ref, ks_ref, vs_ref, o_ref,
                key_scr, qim_scr, wt_scr, thr_scr, cut_scr,
                qm_scr, s0, s1, p0, p1, a0, a1, m_scr, l_scr, acc_scr, *, tq, topk, n_q_blocks):
    i = pl.program_id(1)
    p = pl.program_id(2)
    n_chunks, n_pairs = _chunk_counts(i, tq)
    n_tiles = n_chunks * (TK // HK)
    slabs = HK // SUBLANES
    lane = lax.broadcasted_iota(I32, (tq, LANES), 1)
    in_a = (lane < 32) | ((lane >= 64) & (lane < 96))

    @pl.when(i >= n_q_blocks)
    def _pad_rows():
        o_ref[...] = jnp.zeros(o_ref.shape, o_ref.dtype)

    @pl.when((i < n_q_blocks) & (p == 0))
    def _select():
        for pp in range(N_PAIRS):
            qp = qi_ref[0, pp]
            zero = jnp.zeros_like(qp)
            qim_scr[(2 * pp) * tq:(2 * pp + 1) * tq] = jnp.where(in_a, qp, zero)
            qim_scr[(2 * pp + 1) * tq:(2 * pp + 2) * tq] = jnp.where(in_a, zero, qp)
        wt_scr[...] = wi_ref[0].T

        def logits(t):
            kc = ki_ref[0, pl.ds(pl.multiple_of(t * HK, HK), HK), :]
            return lax.dot_general(kc, qim_scr[...], _NT, preferred_element_type=F32)

        def to_keys(lg, t):
            sc = jnp.zeros((HK, tq), F32)
            for hh in range(IDX_HEADS):
                sc = sc + wt_scr[hh:hh + 1, :] * jnp.maximum(lg[:, hh * tq:(hh + 1) * tq], 0.0)
            bits = pltpu.bitcast(sc, I32)
            key = bits ^ ((bits >> 31) & 0x7FFFFFFF)
            key = key - (key >> 31)
            kpos = lax.broadcasted_iota(I32, (HK, tq), 0) + t * HK
            qpos = lax.broadcasted_iota(I32, (HK, tq), 1) + i * tq
            key_scr[t] = jnp.where(kpos <= qpos, key, INT_MIN)

        def score_chunk(c, carry):
            lg0 = logits(2 * c)
            lg1 = logits(2 * c + 1)
            to_keys(lg0, 2 * c)
            to_keys(lg1, 2 * c + 1)
            return carry

        lax.fori_loop(0, n_chunks, score_chunk, 0)

        def count(pred):
            def body(t, acc):
                kk = key_scr[t].reshape(slabs, SUBLANES, tq)
                return acc + jnp.sum(pred(kk, t).astype(I32), axis=0, dtype=I32)
            acc = lax.fori_loop(0, n_tiles, body, jnp.zeros((SUBLANES, tq), I32))
            return jnp.broadcast_to(jnp.sum(acc, axis=0, keepdims=True, dtype=I32), (SUBLANES, tq))

        def bit_step(it, t):
            cand = t + lax.shift_left(jnp.int32(1), lax.convert_element_type(31 - it, I32))
            cnt = count(lambda kk, _: kk >= cand[None])
            return jnp.where(cnt >= topk, cand, t)

        thr = lax.fori_loop(0, 32, bit_step, jnp.full((SUBLANES, tq), INT_MIN, I32))
        thr_scr[...] = thr
        n_gt = count(lambda kk, _: kk > thr[None])
        n_ge = count(lambda kk, _: kk >= thr[None])
        need = topk - n_gt
        cut_scr[...] = jnp.full((SUBLANES, tq), 2 ** 30, I32)
        excess = jnp.max(jnp.where((n_ge > topk) & (thr > INT_MIN), 1, 0))
        kidx = (lax.broadcasted_iota(I32, (slabs, SUBLANES, tq), 0) * SUBLANES
                + lax.broadcasted_iota(I32, (slabs, SUBLANES, tq), 1))

        @pl.when(excess > 0)
        def _ties():
            def idx_step(it, x):
                cand = x + lax.shift_left(jnp.int32(1), lax.convert_element_type(13 - it, I32))
                cnt = count(lambda kk, t: (kk == thr[None]) & ((kidx + t * HK) < cand[None]))
                return jnp.where(cnt < need, cand, x)

            cut_scr[...] = lax.fori_loop(0, 14, idx_step, jnp.zeros((SUBLANES, tq), I32))

        def to_bias(t, carry):
            kk = key_scr[t].reshape(slabs, SUBLANES, tq)
            th = thr_scr[...][None]
            sel = (kk > th) | ((kk == th) & ((kidx + t * HK) <= cut_scr[...][None]) & (th > INT_MIN))
            bias = jnp.where(sel, 0.0, NEG_BIG).astype(F32).reshape(HK, tq)
            key_scr[t] = pltpu.bitcast(bias, I32)
            return carry

        lax.fori_loop(0, n_tiles, to_bias, 0)

        def fill(t, carry):
            key_scr[t] = pltpu.bitcast(jnp.full((HK, tq), NEG_BIG, F32), I32)
            return carry

        lax.fori_loop(n_tiles, 2 * n_pairs * (TK // HK), fill, 0)

    @pl.when(i < n_q_blocks)
    def _attend():
        q = qs_ref[0, p]
        zero = jnp.zeros_like(q)
        qm_scr[0:tq] = jnp.where(in_a, q, zero)
        qm_scr[tq:2 * tq] = jnp.where(in_a, zero, q)
        _flash_init(m_scr, l_scr, acc_scr, p1, a1)

        def clamp(c):
            return jnp.clip(c, 0, n_chunks - 1)

        k_at = lambda c: ks_ref[0, p, pl.ds(pl.multiple_of(clamp(c) * TK, TK), TK), :]
        v_at = lambda c: vs_ref[0, p, clamp(c)]

        def biased(s_ref, c):
            bias = jnp.concatenate(
                [pltpu.bitcast(key_scr[c * (TK // HK) + t], F32) for t in range(TK // HK)], axis=0)
            return s_ref[...] + jnp.concatenate([bias, bias], axis=1)

        _stage_scores(k_at(0), qm_scr, s0)

        def pair(j, carry):
            c = 2 * j
            _stage_pv(p1, a1, v_at(c - 1), acc_scr)
            _stage_softmax(biased(s0, c), p0, a0, m_scr, l_scr)
            _stage_scores(k_at(c + 1), qm_scr, s1)
            _stage_pv(p0, a0, v_at(c), acc_scr)
            _stage_softmax(biased(s1, c + 1), p1, a1, m_scr, l_scr)
            _stage_scores(k_at(c + 2), qm_scr, s0)
            return carry

        lax.fori_loop(0, n_pairs, pair, 0)
        _stage_pv(p1, a1, v_at(2 * n_pairs - 1), acc_scr)
        _flash_finish(o_ref, l_scr, acc_scr, tq)


def _dsa_call(qi, wi, qs, ki, ks, vs_t, *, tq, topk, n_q_blocks, interpret=False):
    b, _, tp, _ = qs.shape
    once = pl.Buffered(1)
    max_pairs = (tp // TK + 1) // 2
    return pl.pallas_call(
        functools.partial(_dsa_kernel, tq=tq, topk=topk, n_q_blocks=n_q_blocks),
        grid=(b, tp // tq, N_PAIRS),
        in_specs=[pl.BlockSpec((1, N_PAIRS, tq, LANES), lambda bi, i, p: (bi, 0, i, 0)),
                  pl.BlockSpec((1, tq, LANES), lambda bi, i, p: (bi, i, 0)),
                  pl.BlockSpec((1, N_PAIRS, tq, LANES), lambda bi, i, p: (bi, 0, i, 0)),
                  pl.BlockSpec((1, tp, LANES), lambda bi, i, p: (bi, 0, 0), pipeline_mode=once),
                  pl.BlockSpec((1, N_PAIRS, tp, LANES), lambda bi, i, p: (bi, 0, 0, 0), pipeline_mode=once),
                  pl.BlockSpec((1, N_PAIRS, tp // TK, LANES, TK), lambda bi, i, p: (bi, 0, 0, 0, 0),
                               pipeline_mode=once)],
        out_specs=pl.BlockSpec((1, tq, LANES), lambda bi, i, p: (bi, i, p)),
        out_shape=jax.ShapeDtypeStruct((b, tp, N_PAIRS * LANES), BF16),
        scratch_shapes=[pltpu.VMEM((2 * max_pairs * (TK // HK), HK, tq), I32),
                        pltpu.VMEM((IDX_HEADS * tq, LANES), BF16),
                        pltpu.VMEM((LANES, tq), F32),
                        pltpu.VMEM((SUBLANES, tq), I32),
                        pltpu.VMEM((SUBLANES, tq), I32),
                        pltpu.VMEM((2 * tq, LANES), BF16)] + _FLASH_SCRATCH(tq),
        compiler_params=pltpu.CompilerParams(
            dimension_semantics=("arbitrary", "arbitrary", "arbitrary"), vmem_limit_bytes=VMEM_LIMIT),
        interpret=interpret,
        name="dsa_attn",
    )(qi, wi, qs, ki, ks, vs_t)


def _ffn_kernel(h_ref, oa_ref, ob_ref, wo_ref, g2_ref, wg_ref, wu_ref, wd_ref, gf_ref, out_ref):
    o = jnp.concatenate([oa_ref[0], ob_ref[0]], axis=1)
    h1 = h_ref[0] + jnp.dot(o, wo_ref[...], preferred_element_type=F32)
    u = _rms(h1, g2_ref[...]).astype(BF16)
    gate = jnp.dot(u, wg_ref[...], preferred_element_type=F32)
    up = jnp.dot(u, wu_ref[...], preferred_element_type=F32)
    act = (gate * jax.nn.sigmoid(gate) * up).astype(BF16)
    h2 = h1 + jnp.dot(act, wd_ref[...], preferred_element_type=F32)
    out_ref[0] = _rms(h2, gf_ref[...])


def _ffn_call(h, oa, ob, wo, g2, wg, wu, wd, gf, *, interpret=False):
    b, tp, d = h.shape
    tm = TM
    once = pl.Buffered(1)
    const = lambda shape: pl.BlockSpec(shape, lambda bi, i: (0,) * len(shape), pipeline_mode=once)
    row = lambda w: pl.BlockSpec((1, tm, w), lambda bi, i: (bi, i, 0))
    return pl.pallas_call(
        _ffn_kernel,
        grid=(b, tp // tm),
        in_specs=[row(d), row(oa.shape[-1]), row(ob.shape[-1]), const(wo.shape), const((1, d)),
                  const(wg.shape), const(wu.shape), const(wd.shape), const((1, d))],
        out_specs=row(d),
        out_shape=jax.ShapeDtypeStruct((b, tp, d), F32),
        compiler_params=pltpu.CompilerParams(
            dimension_semantics=("arbitrary", "arbitrary"), vmem_limit_bytes=VMEM_LIMIT),
        interpret=interpret,
        name="ffn",
    )(h, oa, ob, wo, g2, wg, wu, wd, gf)


def _layer(h, t_real, attn_norm_g, w_in, mla_q_norm_g, w_uq, mla_kv_norm_g, w_ukv, w_o, ffn_norm_g,
           w_gate, w_up, w_down, out_g, topk, *, interpret=False):
    tp = h.shape[1]
    n_q_blocks = -(-t_real // TQ)
    win = _take_cols(w_in, _w_in_cols()).astype(BF16)
    wuq = _take_cols(w_uq, _w_uq_cols()).astype(BF16)
    wukv = jnp.take(w_ukv, jnp.asarray(_w_ukv_cols()), axis=1).astype(BF16)
    tables = _rope_tables(tp)
    qmla, kmla, vmla_t, qs, ks, vs_t, qi, ki, wi = _proj_call(
        h, attn_norm_g[None], win, mla_q_norm_g[None], wuq, mla_kv_norm_g[None], wukv, tables,
        interpret=interpret)
    o_mla = _mla_call(qmla, kmla, vmla_t, tq=TQ, n_q_blocks=n_q_blocks, interpret=interpret)
    o_dsa = _dsa_call(qi, wi, qs, ki, ks, vs_t, tq=TQ, topk=topk, n_q_blocks=n_q_blocks, interpret=interpret)
    return _ffn_call(h, o_mla, o_dsa, w_o.astype(BF16), ffn_norm_g[None], w_gate.astype(BF16),
                     w_up.astype(BF16), w_down.astype(BF16), out_g[None], interpret=interpret)


def kernel(x, meta_tokens, attn_norm_g, w_in, mla_q_norm_g, w_uq, mla_kv_norm_g, w_ukv, w_o,
           ffn_norm_g, w_gate, w_up, w_down, final_norm_g):
    b, seq, d = x.shape
    depth = w_in.shape[0]
    assert depth == 1, "the final norm is fused into the layer's last kernel"
    topk = min(TOPK_MAX, seq // 4)
    meta = jnp.broadcast_to(meta_tokens[None].astype(x.dtype), (b, N_META, d))
    t = N_META + seq
    tp = -(-t // TK) * TK
    h = jnp.concatenate([meta, x, jnp.zeros((b, tp - t, d), x.dtype)], axis=1)
    out = _layer(h, t, attn_norm_g[0], w_in[0], mla_q_norm_g[0], w_uq[0], mla_kv_norm_g[0], w_ukv[0],
                 w_o[0], ffn_norm_g[0], w_gate[0], w_up[0], w_down[0], final_norm_g, topk)
    return out[:, N_META:t]
```

```python
import functools

import numpy as np
import jax
import jax.numpy as jnp
from jax import lax
from jax.experimental import pallas as pl
from jax.experimental.pallas import tpu as pltpu

F32 = jnp.float32
BF16 = jnp.bfloat16
I32 = jnp.int32

D_MODEL = 1024
N_META = 16
ROPE_THETA = 10000.0
EPS = 1e-6
MLA_HEADS = 8
MLA_Q_RANK = 256
MLA_KV_RANK = 128
MLA_NOPE_DIM = 64
MLA_ROPE_DIM = 32
MLA_QK_DIM = MLA_NOPE_DIM + MLA_ROPE_DIM
MLA_V_DIM = 64
DSA_HEADS = 8
DSA_HEAD_DIM = 64
IDX_HEADS = 8
IDX_DIM = 64
TOPK_MAX = 256
D_FF = 2816

LANES = 128
SUBLANES = 8
N_PAIRS = 4
NEG_BIG = -1e30
INT_MIN = -2 ** 31
VMEM_LIMIT = 58 * 1024 * 1024
TQ = 256
TK = 512
TM = TK
HK = TK // 2

_OFF_CQ, _OFF_CKV, _OFF_KR, _OFF_QS, _OFF_KS, _OFF_VS, _OFF_QI, _OFF_KI, _OFF_WI = (
    0, 256, 384, 416, 928, 1440, 1952, 2464, 2528)
_C_IN = 2536
_P_CQ, _P_CKV, _P_QS, _P_KS, _P_VS, _P_QI, _P_KI, _P_KR, _P_WI, _P_END = (
    0, 256, 384, 896, 1408, 1920, 2432, 2560, 2688, 2816)


def _pair_rope_cols(base, hd=64):
    half = hd // 2
    cols = []
    for p in range(N_PAIRS):
        a, b = 2 * p, 2 * p + 1
        cols += [base + a * hd + d for d in range(half)]
        cols += [base + b * hd + d for d in range(half)]
        cols += [base + a * hd + half + d for d in range(half)]
        cols += [base + b * hd + half + d for d in range(half)]
    return cols


def _w_in_cols():
    z = _C_IN
    cols = list(range(_OFF_CQ, _OFF_CQ + 256)) + list(range(_OFF_CKV, _OFF_CKV + 128))
    cols += _pair_rope_cols(_OFF_QS) + _pair_rope_cols(_OFF_KS)
    cols += list(range(_OFF_VS, _OFF_VS + 512))
    cols += _pair_rope_cols(_OFF_QI)
    ki1 = [_OFF_KI + d for d in range(32)]
    ki2 = [_OFF_KI + 32 + d for d in range(32)]
    cols += ki1 + ki1 + ki2 + ki2
    kr1 = [_OFF_KR + d for d in range(16)]
    kr2 = [_OFF_KR + 16 + d for d in range(16)]
    cols += kr1 + kr1 + [z] * 32 + kr2 + kr2 + [z] * 32
    cols += [_OFF_WI + d for d in range(IDX_HEADS)] + [z] * (LANES - IDX_HEADS)
    assert len(cols) == _P_END
    return np.asarray(cols, np.int32)


def _w_uq_cols():
    z = MLA_HEADS * MLA_QK_DIM
    cols = []
    for p in range(N_PAIRS):
        a, b = 2 * p, 2 * p + 1
        cols += [a * MLA_QK_DIM + d for d in range(64)] + [b * MLA_QK_DIM + d for d in range(64)]
        cols += [a * MLA_QK_DIM + 64 + d for d in range(16)] + [b * MLA_QK_DIM + 64 + d for d in range(16)]
        cols += [z] * 32
        cols += [a * MLA_QK_DIM + 80 + d for d in range(16)] + [b * MLA_QK_DIM + 80 + d for d in range(16)]
        cols += [z] * 32
    return np.asarray(cols, np.int32)


def _w_ukv_cols():
    cols = [h * 128 + d for h in range(MLA_HEADS) for d in range(64)]
    cols += [h * 128 + 64 + d for h in range(MLA_HEADS) for d in range(64)]
    return np.asarray(cols, np.int32)


def _take_cols(w, cols):
    wz = jnp.concatenate([w, jnp.zeros((w.shape[0], 1), w.dtype)], axis=1)
    return jnp.take(wz, jnp.asarray(cols), axis=1)


def _rope_tables(tp):
    pos = jnp.arange(tp, dtype=jnp.int32).astype(F32)

    def cs(half):
        inv = jnp.power(ROPE_THETA, -jnp.arange(half, dtype=F32) / half)
        ang = pos[:, None] * inv[None, :]
        return jnp.cos(ang), jnp.sin(ang)

    c32, s32 = cs(32)
    c16, s16 = cs(16)
    z = jnp.zeros((tp, 32), F32)
    c64t = jnp.concatenate([c32, c32, c32, c32], axis=1)
    s64t = jnp.concatenate([-s32, -s32, s32, s32], axis=1)
    c32t = jnp.concatenate([c16, c16, z, c16, c16, z], axis=1)
    s32t = jnp.concatenate([-s16, -s16, z, s16, s16, z], axis=1)
    return c64t, s64t, c32t, s32t


def _rms(x, g):
    return x * lax.rsqrt(jnp.mean(x * x, axis=-1, keepdims=True) + EPS) * g


def _rope_slab(x, c, s):
    return x * c + pltpu.roll(x, 64, axis=1) * s


def _proj_kernel(h_ref, g_ref, win_ref, gq_ref, wuq_ref, gkv_ref, wukv_ref,
                 c64_ref, s64_ref, c32_ref, s32_ref,
                 qmla_ref, kmla_ref, vmla_ref, qs_ref, ks_ref, vs_ref, qi_ref, ki_ref, wi_ref):
    u = _rms(h_ref[0], g_ref[...]).astype(BF16)
    proj = jnp.dot(u, win_ref[...], preferred_element_type=F32)
    c64, s64, c32, s32 = c64_ref[...], s64_ref[...], c32_ref[...], s32_ref[...]

    cq = _rms(proj[:, _P_CQ:_P_CQ + 256], gq_ref[...]).astype(BF16)
    q = jnp.dot(cq, wuq_ref[...], preferred_element_type=F32)
    ckv = _rms(proj[:, _P_CKV:_P_CKV + 128], gkv_ref[...]).astype(BF16)
    kv = jnp.dot(ckv, wukv_ref[...], preferred_element_type=F32)
    kr = _rope_slab(proj[:, _P_KR:_P_KR + 128], c32, s32).astype(BF16)
    q_scale = MLA_QK_DIM ** -0.5
    s_scale = DSA_HEAD_DIM ** -0.5
    i_scale = IDX_DIM ** -0.5
    for p in range(N_PAIRS):
        qn = q[:, p * 256:p * 256 + 128] * q_scale
        qr = _rope_slab(q[:, p * 256 + 128:p * 256 + 256], c32, s32) * q_scale
        qmla_ref[0, p, :, 0:128] = qn.astype(BF16)
        qmla_ref[0, p, :, 128:256] = qr.astype(BF16)
        kmla_ref[0, p, :, 0:128] = kv[:, p * 128:(p + 1) * 128].astype(BF16)
        kmla_ref[0, p, :, 128:256] = kr
        vmla_ref[0, p] = kv[:, 512 + p * 128:512 + (p + 1) * 128].astype(BF16)
        sl = slice(p * 128, (p + 1) * 128)
        qs_ref[0, p] = (_rope_slab(proj[:, _P_QS:_P_KS][:, sl], c64, s64) * s_scale).astype(BF16)
        ks_ref[0, p] = _rope_slab(proj[:, _P_KS:_P_VS][:, sl], c64, s64).astype(BF16)
        vs_ref[0, p, 0] = proj[:, _P_VS:_P_QI][:, sl].T.astype(BF16)
        qi_ref[0, p] = (_rope_slab(proj[:, _P_QI:_P_KI][:, sl], c64, s64) * i_scale).astype(BF16)
    ki_ref[0] = _rope_slab(proj[:, _P_KI:_P_KR], c64, s64).astype(BF16)
    wi_ref[0] = proj[:, _P_WI:_P_END] * (IDX_HEADS ** -0.5)


def _proj_call(h, g, win, gq, wuq, gkv, wukv, tables, *, interpret=False):
    b, tp, d = h.shape
    tm = TM
    nt = tp // tm
    const = lambda shape: pl.BlockSpec(shape, lambda bi, i: (0,) * len(shape))
    tab = pl.BlockSpec((tm, LANES), lambda bi, i: (i, 0))
    pair = lambda w: pl.BlockSpec((1, N_PAIRS, tm, w), lambda bi, i: (bi, 0, i, 0))
    pair_t = pl.BlockSpec((1, N_PAIRS, 1, LANES, tm), lambda bi, i: (bi, 0, i, 0, 0))
    row = lambda w: pl.BlockSpec((1, tm, w), lambda bi, i: (bi, i, 0))
    sds = jax.ShapeDtypeStruct
    v_t = sds((b, N_PAIRS, nt, LANES, tm), BF16)
    out_shape = (
        sds((b, N_PAIRS, tp, 256), BF16), sds((b, N_PAIRS, tp, 256), BF16), sds((b, N_PAIRS, tp, 128), BF16),
        sds((b, N_PAIRS, tp, 128), BF16), sds((b, N_PAIRS, tp, 128), BF16), v_t,
        sds((b, N_PAIRS, tp, 128), BF16), sds((b, tp, 128), BF16), sds((b, tp, 128), F32))
    return pl.pallas_call(
        _proj_kernel,
        grid=(b, nt),
        in_specs=[row(d), const((1, d)), const(win.shape), const((1, MLA_Q_RANK)), const(wuq.shape),
                  const((1, MLA_KV_RANK)), const(wukv.shape), tab, tab, tab, tab],
        out_specs=(pair(256), pair(256), pair(128), pair(128), pair(128), pair_t, pair(128),
                   row(128), row(128)),
        out_shape=out_shape,
        compiler_params=pltpu.CompilerParams(
            dimension_semantics=("arbitrary", "arbitrary"), vmem_limit_bytes=VMEM_LIMIT),
        interpret=interpret,
        name="proj",
    )(h, g, win, gq, wuq, gkv, wukv, *tables)


_NT = (((1,), (1,)), ((), ()))


def _stage_scores(k, qm_scr, s_ref):
    s_ref[...] = lax.dot_general(k, qm_scr[...], _NT, preferred_element_type=F32)


def _stage_softmax(s, p_ref, a_ref, m_scr, l_scr):
    m_prev = m_scr[...]
    m_new = jnp.maximum(m_prev, jnp.max(s, axis=0, keepdims=True))
    alpha = jnp.exp(m_prev - m_new)
    p = jnp.exp(s - m_new[0:1])
    l_scr[...] = alpha * l_scr[...] + jnp.sum(p, axis=0, keepdims=True)
    m_scr[...] = m_new
    p_ref[...] = p.astype(BF16)
    a_ref[...] = alpha


def _stage_pv(p_ref, a_ref, v_t, acc_scr):
    acc_scr[...] = a_ref[0:1] * acc_scr[...] + jnp.dot(v_t, p_ref[...], preferred_element_type=F32)


def _flash_init(m_scr, l_scr, acc_scr, p1, a1):
    m_scr[...] = jnp.full(m_scr.shape, NEG_BIG, F32)
    l_scr[...] = jnp.zeros(l_scr.shape, F32)
    acc_scr[...] = jnp.zeros(acc_scr.shape, F32)
    p1[...] = jnp.zeros(p1.shape, BF16)
    a1[...] = jnp.ones(a1.shape, F32)


def _flash_finish(o_ref, l_scr, acc_scr, tq):
    o = (acc_scr[...] / l_scr[0:1]).T
    lane = lax.broadcasted_iota(I32, (tq, LANES), 1)
    o_ref[0] = jnp.where(lane < 64, o[0:tq], o[tq:2 * tq]).astype(o_ref.dtype)


def _chunk_counts(i, tq):
    n_chunks = ((i + 1) * tq + TK - 1) // TK
    return n_chunks, (n_chunks + 1) // 2


_FLASH_SCRATCH = lambda tq: [
    pltpu.VMEM((TK, 2 * tq), F32), pltpu.VMEM((TK, 2 * tq), F32),
    pltpu.VMEM((TK, 2 * tq), BF16), pltpu.VMEM((TK, 2 * tq), BF16),
    pltpu.VMEM((SUBLANES, 2 * tq), F32), pltpu.VMEM((SUBLANES, 2 * tq), F32),
    pltpu.VMEM((SUBLANES, 2 * tq), F32), pltpu.VMEM((SUBLANES, 2 * tq), F32),
    pltpu.VMEM((LANES, 2 * tq), F32)]


def _row_scores(qm_scr, k, s_ref):
    s_ref[...] = lax.dot_general(qm_scr[...], k, _NT, preferred_element_type=F32)


def _row_softmax(s, p_ref, a_ref, m_scr, l_scr):
    m_prev = m_scr[...]
    m_new = jnp.maximum(m_prev, jnp.max(s, axis=-1, keepdims=True))
    alpha = jnp.exp(m_prev - m_new)
    p = jnp.exp(s - jnp.concatenate([m_new] * (s.shape[1] // LANES), axis=1))
    l_scr[...] = alpha * l_scr[...] + jnp.sum(p, axis=-1, keepdims=True)
    m_scr[...] = m_new
    p_ref[...] = p.astype(BF16)
    a_ref[...] = alpha


def _row_pv(p_ref, a_ref, v, acc_scr):
    acc_scr[...] = a_ref[...] * acc_scr[...] + jnp.dot(p_ref[...], v, preferred_element_type=F32)


def _row_finish(o_ref, l_scr, acc_scr, tq):
    lane = lax.broadcasted_iota(I32, (tq, LANES), 1)
    oa = acc_scr[0:tq] / l_scr[0:tq]
    ob = acc_scr[tq:2 * tq] / l_scr[tq:2 * tq]
    o_ref[0] = jnp.where(lane < 64, oa, ob).astype(o_ref.dtype)


_ROW_SCRATCH = lambda tq: [
    pltpu.VMEM((2 * tq, TK), F32), pltpu.VMEM((2 * tq, TK), F32),
    pltpu.VMEM((2 * tq, TK), BF16), pltpu.VMEM((2 * tq, TK), BF16),
    pltpu.VMEM((2 * tq, LANES), F32), pltpu.VMEM((2 * tq, LANES), F32),
    pltpu.VMEM((2 * tq, LANES), F32), pltpu.VMEM((2 * tq, LANES), F32),
    pltpu.VMEM((2 * tq, LANES), F32)]


def _mla_kernel(q_ref, k_ref, v_ref, o_ref, qm_scr, s0, s1, p0, p1, a0, a1, m_scr, l_scr, acc_scr,
                *, tq, n_q_blocks):
    i = pl.program_id(2)

    @pl.when(i >= n_q_blocks)
    def _pad_rows():
        o_ref[...] = jnp.zeros(o_ref.shape, o_ref.dtype)

    @pl.when(i < n_q_blocks)
    def _attend():
        q = q_ref[0, 0]
        lane = lax.broadcasted_iota(I32, (tq, 256), 1)
        in_a = (lane < 64) | ((lane >= 128) & (lane < 144)) | ((lane >= 192) & (lane < 208))
        in_b = ((lane >= 64) & (lane < 128)) | ((lane >= 144) & (lane < 160)) | ((lane >= 208) & (lane < 224))
        zero = jnp.zeros_like(q)
        qm_scr[0:tq] = jnp.where(in_a, q, zero)
        qm_scr[tq:2 * tq] = jnp.where(in_b, q, zero)
        _flash_init(m_scr, l_scr, acc_scr, p1, a1)
        n_chunks, n_pairs = _chunk_counts(i, tq)

        def rows(c):
            cc = jnp.clip(c, 0, n_chunks - 1)
            return pl.ds(pl.multiple_of(cc * TK, TK), TK)

        k_at = lambda c: k_ref[0, 0, rows(c), :]
        v_at = lambda c: v_ref[0, 0, rows(c), :]

        def masked(s_ref, c):
            qry = (lax.broadcasted_iota(I32, (2 * tq, TK), 0) & (tq - 1)) + i * tq
            key = lax.broadcasted_iota(I32, (2 * tq, TK), 1) + c * TK
            return jnp.where(key <= qry, s_ref[...], NEG_BIG)

        _row_scores(qm_scr, k_at(0), s0)

        def pair(j, carry):
            c = 2 * j
            _row_pv(p1, a1, v_at(c - 1), acc_scr)
            _row_softmax(s0[...], p0, a0, m_scr, l_scr)
            _row_scores(qm_scr, k_at(c + 1), s1)
            _row_pv(p0, a0, v_at(c), acc_scr)
            _row_softmax(s1[...], p1, a1, m_scr, l_scr)
            _row_scores(qm_scr, k_at(c + 2), s0)
            return carry

        lax.fori_loop(0, n_pairs - 1, pair, 0)
        c = 2 * (n_pairs - 1)
        _row_pv(p1, a1, v_at(c - 1), acc_scr)
        _row_softmax(masked(s0, c), p0, a0, m_scr, l_scr)
        _row_scores(qm_scr, k_at(c + 1), s1)
        _row_pv(p0, a0, v_at(c), acc_scr)
        _row_softmax(masked(s1, c + 1), p1, a1, m_scr, l_scr)
        _row_pv(p1, a1, v_at(c + 1), acc_scr)
        _row_finish(o_ref, l_scr, acc_scr, tq)


def _mla_call(qmla, kmla, vmla, *, tq, n_q_blocks, interpret=False):
    b, _, tp, _ = qmla.shape
    return pl.pallas_call(
        functools.partial(_mla_kernel, tq=tq, n_q_blocks=n_q_blocks),
        grid=(b, N_PAIRS, tp // tq),
        in_specs=[pl.BlockSpec((1, 1, tq, 256), lambda bi, p, i: (bi, p, i, 0)),
                  pl.BlockSpec((1, 1, tp, 256), lambda bi, p, i: (bi, p, 0, 0)),
                  pl.BlockSpec((1, 1, tp, 128), lambda bi, p, i: (bi, p, 0, 0))],
        out_specs=pl.BlockSpec((1, tq, LANES), lambda bi, p, i: (bi, i, p)),
        out_shape=jax.ShapeDtypeStruct((b, tp, N_PAIRS * LANES), BF16),
        scratch_shapes=[pltpu.VMEM((2 * tq, 256), BF16)] + _ROW_SCRATCH(tq),
        compiler_params=pltpu.CompilerParams(
            dimension_semantics=("arbitrary", "arbitrary", "arbitrary"), vmem_limit_bytes=VMEM_LIMIT),
        interpret=interpret,
        name="mla_attn",
    )(qmla, kmla, vmla)


def _dsa_kernel(qi_ref, wi_ref, qs_ref, ki_ref, ks_ref, vs_ref, o_ref,
                key_scr, qim_scr, wt_scr, thr_scr, cut_scr,
                qm_scr, s0, s1, p0, p1, a0, a1, m_scr, l_scr, acc_scr, *, tq, topk, n_q_blocks):
    i = pl.program_id(1)
    p = pl.program_id(2)
    n_chunks, n_pairs = _chunk_counts(i, tq)
    n_tiles = n_chunks * (TK // HK)
    slabs = HK // SUBLANES
    lane = lax.broadcasted_iota(I32, (tq, LANES), 1)
    in_a = (lane < 32) | ((lane >= 64) & (lane < 96))

    @pl.when(i >= n_q_blocks)
    def _pad_rows():
        o_ref[...] = jnp.zeros(o_ref.shape, o_ref.dtype)

    @pl.when((i < n_q_blocks) & (p == 0))
    def _select():
        for pp in range(N_PAIRS):
            qp = qi_ref[0, pp]
            zero = jnp.zeros_like(qp)
            qim_scr[(2 * pp) * tq:(2 * pp + 1) * tq] = jnp.where(in_a, qp, zero)
            qim_scr[(2 * pp + 1) * tq:(2 * pp + 2) * tq] = jnp.where(in_a, zero, qp)
        wt_scr[...] = wi_ref[0].T

        def logits(t):
            kc = ki_ref[0, pl.ds(pl.multiple_of(t * HK, HK), HK), :]
            return lax.dot_general(kc, qim_scr[...], _NT, preferred_element_type=F32)

        def to_keys(lg, t):
            sc = jnp.zeros((HK, tq), F32)
            for hh in range(IDX_HEADS):
                sc = sc + wt_scr[hh:hh + 1, :] * jnp.maximum(lg[:, hh * tq:(hh + 1) * tq], 0.0)
            bits = pltpu.bitcast(sc, I32)
            key = bits ^ ((bits >> 31) & 0x7FFFFFFF)
            key = key - (key >> 31)
            kpos = lax.broadcasted_iota(I32, (HK, tq), 0) + t * HK
            qpos = lax.broadcasted_iota(I32, (HK, tq), 1) + i * tq
            key_scr[t] = jnp.where(kpos <= qpos, key, INT_MIN)

        def score_chunk(c, carry):
            lg0 = logits(2 * c)
            lg1 = logits(2 * c + 1)
            to_keys(lg0, 2 * c)
            to_keys(lg1, 2 * c + 1)
            return carry

        lax.fori_loop(0, n_chunks, score_chunk, 0)

        def count(pred):
            def body(c, acc):
                for u in range(TK // HK):
                    t = c * (TK // HK) + u
                    kk = key_scr[t].reshape(slabs, SUBLANES, tq)
                    acc = acc + jnp.sum(pred(kk, t).astype(I32), axis=0, dtype=I32)
                return acc
            acc = lax.fori_loop(0, n_chunks, body, jnp.zeros((SUBLANES, tq), I32))
            return jnp.broadcast_to(jnp.sum(acc, axis=0, keepdims=True, dtype=I32), (SUBLANES, tq))

        def bit_step(it, t):
            cand = t + lax.shift_left(jnp.int32(1), lax.convert_element_type(31 - it, I32))
            cnt = count(lambda kk, _: kk >= cand[None])
            return jnp.where(cnt >= topk, cand, t)

        thr = lax.fori_loop(0, 32, bit_step, jnp.full((SUBLANES, tq), INT_MIN, I32))
        thr_scr[...] = thr
        n_gt = count(lambda kk, _: kk > thr[None])
        n_ge = count(lambda kk, _: kk >= thr[None])
        need = topk - n_gt
        cut_scr[...] = jnp.full((SUBLANES, tq), 2 ** 30, I32)
        excess = jnp.max(jnp.where((n_ge > topk) & (thr > INT_MIN), 1, 0))
        kidx = (lax.broadcasted_iota(I32, (slabs, SUBLANES, tq), 0) * SUBLANES
                + lax.broadcasted_iota(I32, (slabs, SUBLANES, tq), 1))

        @pl.when(excess > 0)
        def _ties():
            def idx_step(it, x):
                cand = x + lax.shift_left(jnp.int32(1), lax.convert_element_type(13 - it, I32))
                cnt = count(lambda kk, t: (kk == thr[None]) & ((kidx + t * HK) < cand[None]))
                return jnp.where(cnt < need, cand, x)

            cut_scr[...] = lax.fori_loop(0, 14, idx_step, jnp.zeros((SUBLANES, tq), I32))

        def to_bias(t, carry):
            kk = key_scr[t].reshape(slabs, SUBLANES, tq)
            th = thr_scr[...][None]
            sel = (kk > th) | ((kk == th) & ((kidx + t * HK) <= cut_scr[...][None]) & (th > INT_MIN))
            bias = jnp.where(sel, 0.0, NEG_BIG).astype(F32).reshape(HK, tq)
            key_scr[t] = pltpu.bitcast(bias, I32)
            return carry

        lax.fori_loop(0, n_tiles, to_bias, 0)

        def fill(t, carry):
            key_scr[t] = pltpu.bitcast(jnp.full((HK, tq), NEG_BIG, F32), I32)
            return carry

        lax.fori_loop(n_tiles, 2 * n_pairs * (TK // HK), fill, 0)

    @pl.when(i < n_q_blocks)
    def _attend():
        q = qs_ref[0, p]
        zero = jnp.zeros_like(q)
        qm_scr[0:tq] = jnp.where(in_a, q, zero)
        qm_scr[tq:2 * tq] = jnp.where(in_a, zero, q)
        _flash_init(m_scr, l_scr, acc_scr, p1, a1)

        def clamp(c):
            return jnp.clip(c, 0, n_chunks - 1)

        k_at = lambda c: ks_ref[0, p, pl.ds(pl.multiple_of(clamp(c) * TK, TK), TK), :]
        v_at = lambda c: vs_ref[0, p, clamp(c)]

        def biased(s_ref, c):
            bias = jnp.concatenate(
                [pltpu.bitcast(key_scr[c * (TK // HK) + t], F32) for t in range(TK // HK)], axis=0)
            return s_ref[...] + jnp.concatenate([bias, bias], axis=1)

        _stage_scores(k_at(0), qm_scr, s0)

        def pair(j, carry):
            c = 2 * j
            _stage_pv(p1, a1, v_at(c - 1), acc_scr)
            _stage_softmax(biased(s0, c), p0, a0, m_scr, l_scr)
            _stage_scores(k_at(c + 1), qm_scr, s1)
            _stage_pv(p0, a0, v_at(c), acc_scr)
            _stage_softmax(biased(s1, c + 1), p1, a1, m_scr, l_scr)
            _stage_scores(k_at(c + 2), qm_scr, s0)
            return carry

        lax.fori_loop(0, n_pairs, pair, 0)
        _stage_pv(p1, a1, v_at(2 * n_pairs - 1), acc_scr)
        _flash_finish(o_ref, l_scr, acc_scr, tq)


def _dsa_call(qi, wi, qs, ki, ks, vs_t, *, tq, topk, n_q_blocks, interpret=False):
    b, _, tp, _ = qs.shape
    once = pl.Buffered(1)
    max_pairs = (tp // TK + 1) // 2
    return pl.pallas_call(
        functools.partial(_dsa_kernel, tq=tq, topk=topk, n_q_blocks=n_q_blocks),
        grid=(b, tp // tq, N_PAIRS),
        in_specs=[pl.BlockSpec((1, N_PAIRS, tq, LANES), lambda bi, i, p: (bi, 0, i, 0)),
                  pl.BlockSpec((1, tq, LANES), lambda bi, i, p: (bi, i, 0)),
                  pl.BlockSpec((1, N_PAIRS, tq, LANES), lambda bi, i, p: (bi, 0, i, 0)),
                  pl.BlockSpec((1, tp, LANES), lambda bi, i, p: (bi, 0, 0), pipeline_mode=once),
                  pl.BlockSpec((1, N_PAIRS, tp, LANES), lambda bi, i, p: (bi, 0, 0, 0), pipeline_mode=once),
                  pl.BlockSpec((1, N_PAIRS, tp // TK, LANES, TK), lambda bi, i, p: (bi, 0, 0, 0, 0),
                               pipeline_mode=once)],
        out_specs=pl.BlockSpec((1, tq, LANES), lambda bi, i, p: (bi, i, p)),
        out_shape=jax.ShapeDtypeStruct((b, tp, N_PAIRS * LANES), BF16),
        scratch_shapes=[pltpu.VMEM((2 * max_pairs * (TK // HK), HK, tq), I32),
                        pltpu.VMEM((IDX_HEADS * tq, LANES), BF16),
                        pltpu.VMEM((LANES, tq), F32),
                        pltpu.VMEM((SUBLANES, tq), I32),
                        pltpu.VMEM((SUBLANES, tq), I32),
                        pltpu.VMEM((2 * tq, LANES), BF16)] + _FLASH_SCRATCH(tq),
        compiler_params=pltpu.CompilerParams(
            dimension_semantics=("arbitrary", "arbitrary", "arbitrary"), vmem_limit_bytes=VMEM_LIMIT),
        interpret=interpret,
        name="dsa_attn",
    )(qi, wi, qs, ki, ks, vs_t)


def _ffn_kernel(h_ref, oa_ref, ob_ref, wo_ref, g2_ref, wg_ref, wu_ref, wd_ref, gf_ref, out_ref):
    o = jnp.concatenate([oa_ref[0], ob_ref[0]], axis=1)
    h1 = h_ref[0] + jnp.dot(o, wo_ref[...], preferred_element_type=F32)
    u = _rms(h1, g2_ref[...]).astype(BF16)
    gate = jnp.dot(u, wg_ref[...], preferred_element_type=F32)
    up = jnp.dot(u, wu_ref[...], preferred_element_type=F32)
    act = (gate * jax.nn.sigmoid(gate) * up).astype(BF16)
    h2 = h1 + jnp.dot(act, wd_ref[...], preferred_element_type=F32)
    out_ref[0] = _rms(h2, gf_ref[...])


def _ffn_call(h, oa, ob, wo, g2, wg, wu, wd, gf, *, interpret=False):
    b, tp, d = h.shape
    tm = TM
    once = pl.Buffered(1)
    const = lambda shape: pl.BlockSpec(shape, lambda bi, i: (0,) * len(shape), pipeline_mode=once)
    row = lambda w: pl.BlockSpec((1, tm, w), lambda bi, i: (bi, i, 0))
    return pl.pallas_call(
        _ffn_kernel,
        grid=(b, tp // tm),
        in_specs=[row(d), row(oa.shape[-1]), row(ob.shape[-1]), const(wo.shape), const((1, d)),
                  const(wg.shape), const(wu.shape), const(wd.shape), const((1, d))],
        out_specs=row(d),
        out_shape=jax.ShapeDtypeStruct((b, tp, d), F32),
        compiler_params=pltpu.CompilerParams(
            dimension_semantics=("arbitrary", "arbitrary"), vmem_limit_bytes=VMEM_LIMIT),
        interpret=interpret,
        name="ffn",
    )(h, oa, ob, wo, g2, wg, wu, wd, gf)


def _layer(h, t_real, attn_norm_g, w_in, mla_q_norm_g, w_uq, mla_kv_norm_g, w_ukv, w_o, ffn_norm_g,
           w_gate, w_up, w_down, out_g, topk, *, interpret=False):
    tp = h.shape[1]
    n_q_blocks = -(-t_real // TQ)
    win = _take_cols(w_in, _w_in_cols()).astype(BF16)
    wuq = _take_cols(w_uq, _w_uq_cols()).astype(BF16)
    wukv = jnp.take(w_ukv, jnp.asarray(_w_ukv_cols()), axis=1).astype(BF16)
    tables = _rope_tables(tp)
    qmla, kmla, vmla, qs, ks, vs_t, qi, ki, wi = _proj_call(
        h, attn_norm_g[None], win, mla_q_norm_g[None], wuq, mla_kv_norm_g[None], wukv, tables,
        interpret=interpret)
    o_mla = _mla_call(qmla, kmla, vmla, tq=TQ, n_q_blocks=n_q_blocks, interpret=interpret)
    o_dsa = _dsa_call(qi, wi, qs, ki, ks, vs_t, tq=TQ, topk=topk, n_q_blocks=n_q_blocks, interpret=interpret)
    return _ffn_call(h, o_mla, o_dsa, w_o.astype(BF16), ffn_norm_g[None], w_gate.astype(BF16),
                     w_up.astype(BF16), w_down.astype(BF16), out_g[None], interpret=interpret)


def kernel(x, meta_tokens, attn_norm_g, w_in, mla_q_norm_g, w_uq, mla_kv_norm_g, w_ukv, w_o,
           ffn_norm_g, w_gate, w_up, w_down, final_norm_g):
    b, seq, d = x.shape
    depth = w_in.shape[0]
    assert depth == 1, "the final norm is fused into the layer's last kernel"
    topk = min(TOPK_MAX, seq // 4)
    meta = jnp.broadcast_to(meta_tokens[None].astype(x.dtype), (b, N_META, d))
    t = N_META + seq
    tp = -(-t // TK) * TK
    h = jnp.concatenate([meta, x, jnp.zeros((b, tp - t, d), x.dtype)], axis=1)
    out = _layer(h, t, attn_norm_g[0], w_in[0], mla_q_norm_g[0], w_uq[0], mla_kv_norm_g[0], w_ukv[0],
                 w_o[0], ffn_norm_g[0], w_gate[0], w_up[0], w_down[0], final_norm_g, topk)
    return out[:, N_META:t]
```

```python
import functools

import numpy as np
import jax
import jax.numpy as jnp
from jax import lax
from jax.experimental import pallas as pl
from jax.experimental.pallas import tpu as pltpu

F32 = jnp.float32
BF16 = jnp.bfloat16
I32 = jnp.int32

D_MODEL = 1024
N_META = 16
ROPE_THETA = 10000.0
EPS = 1e-6
MLA_HEADS = 8
MLA_Q_RANK = 256
MLA_KV_RANK = 128
MLA_NOPE_DIM = 64
MLA_ROPE_DIM = 32
MLA_QK_DIM = MLA_NOPE_DIM + MLA_ROPE_DIM
MLA_V_DIM = 64
DSA_HEADS = 8
DSA_HEAD_DIM = 64
IDX_HEADS = 8
IDX_DIM = 64
TOPK_MAX = 256
D_FF = 2816

LANES = 128
SUBLANES = 8
N_PAIRS = 4
NEG_BIG = -1e30
INT_MIN = -2 ** 31
VMEM_LIMIT = 58 * 1024 * 1024
TQ = 256
TK = 512
TM = TK
HK = TK // 2

_OFF_CQ, _OFF_CKV, _OFF_KR, _OFF_QS, _OFF_KS, _OFF_VS, _OFF_QI, _OFF_KI, _OFF_WI = (
    0, 256, 384, 416, 928, 1440, 1952, 2464, 2528)
_C_IN = 2536
_P_CQ, _P_CKV, _P_QS, _P_KS, _P_VS, _P_QI, _P_KI, _P_KR, _P_WI, _P_END = (
    0, 256, 384, 896, 1408, 1920, 2432, 2560, 2688, 2816)


def _pair_rope_cols(base, hd=64):
    half = hd // 2
    cols = []
    for p in range(N_PAIRS):
        a, b = 2 * p, 2 * p + 1
        cols += [base + a * hd + d for d in range(half)]
        cols += [base + b * hd + d for d in range(half)]
        cols += [base + a * hd + half + d for d in range(half)]
        cols += [base + b * hd + half + d for d in range(half)]
    return cols


def _w_in_cols():
    z = _C_IN
    cols = list(range(_OFF_CQ, _OFF_CQ + 256)) + list(range(_OFF_CKV, _OFF_CKV + 128))
    cols += _pair_rope_cols(_OFF_QS) + _pair_rope_cols(_OFF_KS)
    cols += list(range(_OFF_VS, _OFF_VS + 512))
    cols += _pair_rope_cols(_OFF_QI)
    ki1 = [_OFF_KI + d for d in range(32)]
    ki2 = [_OFF_KI + 32 + d for d in range(32)]
    cols += ki1 + ki1 + ki2 + ki2
    kr1 = [_OFF_KR + d for d in range(16)]
    kr2 = [_OFF_KR + 16 + d for d in range(16)]
    cols += kr1 + kr1 + [z] * 32 + kr2 + kr2 + [z] * 32
    cols += [_OFF_WI + d for d in range(IDX_HEADS)] + [z] * (LANES - IDX_HEADS)
    assert len(cols) == _P_END
    return np.asarray(cols, np.int32)


def _w_uq_cols():
    z = MLA_HEADS * MLA_QK_DIM
    cols = []
    for p in range(N_PAIRS):
        a, b = 2 * p, 2 * p + 1
        cols += [a * MLA_QK_DIM + d for d in range(64)] + [b * MLA_QK_DIM + d for d in range(64)]
        cols += [a * MLA_QK_DIM + 64 + d for d in range(16)] + [b * MLA_QK_DIM + 64 + d for d in range(16)]
        cols += [z] * 32
        cols += [a * MLA_QK_DIM + 80 + d for d in range(16)] + [b * MLA_QK_DIM + 80 + d for d in range(16)]
        cols += [z] * 32
    return np.asarray(cols, np.int32)


def _w_ukv_cols():
    cols = [h * 128 + d for h in range(MLA_HEADS) for d in range(64)]
    cols += [h * 128 + 64 + d for h in range(MLA_HEADS) for d in range(64)]
    return np.asarray(cols, np.int32)


def _take_cols(w, cols):
    wz = jnp.concatenate([w, jnp.zeros((w.shape[0], 1), w.dtype)], axis=1)
    return jnp.take(wz, jnp.asarray(cols), axis=1)


def _rope_tables(tp):
    pos = jnp.arange(tp, dtype=jnp.int32).astype(F32)

    def cs(half):
        inv = jnp.power(ROPE_THETA, -jnp.arange(half, dtype=F32) / half)
        ang = pos[:, None] * inv[None, :]
        return jnp.cos(ang), jnp.sin(ang)

    c32, s32 = cs(32)
    c16, s16 = cs(16)
    z = jnp.zeros((tp, 32), F32)
    c64t = jnp.concatenate([c32, c32, c32, c32], axis=1)
    s64t = jnp.concatenate([-s32, -s32, s32, s32], axis=1)
    c32t = jnp.concatenate([c16, c16, z, c16, c16, z], axis=1)
    s32t = jnp.concatenate([-s16, -s16, z, s16, s16, z], axis=1)
    return c64t, s64t, c32t, s32t


def _rms(x, g):
    return x * lax.rsqrt(jnp.mean(x * x, axis=-1, keepdims=True) + EPS) * g


def _rope_slab(x, c, s):
    return x * c + pltpu.roll(x, 64, axis=1) * s


def _proj_kernel(h_ref, g_ref, win_ref, gq_ref, wuq_ref, gkv_ref, wukv_ref,
                 c64_ref, s64_ref, c32_ref, s32_ref,
                 qmla_ref, kmla_ref, vmla_ref, qs_ref, ks_ref, vs_ref, qi_ref, ki_ref, wi_ref):
    u = _rms(h_ref[0], g_ref[...]).astype(BF16)
    proj = jnp.dot(u, win_ref[...], preferred_element_type=F32)
    c64, s64, c32, s32 = c64_ref[...], s64_ref[...], c32_ref[...], s32_ref[...]

    cq = _rms(proj[:, _P_CQ:_P_CQ + 256], gq_ref[...]).astype(BF16)
    q = jnp.dot(cq, wuq_ref[...], preferred_element_type=F32)
    ckv = _rms(proj[:, _P_CKV:_P_CKV + 128], gkv_ref[...]).astype(BF16)
    kv = jnp.dot(ckv, wukv_ref[...], preferred_element_type=F32)
    kr = _rope_slab(proj[:, _P_KR:_P_KR + 128], c32, s32).astype(BF16)
    q_scale = MLA_QK_DIM ** -0.5
    s_scale = DSA_HEAD_DIM ** -0.5
    i_scale = IDX_DIM ** -0.5
    for p in range(N_PAIRS):
        qn = q[:, p * 256:p * 256 + 128] * q_scale
        qr = _rope_slab(q[:, p * 256 + 128:p * 256 + 256], c32, s32) * q_scale
        qmla_ref[0, p, :, 0:128] = qn.astype(BF16)
        qmla_ref[0, p, :, 128:256] = qr.astype(BF16)
        kmla_ref[0, p, :, 0:128] = kv[:, p * 128:(p + 1) * 128].astype(BF16)
        kmla_ref[0, p, :, 128:256] = kr
        vmla_ref[0, p] = kv[:, 512 + p * 128:512 + (p + 1) * 128].astype(BF16)
        sl = slice(p * 128, (p + 1) * 128)
        qs_ref[0, p] = (_rope_slab(proj[:, _P_QS:_P_KS][:, sl], c64, s64) * s_scale).astype(BF16)
        ks_ref[0, p] = _rope_slab(proj[:, _P_KS:_P_VS][:, sl], c64, s64).astype(BF16)
        vs_ref[0, p] = proj[:, _P_VS:_P_QI][:, sl].astype(BF16)
        qi_ref[0, p] = (_rope_slab(proj[:, _P_QI:_P_KI][:, sl], c64, s64) * i_scale).astype(BF16)
    ki_ref[0] = _rope_slab(proj[:, _P_KI:_P_KR], c64, s64).astype(BF16)
    wi_ref[0] = proj[:, _P_WI:_P_END] * (IDX_HEADS ** -0.5)


def _proj_call(h, g, win, gq, wuq, gkv, wukv, tables, *, interpret=False):
    b, tp, d = h.shape
    tm = TM
    nt = tp // tm
    const = lambda shape: pl.BlockSpec(shape, lambda bi, i: (0,) * len(shape))
    tab = pl.BlockSpec((tm, LANES), lambda bi, i: (i, 0))
    pair = lambda w: pl.BlockSpec((1, N_PAIRS, tm, w), lambda bi, i: (bi, 0, i, 0))
    row = lambda w: pl.BlockSpec((1, tm, w), lambda bi, i: (bi, i, 0))
    sds = jax.ShapeDtypeStruct
    out_shape = (
        sds((b, N_PAIRS, tp, 256), BF16), sds((b, N_PAIRS, tp, 256), BF16), sds((b, N_PAIRS, tp, 128), BF16),
        sds((b, N_PAIRS, tp, 128), BF16), sds((b, N_PAIRS, tp, 128), BF16), sds((b, N_PAIRS, tp, 128), BF16),
        sds((b, N_PAIRS, tp, 128), BF16), sds((b, tp, 128), BF16), sds((b, tp, 128), F32))
    return pl.pallas_call(
        _proj_kernel,
        grid=(b, nt),
        in_specs=[row(d), const((1, d)), const(win.shape), const((1, MLA_Q_RANK)), const(wuq.shape),
                  const((1, MLA_KV_RANK)), const(wukv.shape), tab, tab, tab, tab],
        out_specs=(pair(256), pair(256), pair(128), pair(128), pair(128), pair(128), pair(128),
                   row(128), row(128)),
        out_shape=out_shape,
        compiler_params=pltpu.CompilerParams(
            dimension_semantics=("arbitrary", "arbitrary"), vmem_limit_bytes=VMEM_LIMIT),
        interpret=interpret,
        name="proj",
    )(h, g, win, gq, wuq, gkv, wukv, *tables)


_NT = (((1,), (1,)), ((), ()))


def _stage_scores(k, qm_scr, s_ref):
    s_ref[...] = lax.dot_general(k, qm_scr[...], _NT, preferred_element_type=F32)


def _stage_softmax(s, p_ref, a_ref, m_scr, l_scr):
    m_prev = m_scr[...]
    m_new = jnp.maximum(m_prev, jnp.max(s, axis=0, keepdims=True))
    alpha = jnp.exp(m_prev - m_new)
    p = jnp.exp(s - m_new[0:1])
    l_scr[...] = alpha * l_scr[...] + jnp.sum(p, axis=0, keepdims=True)
    m_scr[...] = m_new
    p_ref[...] = p.astype(BF16)
    a_ref[...] = alpha


def _stage_pv(p_ref, a_ref, v_t, acc_scr):
    acc_scr[...] = a_ref[0:1] * acc_scr[...] + jnp.dot(v_t, p_ref[...], preferred_element_type=F32)


def _flash_init(m_scr, l_scr, acc_scr, p1, a1):
    m_scr[...] = jnp.full(m_scr.shape, NEG_BIG, F32)
    l_scr[...] = jnp.zeros(l_scr.shape, F32)
    acc_scr[...] = jnp.zeros(acc_scr.shape, F32)
    p1[...] = jnp.zeros(p1.shape, BF16)
    a1[...] = jnp.ones(a1.shape, F32)


def _flash_finish(o_ref, l_scr, acc_scr, tq):
    o = (acc_scr[...] / l_scr[0:1]).T
    lane = lax.broadcasted_iota(I32, (tq, LANES), 1)
    o_ref[0] = jnp.where(lane < 64, o[0:tq], o[tq:2 * tq]).astype(o_ref.dtype)


def _chunk_counts(i, tq):
    n_chunks = ((i + 1) * tq + TK - 1) // TK
    return n_chunks, (n_chunks + 1) // 2


_FLASH_SCRATCH = lambda tq: [
    pltpu.VMEM((TK, 2 * tq), F32), pltpu.VMEM((TK, 2 * tq), F32),
    pltpu.VMEM((TK, 2 * tq), BF16), pltpu.VMEM((TK, 2 * tq), BF16),
    pltpu.VMEM((SUBLANES, 2 * tq), F32), pltpu.VMEM((SUBLANES, 2 * tq), F32),
    pltpu.VMEM((SUBLANES, 2 * tq), F32), pltpu.VMEM((SUBLANES, 2 * tq), F32),
    pltpu.VMEM((LANES, 2 * tq), F32)]


def _row_scores(qm_scr, k, s_ref):
    s_ref[...] = lax.dot_general(qm_scr[...], k, _NT, preferred_element_type=F32)


def _row_softmax(s, p_ref, a_ref, m_scr, l_scr):
    m_prev = m_scr[...]
    m_new = jnp.maximum(m_prev, jnp.max(s, axis=-1, keepdims=True))
    alpha = jnp.exp(m_prev - m_new)
    p = jnp.exp(s - jnp.concatenate([m_new] * (s.shape[1] // LANES), axis=1))
    l_scr[...] = alpha * l_scr[...] + jnp.sum(p, axis=-1, keepdims=True)
    m_scr[...] = m_new
    p_ref[...] = p.astype(BF16)
    a_ref[...] = alpha


def _row_pv(p_ref, a_ref, v, acc_scr):
    acc_scr[...] = a_ref[...] * acc_scr[...] + jnp.dot(p_ref[...], v, preferred_element_type=F32)


def _row_finish(o_ref, l_scr, acc_scr, tq):
    lane = lax.broadcasted_iota(I32, (tq, LANES), 1)
    oa = acc_scr[0:tq] / l_scr[0:tq]
    ob = acc_scr[tq:2 * tq] / l_scr[tq:2 * tq]
    o_ref[0] = jnp.where(lane < 64, oa, ob).astype(o_ref.dtype)


_ROW_SCRATCH = lambda tq: [
    pltpu.VMEM((2 * tq, TK), F32), pltpu.VMEM((2 * tq, TK), F32),
    pltpu.VMEM((2 * tq, TK), BF16), pltpu.VMEM((2 * tq, TK), BF16),
    pltpu.VMEM((2 * tq, LANES), F32), pltpu.VMEM((2 * tq, LANES), F32),
    pltpu.VMEM((2 * tq, LANES), F32), pltpu.VMEM((2 * tq, LANES), F32),
    pltpu.VMEM((2 * tq, LANES), F32)]


def _mla_kernel(q_ref, k_ref, v_ref, o_ref, qm_scr, s0, s1, p0, p1, a0, a1, m_scr, l_scr, acc_scr,
                *, tq, n_q_blocks):
    i = pl.program_id(2)

    @pl.when(i >= n_q_blocks)
    def _pad_rows():
        o_ref[...] = jnp.zeros(o_ref.shape, o_ref.dtype)

    @pl.when(i < n_q_blocks)
    def _attend():
        q = q_ref[0, 0]
        lane = lax.broadcasted_iota(I32, (tq, 256), 1)
        in_a = (lane < 64) | ((lane >= 128) & (lane < 144)) | ((lane >= 192) & (lane < 208))
        in_b = ((lane >= 64) & (lane < 128)) | ((lane >= 144) & (lane < 160)) | ((lane >= 208) & (lane < 224))
        zero = jnp.zeros_like(q)
        qm_scr[0:tq] = jnp.where(in_a, q, zero)
        qm_scr[tq:2 * tq] = jnp.where(in_b, q, zero)
        _flash_init(m_scr, l_scr, acc_scr, p1, a1)
        n_chunks, n_pairs = _chunk_counts(i, tq)

        def rows(c):
            cc = jnp.clip(c, 0, n_chunks - 1)
            return pl.ds(pl.multiple_of(cc * TK, TK), TK)

        k_at = lambda c: k_ref[0, 0, rows(c), :]
        v_at = lambda c: v_ref[0, 0, rows(c), :]

        def masked(s_ref, c):
            qry = (lax.broadcasted_iota(I32, (2 * tq, TK), 0) & (tq - 1)) + i * tq
            key = lax.broadcasted_iota(I32, (2 * tq, TK), 1) + c * TK
            return jnp.where(key <= qry, s_ref[...], NEG_BIG)

        _row_scores(qm_scr, k_at(0), s0)

        def pair(j, carry):
            c = 2 * j
            _row_pv(p1, a1, v_at(c - 1), acc_scr)
            _row_softmax(s0[...], p0, a0, m_scr, l_scr)
            _row_scores(qm_scr, k_at(c + 1), s1)
            _row_pv(p0, a0, v_at(c), acc_scr)
            _row_softmax(s1[...], p1, a1, m_scr, l_scr)
            _row_scores(qm_scr, k_at(c + 2), s0)
            return carry

        lax.fori_loop(0, n_pairs - 1, pair, 0)
        c = 2 * (n_pairs - 1)
        _row_pv(p1, a1, v_at(c - 1), acc_scr)
        _row_softmax(masked(s0, c), p0, a0, m_scr, l_scr)
        _row_scores(qm_scr, k_at(c + 1), s1)
        _row_pv(p0, a0, v_at(c), acc_scr)
        _row_softmax(masked(s1, c + 1), p1, a1, m_scr, l_scr)
        _row_pv(p1, a1, v_at(c + 1), acc_scr)
        _row_finish(o_ref, l_scr, acc_scr, tq)


def _mla_call(qmla, kmla, vmla, *, tq, n_q_blocks, interpret=False):
    b, _, tp, _ = qmla.shape
    return pl.pallas_call(
        functools.partial(_mla_kernel, tq=tq, n_q_blocks=n_q_blocks),
        grid=(b, N_PAIRS, tp // tq),
        in_specs=[pl.BlockSpec((1, 1, tq, 256), lambda bi, p, i: (bi, p, i, 0)),
                  pl.BlockSpec((1, 1, tp, 256), lambda bi, p, i: (bi, p, 0, 0)),
                  pl.BlockSpec((1, 1, tp, 128), lambda bi, p, i: (bi, p, 0, 0))],
        out_specs=pl.BlockSpec((1, tq, LANES), lambda bi, p, i: (bi, i, p)),
        out_shape=jax.ShapeDtypeStruct((b, tp, N_PAIRS * LANES), BF16),
        scratch_shapes=[pltpu.VMEM((2 * tq, 256), BF16)] + _ROW_SCRATCH(tq),
        compiler_params=pltpu.CompilerParams(
            dimension_semantics=("arbitrary", "arbitrary", "arbitrary"), vmem_limit_bytes=VMEM_LIMIT),
        interpret=interpret,
        name="mla_attn",
    )(qmla, kmla, vmla)


def _dsa_kernel(qi_ref, wi_ref, qs_ref, ki_ref, ks_ref, vs_ref, o_ref,
                key_scr, qim_scr, wt_scr, thr_scr, cut_scr,
                qm_scr, s0, s1, p0, p1, a0, a1, m_scr, l_scr, acc_scr, *, tq, topk, n_q_blocks):
    i = pl.program_id(1)
    p = pl.program_id(2)
    n_chunks, n_pairs = _chunk_counts(i, tq)
    n_tiles = n_chunks * (TK // HK)
    slabs = HK // SUBLANES
    lane = lax.broadcasted_iota(I32, (tq, LANES), 1)
    in_a = (lane < 32) | ((lane >= 64) & (lane < 96))

    @pl.when(i >= n_q_blocks)
    def _pad_rows():
        o_ref[...] = jnp.zeros(o_ref.shape, o_ref.dtype)

    @pl.when((i < n_q_blocks) & (p == 0))
    def _select():
        for pp in range(N_PAIRS):
            qp = qi_ref[0, pp].astype(F32)
            zero = jnp.zeros_like(qp)
            qim_scr[:, (2 * pp) * tq:(2 * pp + 1) * tq] = jnp.where(in_a, qp, zero).T.astype(BF16)
            qim_scr[:, (2 * pp + 1) * tq:(2 * pp + 2) * tq] = jnp.where(in_a, zero, qp).T.astype(BF16)
        wt_scr[...] = wi_ref[0].T

        def logits(t):
            kc = ki_ref[0, pl.ds(pl.multiple_of(t * HK, HK), HK), :]
            return jnp.dot(kc, qim_scr[...], preferred_element_type=F32)

        def to_keys(lg, t):
            sc = jnp.zeros((HK, tq), F32)
            for hh in range(IDX_HEADS):
                sc = sc + wt_scr[hh:hh + 1, :] * jnp.maximum(lg[:, hh * tq:(hh + 1) * tq], 0.0)
            bits = pltpu.bitcast(sc, I32)
            key = bits ^ ((bits >> 31) & 0x7FFFFFFF)
            key = key - (key >> 31)
            kpos = lax.broadcasted_iota(I32, (HK, tq), 0) + t * HK
            qpos = lax.broadcasted_iota(I32, (HK, tq), 1) + i * tq
            key_scr[t] = jnp.where(kpos <= qpos, key, INT_MIN)

        def score_chunk(c, carry):
            lg0 = logits(2 * c)
            lg1 = logits(2 * c + 1)
            to_keys(lg0, 2 * c)
            to_keys(lg1, 2 * c + 1)
            return carry

        lax.fori_loop(0, n_chunks, score_chunk, 0)

        def count(pred):
            def body(c, acc):
                for u in range(TK // HK):
                    t = c * (TK // HK) + u
                    kk = key_scr[t].reshape(slabs, SUBLANES, tq)
                    acc = acc + jnp.sum(pred(kk, t).astype(I32), axis=0, dtype=I32)
                return acc
            acc = lax.fori_loop(0, n_chunks, body, jnp.zeros((SUBLANES, tq), I32))
            return jnp.broadcast_to(jnp.sum(acc, axis=0, keepdims=True, dtype=I32), (SUBLANES, tq))

        def bit_step(it, t):
            cand = t + lax.shift_left(jnp.int32(1), lax.convert_element_type(31 - it, I32))
            cnt = count(lambda kk, _: kk >= cand[None])
            return jnp.where(cnt >= topk, cand, t)

        thr = lax.fori_loop(0, 32, bit_step, jnp.full((SUBLANES, tq), INT_MIN, I32))
        thr_scr[...] = thr
        n_gt = count(lambda kk, _: kk > thr[None])
        n_ge = count(lambda kk, _: kk >= thr[None])
        need = topk - n_gt
        cut_scr[...] = jnp.full((SUBLANES, tq), 2 ** 30, I32)
        excess = jnp.max(jnp.where((n_ge > topk) & (thr > INT_MIN), 1, 0))
        kidx = (lax.broadcasted_iota(I32, (slabs, SUBLANES, tq), 0) * SUBLANES
                + lax.broadcasted_iota(I32, (slabs, SUBLANES, tq), 1))

        @pl.when(excess > 0)
        def _ties():
            def idx_step(it, x):
                cand = x + lax.shift_left(jnp.int32(1), lax.convert_element_type(13 - it, I32))
                cnt = count(lambda kk, t: (kk == thr[None]) & ((kidx + t * HK) < cand[None]))
                return jnp.where(cnt < need, cand, x)

            cut_scr[...] = lax.fori_loop(0, 14, idx_step, jnp.zeros((SUBLANES, tq), I32))

        def to_bias(t, carry):
            kk = key_scr[t].reshape(slabs, SUBLANES, tq)
            th = thr_scr[...][None]
            sel = (kk > th) | ((kk == th) & ((kidx + t * HK) <= cut_scr[...][None]) & (th > INT_MIN))
            bias = jnp.where(sel, 0.0, NEG_BIG).astype(F32).reshape(HK, tq)
            key_scr[t] = pltpu.bitcast(bias.T, I32)
            return carry

        lax.fori_loop(0, n_tiles, to_bias, 0)

        def fill(t, carry):
            key_scr[t] = pltpu.bitcast(jnp.full((HK, tq), NEG_BIG, F32), I32)
            return carry

        lax.fori_loop(n_tiles, 2 * n_pairs * (TK // HK), fill, 0)

    @pl.when(i < n_q_blocks)
    def _attend():
        q = qs_ref[0, p]
        zero = jnp.zeros_like(q)
        qm_scr[0:tq] = jnp.where(in_a, q, zero)
        qm_scr[tq:2 * tq] = jnp.where(in_a, zero, q)
        _flash_init(m_scr, l_scr, acc_scr, p1, a1)

        def rows(c):
            cc = jnp.clip(c, 0, n_chunks - 1)
            return pl.ds(pl.multiple_of(cc * TK, TK), TK)

        k_at = lambda c: ks_ref[0, p, rows(c), :]
        v_at = lambda c: vs_ref[0, p, rows(c), :]

        def biased(s_ref, c):
            bias = jnp.concatenate(
                [pltpu.bitcast(key_scr[c * (TK // HK) + t], F32) for t in range(TK // HK)], axis=1)
            return s_ref[...] + jnp.concatenate([bias, bias], axis=0)

        _row_scores(qm_scr, k_at(0), s0)

        def pair(j, carry):
            c = 2 * j
            _row_pv(p1, a1, v_at(c - 1), acc_scr)
            _row_softmax(biased(s0, c), p0, a0, m_scr, l_scr)
            _row_scores(qm_scr, k_at(c + 1), s1)
            _row_pv(p0, a0, v_at(c), acc_scr)
            _row_softmax(biased(s1, c + 1), p1, a1, m_scr, l_scr)
            _row_scores(qm_scr, k_at(c + 2), s0)
            return carry

        lax.fori_loop(0, n_pairs, pair, 0)
        _row_pv(p1, a1, v_at(2 * n_pairs - 1), acc_scr)
        _row_finish(o_ref, l_scr, acc_scr, tq)


def _dsa_call(qi, wi, qs, ki, ks, vs, *, tq, topk, n_q_blocks, interpret=False):
    b, _, tp, _ = qs.shape
    assert HK == tq, "mask tiles are transposed in place"
    once = pl.Buffered(1)
    max_pairs = (tp // TK + 1) // 2
    return pl.pallas_call(
        functools.partial(_dsa_kernel, tq=tq, topk=topk, n_q_blocks=n_q_blocks),
        grid=(b, tp // tq, N_PAIRS),
        in_specs=[pl.BlockSpec((1, N_PAIRS, tq, LANES), lambda bi, i, p: (bi, 0, i, 0)),
                  pl.BlockSpec((1, tq, LANES), lambda bi, i, p: (bi, i, 0)),
                  pl.BlockSpec((1, N_PAIRS, tq, LANES), lambda bi, i, p: (bi, 0, i, 0)),
                  pl.BlockSpec((1, tp, LANES), lambda bi, i, p: (bi, 0, 0), pipeline_mode=once),
                  pl.BlockSpec((1, N_PAIRS, tp, LANES), lambda bi, i, p: (bi, 0, 0, 0), pipeline_mode=once),
                  pl.BlockSpec((1, N_PAIRS, tp, LANES), lambda bi, i, p: (bi, 0, 0, 0), pipeline_mode=once)],
        out_specs=pl.BlockSpec((1, tq, LANES), lambda bi, i, p: (bi, i, p)),
        out_shape=jax.ShapeDtypeStruct((b, tp, N_PAIRS * LANES), BF16),
        scratch_shapes=[pltpu.VMEM((2 * max_pairs * (TK // HK), HK, tq), I32),
                        pltpu.VMEM((LANES, IDX_HEADS * tq), BF16),
                        pltpu.VMEM((LANES, tq), F32),
                        pltpu.VMEM((SUBLANES, tq), I32),
                        pltpu.VMEM((SUBLANES, tq), I32),
                        pltpu.VMEM((2 * tq, LANES), BF16)] + _ROW_SCRATCH(tq),
        compiler_params=pltpu.CompilerParams(
            dimension_semantics=("arbitrary", "arbitrary", "arbitrary"), vmem_limit_bytes=VMEM_LIMIT),
        interpret=interpret,
        name="dsa_attn",
    )(qi, wi, qs, ki, ks, vs)


def _ffn_kernel(h_ref, oa_ref, ob_ref, wo_ref, g2_ref, wg_ref, wu_ref, wd_ref, gf_ref, out_ref):
    o = jnp.concatenate([oa_ref[0], ob_ref[0]], axis=1)
    h1 = h_ref[0] + jnp.dot(o, wo_ref[...], preferred_element_type=F32)
    u = _rms(h1, g2_ref[...]).astype(BF16)
    gate = jnp.dot(u, wg_ref[...], preferred_element_type=F32)
    up = jnp.dot(u, wu_ref[...], preferred_element_type=F32)
    act = (gate * jax.nn.sigmoid(gate) * up).astype(BF16)
    h2 = h1 + jnp.dot(act, wd_ref[...], preferred_element_type=F32)
    out_ref[0] = _rms(h2, gf_ref[...])


def _ffn_call(h, oa, ob, wo, g2, wg, wu, wd, gf, *, interpret=False):
    b, tp, d = h.shape
    tm = TM
    once = pl.Buffered(1)
    const = lambda shape: pl.BlockSpec(shape, lambda bi, i: (0,) * len(shape), pipeline_mode=once)
    row = lambda w: pl.BlockSpec((1, tm, w), lambda bi, i: (bi, i, 0))
    return pl.pallas_call(
        _ffn_kernel,
        grid=(b, tp // tm),
        in_specs=[row(d), row(oa.shape[-1]), row(ob.shape[-1]), const(wo.shape), const((1, d)),
                  const(wg.shape), const(wu.shape), const(wd.shape), const((1, d))],
        out_specs=row(d),
        out_shape=jax.ShapeDtypeStruct((b, tp, d), F32),
        compiler_params=pltpu.CompilerParams(
            dimension_semantics=("arbitrary", "arbitrary"), vmem_limit_bytes=VMEM_LIMIT),
        interpret=interpret,
        name="ffn",
    )(h, oa, ob, wo, g2, wg, wu, wd, gf)


def _layer(h, t_real, attn_norm_g, w_in, mla_q_norm_g, w_uq, mla_kv_norm_g, w_ukv, w_o, ffn_norm_g,
           w_gate, w_up, w_down, out_g, topk, *, interpret=False):
    tp = h.shape[1]
    n_q_blocks = -(-t_real // TQ)
    win = _take_cols(w_in, _w_in_cols()).astype(BF16)
    wuq = _take_cols(w_uq, _w_uq_cols()).astype(BF16)
    wukv = jnp.take(w_ukv, jnp.asarray(_w_ukv_cols()), axis=1).astype(BF16)
    tables = _rope_tables(tp)
    qmla, kmla, vmla, qs, ks, vs, qi, ki, wi = _proj_call(
        h, attn_norm_g[None], win, mla_q_norm_g[None], wuq, mla_kv_norm_g[None], wukv, tables,
        interpret=interpret)
    o_mla = _mla_call(qmla, kmla, vmla, tq=TQ, n_q_blocks=n_q_blocks, interpret=interpret)
    o_dsa = _dsa_call(qi, wi, qs, ki, ks, vs, tq=TQ, topk=topk, n_q_blocks=n_q_blocks, interpret=interpret)
    return _ffn_call(h, o_mla, o_dsa, w_o.astype(BF16), ffn_norm_g[None], w_gate.astype(BF16),
                     w_up.astype(BF16), w_down.astype(BF16), out_g[None], interpret=interpret)


def kernel(x, meta_tokens, attn_norm_g, w_in, mla_q_norm_g, w_uq, mla_kv_norm_g, w_ukv, w_o,
           ffn_norm_g, w_gate, w_up, w_down, final_norm_g):
    b, seq, d = x.shape
    depth = w_in.shape[0]
    assert depth == 1, "the final norm is fused into the layer's last kernel"
    topk = min(TOPK_MAX, seq // 4)
    meta = jnp.broadcast_to(meta_tokens[None].astype(x.dtype), (b, N_META, d))
    t = N_META + seq
    tp = -(-t // TK) * TK
    h = jnp.concatenate([meta, x, jnp.zeros((b, tp - t, d), x.dtype)], axis=1)
    out = _layer(h, t, attn_norm_g[0], w_in[0], mla_q_norm_g[0], w_uq[0], mla_kv_norm_g[0], w_ukv[0],
                 w_o[0], ffn_norm_g[0], w_gate[0], w_up[0], w_down[0], final_norm_g, topk)
    return out[:, N_META:t]
```

```python
import functools

import numpy as np
import jax
import jax.numpy as jnp
from jax import lax
from jax.experimental import pallas as pl
from jax.experimental.pallas import tpu as pltpu

F32 = jnp.float32
BF16 = jnp.bfloat16
I32 = jnp.int32

D_MODEL = 1024
N_META = 16
ROPE_THETA = 10000.0
EPS = 1e-6
MLA_HEADS = 8
MLA_Q_RANK = 256
MLA_KV_RANK = 128
MLA_NOPE_DIM = 64
MLA_ROPE_DIM = 32
MLA_QK_DIM = MLA_NOPE_DIM + MLA_ROPE_DIM
MLA_V_DIM = 64
DSA_HEADS = 8
DSA_HEAD_DIM = 64
IDX_HEADS = 8
IDX_DIM = 64
TOPK_MAX = 256
D_FF = 2816

LANES = 128
SUBLANES = 8
N_PAIRS = 4
NEG_BIG = -1e30
INT_MIN = -2 ** 31
FIELD_GUARDS = 0x8000 - 2 ** 31
VMEM_LIMIT = 58 * 1024 * 1024
TQ = 256
TK = 512
TM = TK
HK = TK // 2

_OFF_CQ, _OFF_CKV, _OFF_KR, _OFF_QS, _OFF_KS, _OFF_VS, _OFF_QI, _OFF_KI, _OFF_WI = (
    0, 256, 384, 416, 928, 1440, 1952, 2464, 2528)
_C_IN = 2536
_P_CQ, _P_CKV, _P_QS, _P_KS, _P_VS, _P_QI, _P_KI, _P_KR, _P_WI, _P_END = (
    0, 256, 384, 896, 1408, 1920, 2432, 2560, 2688, 2816)


def _pair_rope_cols(base, hd=64):
    half = hd // 2
    cols = []
    for p in range(N_PAIRS):
        a, b = 2 * p, 2 * p + 1
        cols += [base + a * hd + d for d in range(half)]
        cols += [base + b * hd + d for d in range(half)]
        cols += [base + a * hd + half + d for d in range(half)]
        cols += [base + b * hd + half + d for d in range(half)]
    return cols


def _w_in_cols():
    z = _C_IN
    cols = list(range(_OFF_CQ, _OFF_CQ + 256)) + list(range(_OFF_CKV, _OFF_CKV + 128))
    cols += _pair_rope_cols(_OFF_QS) + _pair_rope_cols(_OFF_KS)
    cols += list(range(_OFF_VS, _OFF_VS + 512))
    cols += _pair_rope_cols(_OFF_QI)
    ki1 = [_OFF_KI + d for d in range(32)]
    ki2 = [_OFF_KI + 32 + d for d in range(32)]
    cols += ki1 + ki1 + ki2 + ki2
    kr1 = [_OFF_KR + d for d in range(16)]
    kr2 = [_OFF_KR + 16 + d for d in range(16)]
    cols += kr1 + kr1 + [z] * 32 + kr2 + kr2 + [z] * 32
    cols += [_OFF_WI + d for d in range(IDX_HEADS)] + [z] * (LANES - IDX_HEADS)
    assert len(cols) == _P_END
    return np.asarray(cols, np.int32)


def _w_uq_cols():
    z = MLA_HEADS * MLA_QK_DIM
    cols = []
    for p in range(N_PAIRS):
        a, b = 2 * p, 2 * p + 1
        cols += [a * MLA_QK_DIM + d for d in range(64)] + [b * MLA_QK_DIM + d for d in range(64)]
        cols += [a * MLA_QK_DIM + 64 + d for d in range(16)] + [b * MLA_QK_DIM + 64 + d for d in range(16)]
        cols += [z] * 32
        cols += [a * MLA_QK_DIM + 80 + d for d in range(16)] + [b * MLA_QK_DIM + 80 + d for d in range(16)]
        cols += [z] * 32
    return np.asarray(cols, np.int32)


def _w_ukv_cols():
    cols = [h * 128 + d for h in range(MLA_HEADS) for d in range(64)]
    cols += [h * 128 + 64 + d for h in range(MLA_HEADS) for d in range(64)]
    return np.asarray(cols, np.int32)


def _take_cols(w, cols):
    wz = jnp.concatenate([w, jnp.zeros((w.shape[0], 1), w.dtype)], axis=1)
    return jnp.take(wz, jnp.asarray(cols), axis=1)


def _rope_tables(tp):
    pos = jnp.arange(tp, dtype=jnp.int32).astype(F32)

    def cs(half):
        inv = jnp.power(ROPE_THETA, -jnp.arange(half, dtype=F32) / half)
        ang = pos[:, None] * inv[None, :]
        return jnp.cos(ang), jnp.sin(ang)

    c32, s32 = cs(32)
    c16, s16 = cs(16)
    z = jnp.zeros((tp, 32), F32)
    c64t = jnp.concatenate([c32, c32, c32, c32], axis=1)
    s64t = jnp.concatenate([-s32, -s32, s32, s32], axis=1)
    c32t = jnp.concatenate([c16, c16, z, c16, c16, z], axis=1)
    s32t = jnp.concatenate([-s16, -s16, z, s16, s16, z], axis=1)
    return c64t, s64t, c32t, s32t


def _rms(x, g):
    return x * lax.rsqrt(jnp.mean(x * x, axis=-1, keepdims=True) + EPS) * g


def _rope_slab(x, c, s):
    return x * c + pltpu.roll(x, 64, axis=1) * s


def _proj_kernel(h_ref, g_ref, win_ref, gq_ref, wuq_ref, gkv_ref, wukv_ref,
                 c64_ref, s64_ref, c32_ref, s32_ref,
                 qmla_ref, kmla_ref, vmla_ref, qs_ref, ks_ref, vs_ref, qi_ref, ki_ref, wi_ref):
    u = _rms(h_ref[0], g_ref[...]).astype(BF16)
    proj = jnp.dot(u, win_ref[...], preferred_element_type=F32)
    c64, s64, c32, s32 = c64_ref[...], s64_ref[...], c32_ref[...], s32_ref[...]

    cq = _rms(proj[:, _P_CQ:_P_CQ + 256], gq_ref[...]).astype(BF16)
    q = jnp.dot(cq, wuq_ref[...], preferred_element_type=F32)
    ckv = _rms(proj[:, _P_CKV:_P_CKV + 128], gkv_ref[...]).astype(BF16)
    kv = jnp.dot(ckv, wukv_ref[...], preferred_element_type=F32)
    kr = _rope_slab(proj[:, _P_KR:_P_KR + 128], c32, s32).astype(BF16)
    q_scale = MLA_QK_DIM ** -0.5
    s_scale = DSA_HEAD_DIM ** -0.5
    i_scale = IDX_DIM ** -0.5
    for p in range(N_PAIRS):
        qn = q[:, p * 256:p * 256 + 128] * q_scale
        qr = _rope_slab(q[:, p * 256 + 128:p * 256 + 256], c32, s32) * q_scale
        qmla_ref[0, p, :, 0:128] = qn.astype(BF16)
        qmla_ref[0, p, :, 128:256] = qr.astype(BF16)
        kmla_ref[0, p, :, 0:128] = kv[:, p * 128:(p + 1) * 128].astype(BF16)
        kmla_ref[0, p, :, 128:256] = kr
        vmla_ref[0, p] = kv[:, 512 + p * 128:512 + (p + 1) * 128].astype(BF16)
        sl = slice(p * 128, (p + 1) * 128)
        qs_ref[0, p] = (_rope_slab(proj[:, _P_QS:_P_KS][:, sl], c64, s64) * s_scale).astype(BF16)
        ks_ref[0, p] = _rope_slab(proj[:, _P_KS:_P_VS][:, sl], c64, s64).astype(BF16)
        vs_ref[0, p] = proj[:, _P_VS:_P_QI][:, sl].astype(BF16)
        qi_ref[0, p] = (_rope_slab(proj[:, _P_QI:_P_KI][:, sl], c64, s64) * i_scale).astype(BF16)
    ki_ref[0] = _rope_slab(proj[:, _P_KI:_P_KR], c64, s64).astype(BF16)
    wi_ref[0] = proj[:, _P_WI:_P_END] * (IDX_HEADS ** -0.5)


def _proj_call(h, g, win, gq, wuq, gkv, wukv, tables, *, interpret=False):
    b, tp, d = h.shape
    tm = TM
    nt = tp // tm
    const = lambda shape: pl.BlockSpec(shape, lambda bi, i: (0,) * len(shape))
    tab = pl.BlockSpec((tm, LANES), lambda bi, i: (i, 0))
    pair = lambda w: pl.BlockSpec((1, N_PAIRS, tm, w), lambda bi, i: (bi, 0, i, 0))
    row = lambda w: pl.BlockSpec((1, tm, w), lambda bi, i: (bi, i, 0))
    sds = jax.ShapeDtypeStruct
    out_shape = (
        sds((b, N_PAIRS, tp, 256), BF16), sds((b, N_PAIRS, tp, 256), BF16), sds((b, N_PAIRS, tp, 128), BF16),
        sds((b, N_PAIRS, tp, 128), BF16), sds((b, N_PAIRS, tp, 128), BF16), sds((b, N_PAIRS, tp, 128), BF16),
        sds((b, N_PAIRS, tp, 128), BF16), sds((b, tp, 128), BF16), sds((b, tp, 128), F32))
    return pl.pallas_call(
        _proj_kernel,
        grid=(b, nt),
        in_specs=[row(d), const((1, d)), const(win.shape), const((1, MLA_Q_RANK)), const(wuq.shape),
                  const((1, MLA_KV_RANK)), const(wukv.shape), tab, tab, tab, tab],
        out_specs=(pair(256), pair(256), pair(128), pair(128), pair(128), pair(128), pair(128),
                   row(128), row(128)),
        out_shape=out_shape,
        compiler_params=pltpu.CompilerParams(
            dimension_semantics=("arbitrary", "arbitrary"), vmem_limit_bytes=VMEM_LIMIT),
        interpret=interpret,
        name="proj",
    )(h, g, win, gq, wuq, gkv, wukv, *tables)


_NT = (((1,), (1,)), ((), ()))
N_STREAMS = 1


def _srl(x, n):
    return lax.shift_right_logical(x, jnp.full(x.shape, n, x.dtype))


def _chunk_counts(i, tq):
    n_chunks = ((i + 1) * tq + TK - 1) // TK
    return n_chunks, (n_chunks + 1) // 2


class _Stream:
    def __init__(self, g, qm, s, p, a, m, l, acc):
        self.qm = qm.at[g]
        self.s = (s.at[0, g], s.at[1, g])
        self.p = (p.at[0, g], p.at[1, g])
        self.a = (a.at[0, g], a.at[1, g])
        self.m, self.l, self.acc = m.at[g], l.at[g], acc.at[g]

    def init(self):
        self.m[...] = jnp.full(self.m.shape, NEG_BIG, F32)
        self.l[...] = jnp.zeros(self.l.shape, F32)
        self.acc[...] = jnp.zeros(self.acc.shape, F32)
        self.p[1][...] = jnp.zeros(self.p[1].shape, BF16)
        self.a[1][...] = jnp.ones(self.a[1].shape, F32)

    def scores(self, k, slot):
        self.s[slot][...] = lax.dot_general(self.qm[...], k, _NT, preferred_element_type=F32)

    def softmax(self, s, slot):
        m_prev = self.m[...]
        m_new = jnp.maximum(m_prev, jnp.max(s, axis=-1, keepdims=True))
        alpha = jnp.exp(m_prev - m_new)
        p = jnp.exp(s - jnp.concatenate([m_new] * (s.shape[1] // LANES), axis=1))
        self.l[...] = alpha * self.l[...] + jnp.sum(p, axis=-1, keepdims=True)
        self.m[...] = m_new
        self.p[slot][...] = p.astype(BF16)
        self.a[slot][...] = alpha

    def pv(self, v, slot):
        self.acc[...] = self.a[slot][...] * self.acc[...] + jnp.dot(self.p[slot][...], v, preferred_element_type=F32)

    def finish(self, tq):
        lane = lax.broadcasted_iota(I32, (tq, LANES), 1)
        oa = self.acc[0:tq] / self.l[0:tq]
        ob = self.acc[tq:2 * tq] / self.l[tq:2 * tq]
        return jnp.where(lane < 64, oa, ob)


def _stream_scratch(tq, width):
    g = N_STREAMS
    return [pltpu.VMEM((g, 2 * tq, width), BF16),
            pltpu.VMEM((2, g, 2 * tq, TK), F32), pltpu.VMEM((2, g, 2 * tq, TK), BF16),
            pltpu.VMEM((2, g, 2 * tq, LANES), F32),
            pltpu.VMEM((g, 2 * tq, LANES), F32), pltpu.VMEM((g, 2 * tq, LANES), F32),
            pltpu.VMEM((g, 2 * tq, LANES), F32)]


def _pair_body(streams, c, prep, last):
    for st, _, v_at in streams:
        st.pv(v_at(c - 1), 1)
    for st, _, _ in streams:
        st.softmax(prep(st.s[0], c), 0)
    for st, k_at, _ in streams:
        st.scores(k_at(c + 1), 1)
    for st, _, v_at in streams:
        st.pv(v_at(c), 0)
    for st, _, _ in streams:
        st.softmax(prep(st.s[1], c + 1), 1)
    for st, k_at, v_at in streams:
        if last:
            st.pv(v_at(c + 1), 1)
        else:
            st.scores(k_at(c + 2), 0)


def _mla_kernel(q_ref, k_ref, v_ref, o_ref, qm, s, p, a, m, l, acc, *, tq, n_q_blocks):
    i = pl.program_id(2)

    @pl.when(i >= n_q_blocks)
    def _pad_rows():
        o_ref[...] = jnp.zeros(o_ref.shape, o_ref.dtype)

    @pl.when(i < n_q_blocks)
    def _attend():
        lane = lax.broadcasted_iota(I32, (tq, 256), 1)
        in_a = (lane < 64) | ((lane >= 128) & (lane < 144)) | ((lane >= 192) & (lane < 208))
        in_b = ((lane >= 64) & (lane < 128)) | ((lane >= 144) & (lane < 160)) | ((lane >= 208) & (lane < 224))
        n_chunks, n_pairs = _chunk_counts(i, tq)

        def rows(c):
            cc = jnp.clip(c, 0, n_chunks - 1)
            return pl.ds(pl.multiple_of(cc * TK, TK), TK)

        streams = []
        for g in range(N_STREAMS):
            st = _Stream(g, qm, s, p, a, m, l, acc)
            q = q_ref[0, g]
            zero = jnp.zeros_like(q)
            st.qm[0:tq] = jnp.where(in_a, q, zero)
            st.qm[tq:2 * tq] = jnp.where(in_b, q, zero)
            st.init()
            streams.append((st, lambda c, g=g: k_ref[0, g, rows(c), :], lambda c, g=g: v_ref[0, g, rows(c), :]))

        def masked(s_ref, c):
            qry = (lax.broadcasted_iota(I32, (2 * tq, TK), 0) & (tq - 1)) + i * tq
            key = lax.broadcasted_iota(I32, (2 * tq, TK), 1) + c * TK
            return jnp.where(key <= qry, s_ref[...], NEG_BIG)

        for st, k_at, _ in streams:
            st.scores(k_at(0), 0)

        def pair(j, carry):
            _pair_body(streams, 2 * j, lambda s_ref, c: s_ref[...], last=False)
            return carry

        lax.fori_loop(0, n_pairs - 1, pair, 0)
        _pair_body(streams, 2 * (n_pairs - 1), masked, last=True)
        for g, (st, _, _) in enumerate(streams):
            o_ref[0, :, g * LANES:(g + 1) * LANES] = st.finish(tq).astype(o_ref.dtype)


def _mla_call(qmla, kmla, vmla, *, tq, n_q_blocks, interpret=False):
    b, _, tp, _ = qmla.shape
    g = N_STREAMS
    return pl.pallas_call(
        functools.partial(_mla_kernel, tq=tq, n_q_blocks=n_q_blocks),
        grid=(b, N_PAIRS // g, tp // tq),
        in_specs=[pl.BlockSpec((1, g, tq, 256), lambda bi, pg, i: (bi, pg, i, 0)),
                  pl.BlockSpec((1, g, tp, 256), lambda bi, pg, i: (bi, pg, 0, 0)),
                  pl.BlockSpec((1, g, tp, 128), lambda bi, pg, i: (bi, pg, 0, 0))],
        out_specs=pl.BlockSpec((1, tq, g * LANES), lambda bi, pg, i: (bi, i, pg)),
        out_shape=jax.ShapeDtypeStruct((b, tp, N_PAIRS * LANES), BF16),
        scratch_shapes=_stream_scratch(tq, 256),
        compiler_params=pltpu.CompilerParams(
            dimension_semantics=("arbitrary", "arbitrary", "arbitrary"), vmem_limit_bytes=VMEM_LIMIT),
        interpret=interpret,
        name="mla_attn",
    )(qmla, kmla, vmla)


def _dsa_kernel(qi_ref, wi_ref, qs_ref, ki_ref, ks_ref, vs_ref, o_ref,
                key_scr, fld_scr, qim_scr, wt_scr, thr_scr, cut_scr,
                qm, s, p, a, m, l, acc, *, tq, topk, n_q_blocks):
    i = pl.program_id(1)
    pg = pl.program_id(2)
    n_chunks, n_pairs = _chunk_counts(i, tq)
    n_tiles = n_chunks * (TK // HK)
    slabs = HK // SUBLANES
    lane = lax.broadcasted_iota(I32, (tq, LANES), 1)
    in_a = (lane < 32) | ((lane >= 64) & (lane < 96))

    @pl.when(i >= n_q_blocks)
    def _pad_rows():
        o_ref[...] = jnp.zeros(o_ref.shape, o_ref.dtype)

    @pl.when((i < n_q_blocks) & (pg == 0))
    def _select():
        for pp in range(N_PAIRS):
            qp = qi_ref[0, pp].astype(F32)
            zero = jnp.zeros_like(qp)
            qim_scr[:, (2 * pp) * tq:(2 * pp + 1) * tq] = jnp.where(in_a, qp, zero).T.astype(BF16)
            qim_scr[:, (2 * pp + 1) * tq:(2 * pp + 2) * tq] = jnp.where(in_a, zero, qp).T.astype(BF16)
        wt_scr[...] = wi_ref[0].T

        def logits(t):
            kc = ki_ref[0, pl.ds(pl.multiple_of(t * HK, HK), HK), :]
            return jnp.dot(kc, qim_scr[...], preferred_element_type=F32)

        def to_keys(lg, t):
            sc = jnp.zeros((HK, tq), F32)
            for hh in range(IDX_HEADS):
                sc = sc + wt_scr[hh:hh + 1, :] * jnp.maximum(lg[:, hh * tq:(hh + 1) * tq], 0.0)
            bits = pltpu.bitcast(sc, I32)
            key = bits ^ ((bits >> 31) & 0x7FFFFFFF)
            key = key - (key >> 31)
            kpos = lax.broadcasted_iota(I32, (HK, tq), 0) + t * HK
            qpos = lax.broadcasted_iota(I32, (HK, tq), 1) + i * tq
            key = jnp.where(kpos <= qpos, key, INT_MIN)
            key_scr[t] = key
            return key

        def pack(fa, fb):
            return (fa << 16) | fb | FIELD_GUARDS

        def score_chunk(c, carry):
            lg0 = logits(2 * c)
            lg1 = logits(2 * c + 1)
            k0 = to_keys(lg0, 2 * c)
            k1 = to_keys(lg1, 2 * c + 1)
            top = lambda key: _srl(key ^ INT_MIN, 17)
            fld_scr[c] = pack(top(k0), top(k1))
            return carry

        lax.fori_loop(0, n_chunks, score_chunk, 0)

        def count_fields(cand):
            both = ((cand << 16) | cand)[None]

            def body(c, acc):
                w = fld_scr[c].reshape(slabs, SUBLANES, tq)
                hit = _srl(w - both, 15) & 0x00010001
                return acc + jnp.sum(hit, axis=0, dtype=I32)
            acc = lax.fori_loop(0, n_chunks, body, jnp.zeros((SUBLANES, tq), I32))
            acc = (acc & 0xFFFF) + _srl(acc, 16)
            return jnp.broadcast_to(jnp.sum(acc, axis=0, keepdims=True, dtype=I32), (SUBLANES, tq))

        def field_search(above):
            def step(it, t):
                cand = t | lax.shift_left(jnp.int32(1), lax.convert_element_type(14 - it, I32))
                return jnp.where(count_fields(cand) + above >= topk, cand, t)
            return lax.fori_loop(0, 15, step, jnp.zeros((SUBLANES, tq), I32))

        t_top = field_search(jnp.zeros((SUBLANES, tq), I32))
        above = count_fields(t_top + 1)

        def mid_fields(c, carry):
            def mid(key):
                u = (key ^ INT_MIN).reshape(slabs, SUBLANES, tq)
                alive = _srl(u, 17) == t_top[None]
                return jnp.where(alive, _srl(u, 2) & 0x7FFF, 0).reshape(HK, tq)
            fld_scr[c] = pack(mid(key_scr[2 * c]), mid(key_scr[2 * c + 1]))
            return carry

        lax.fori_loop(0, n_chunks, mid_fields, 0)
        t_mid = field_search(above)

        def count(pred):
            def body(c, acc):
                for u in range(TK // HK):
                    t = c * (TK // HK) + u
                    kk = key_scr[t].reshape(slabs, SUBLANES, tq)
                    acc = acc + jnp.sum(pred(kk, t).astype(I32), axis=0, dtype=I32)
                return acc
            acc = lax.fori_loop(0, n_chunks, body, jnp.zeros((SUBLANES, tq), I32))
            return jnp.broadcast_to(jnp.sum(acc, axis=0, keepdims=True, dtype=I32), (SUBLANES, tq))

        def bit_step(it, t):
            cand = t | lax.shift_left(jnp.int32(1), lax.convert_element_type(1 - it, I32))
            cnt = count(lambda kk, _: kk >= (cand ^ INT_MIN)[None])
            return jnp.where(cnt >= topk, cand, t)

        thr = lax.fori_loop(0, 2, bit_step, (t_top << 17) | (t_mid << 2)) ^ INT_MIN
        thr_scr[...] = thr
        n_gt = count(lambda kk, _: kk > thr[None])
        n_ge = count(lambda kk, _: kk >= thr[None])
        need = topk - n_gt
        cut_scr[...] = jnp.full((SUBLANES, tq), 2 ** 30, I32)
        excess = jnp.max(jnp.where((n_ge > topk) & (thr > INT_MIN), 1, 0))
        kidx = (lax.broadcasted_iota(I32, (slabs, SUBLANES, tq), 0) * SUBLANES
                + lax.broadcasted_iota(I32, (slabs, SUBLANES, tq), 1))

        @pl.when(excess > 0)
        def _ties():
            def idx_step(it, x):
                cand = x + lax.shift_left(jnp.int32(1), lax.convert_element_type(13 - it, I32))
                cnt = count(lambda kk, t: (kk == thr[None]) & ((kidx + t * HK) < cand[None]))
                return jnp.where(cnt < need, cand, x)

            cut_scr[...] = lax.fori_loop(0, 14, idx_step, jnp.zeros((SUBLANES, tq), I32))

        def to_bias(t, carry):
            kk = key_scr[t].reshape(slabs, SUBLANES, tq)
            th = thr_scr[...][None]
            sel = (kk > th) | ((kk == th) & ((kidx + t * HK) <= cut_scr[...][None]) & (th > INT_MIN))
            bias = jnp.where(sel, 0.0, NEG_BIG).astype(F32).reshape(HK, tq)
            key_scr[t] = pltpu.bitcast(bias.T, I32)
            return carry

        lax.fori_loop(0, n_tiles, to_bias, 0)

        def fill(t, carry):
            key_scr[t] = pltpu.bitcast(jnp.full((HK, tq), NEG_BIG, F32), I32)
            return carry

        lax.fori_loop(n_tiles, 2 * n_pairs * (TK // HK), fill, 0)

    @pl.when(i < n_q_blocks)
    def _attend():
        def rows(c):
            cc = jnp.clip(c, 0, n_chunks - 1)
            return pl.ds(pl.multiple_of(cc * TK, TK), TK)

        streams = []
        for g in range(N_STREAMS):
            st = _Stream(g, qm, s, p, a, m, l, acc)
            pr = pg * N_STREAMS + g
            q = qs_ref[0, pr]
            zero = jnp.zeros_like(q)
            st.qm[0:tq] = jnp.where(in_a, q, zero)
            st.qm[tq:2 * tq] = jnp.where(in_a, zero, q)
            st.init()
            streams.append((st, lambda c, pr=pr: ks_ref[0, pr, rows(c), :], lambda c, pr=pr: vs_ref[0, pr, rows(c), :]))

        def biased(s_ref, c):
            bias = jnp.concatenate(
                [pltpu.bitcast(key_scr[c * (TK // HK) + t], F32) for t in range(TK // HK)], axis=1)
            return s_ref[...] + jnp.concatenate([bias, bias], axis=0)

        for st, k_at, _ in streams:
            st.scores(k_at(0), 0)

        def pair(j, carry):
            _pair_body(streams, 2 * j, biased, last=False)
            return carry

        lax.fori_loop(0, n_pairs, pair, 0)
        for g, (st, _, v_at) in enumerate(streams):
            st.pv(v_at(2 * n_pairs - 1), 1)
            o_ref[0, :, g * LANES:(g + 1) * LANES] = st.finish(tq).astype(o_ref.dtype)


def _dsa_call(qi, wi, qs, ki, ks, vs, *, tq, topk, n_q_blocks, interpret=False):
    b, _, tp, _ = qs.shape
    assert HK == tq, "mask tiles are transposed in place"
    once = pl.Buffered(1)
    max_pairs = (tp // TK + 1) // 2
    return pl.pallas_call(
        functools.partial(_dsa_kernel, tq=tq, topk=topk, n_q_blocks=n_q_blocks),
        grid=(b, tp // tq, N_PAIRS // N_STREAMS),
        in_specs=[pl.BlockSpec((1, N_PAIRS, tq, LANES), lambda bi, i, p: (bi, 0, i, 0)),
                  pl.BlockSpec((1, tq, LANES), lambda bi, i, p: (bi, i, 0)),
                  pl.BlockSpec((1, N_PAIRS, tq, LANES), lambda bi, i, p: (bi, 0, i, 0)),
                  pl.BlockSpec((1, tp, LANES), lambda bi, i, p: (bi, 0, 0), pipeline_mode=once),
                  pl.BlockSpec((1, N_PAIRS, tp, LANES), lambda bi, i, p: (bi, 0, 0, 0), pipeline_mode=once),
                  pl.BlockSpec((1, N_PAIRS, tp, LANES), lambda bi, i, p: (bi, 0, 0, 0), pipeline_mode=once)],
        out_specs=pl.BlockSpec((1, tq, N_STREAMS * LANES), lambda bi, i, p: (bi, i, p)),
        out_shape=jax.ShapeDtypeStruct((b, tp, N_PAIRS * LANES), BF16),
        scratch_shapes=[pltpu.VMEM((2 * max_pairs * (TK // HK), HK, tq), I32),
                        pltpu.VMEM((tp // TK, HK, tq), I32),
                        pltpu.VMEM((LANES, IDX_HEADS * tq), BF16),
                        pltpu.VMEM((LANES, tq), F32),
                        pltpu.VMEM((SUBLANES, tq), I32),
                        pltpu.VMEM((SUBLANES, tq), I32)]
        + _stream_scratch(tq, LANES),
        compiler_params=pltpu.CompilerParams(
            dimension_semantics=("arbitrary", "arbitrary", "arbitrary"), vmem_limit_bytes=VMEM_LIMIT),
        interpret=interpret,
        name="dsa_attn",
    )(qi, wi, qs, ki, ks, vs)


def _ffn_kernel(h_ref, oa_ref, ob_ref, wo_ref, g2_ref, wg_ref, wu_ref, wd_ref, gf_ref, out_ref):
    o = jnp.concatenate([oa_ref[0], ob_ref[0]], axis=1)
    h1 = h_ref[0] + jnp.dot(o, wo_ref[...], preferred_element_type=F32)
    u = _rms(h1, g2_ref[...]).astype(BF16)
    gate = jnp.dot(u, wg_ref[...], preferred_element_type=F32)
    up = jnp.dot(u, wu_ref[...], preferred_element_type=F32)
    act = (gate * jax.nn.sigmoid(gate) * up).astype(BF16)
    h2 = h1 + jnp.dot(act, wd_ref[...], preferred_element_type=F32)
    out_ref[0] = _rms(h2, gf_ref[...])


def _ffn_call(h, oa, ob, wo, g2, wg, wu, wd, gf, *, interpret=False):
    b, tp, d = h.shape
    tm = TM
    once = pl.Buffered(1)
    const = lambda shape: pl.BlockSpec(shape, lambda bi, i: (0,) * len(shape), pipeline_mode=once)
    row = lambda w: pl.BlockSpec((1, tm, w), lambda bi, i: (bi, i, 0))
    return pl.pallas_call(
        _ffn_kernel,
        grid=(b, tp // tm),
        in_specs=[row(d), row(oa.shape[-1]), row(ob.shape[-1]), const(wo.shape), const((1, d)),
                  const(wg.shape), const(wu.shape), const(wd.shape), const((1, d))],
        out_specs=row(d),
        out_shape=jax.ShapeDtypeStruct((b, tp, d), F32),
        compiler_params=pltpu.CompilerParams(
            dimension_semantics=("arbitrary", "arbitrary"), vmem_limit_bytes=VMEM_LIMIT),
        interpret=interpret,
        name="ffn",
    )(h, oa, ob, wo, g2, wg, wu, wd, gf)


def _layer(h, t_real, attn_norm_g, w_in, mla_q_norm_g, w_uq, mla_kv_norm_g, w_ukv, w_o, ffn_norm_g,
           w_gate, w_up, w_down, out_g, topk, *, interpret=False):
    tp = h.shape[1]
    n_q_blocks = -(-t_real // TQ)
    win = _take_cols(w_in, _w_in_cols()).astype(BF16)
    wuq = _take_cols(w_uq, _w_uq_cols()).astype(BF16)
    wukv = jnp.take(w_ukv, jnp.asarray(_w_ukv_cols()), axis=1).astype(BF16)
    tables = _rope_tables(tp)
    qmla, kmla, vmla, qs, ks, vs, qi, ki, wi = _proj_call(
        h, attn_norm_g[None], win, mla_q_norm_g[None], wuq, mla_kv_norm_g[None], wukv, tables,
        interpret=interpret)
    o_mla = _mla_call(qmla, kmla, vmla, tq=TQ, n_q_blocks=n_q_blocks, interpret=interpret)
    o_dsa = _dsa_call(qi, wi, qs, ki, ks, vs, tq=TQ, topk=topk, n_q_blocks=n_q_blocks, interpret=interpret)
    return _ffn_call(h, o_mla, o_dsa, w_o.astype(BF16), ffn_norm_g[None], w_gate.astype(BF16),
                     w_up.astype(BF16), w_down.astype(BF16), out_g[None], interpret=interpret)


def kernel(x, meta_tokens, attn_norm_g, w_in, mla_q_norm_g, w_uq, mla_kv_norm_g, w_ukv, w_o,
           ffn_norm_g, w_gate, w_up, w_down, final_norm_g):
    b, seq, d = x.shape
    depth = w_in.shape[0]
    assert depth == 1, "the final norm is fused into the layer's last kernel"
    topk = min(TOPK_MAX, seq // 4)
    meta = jnp.broadcast_to(meta_tokens[None].astype(x.dtype), (b, N_META, d))
    t = N_META + seq
    tp = -(-t // TK) * TK
    h = jnp.concatenate([meta, x, jnp.zeros((b, tp - t, d), x.dtype)], axis=1)
    out = _layer(h, t, attn_norm_g[0], w_in[0], mla_q_norm_g[0], w_uq[0], mla_kv_norm_g[0], w_ukv[0],
                 w_o[0], ffn_norm_g[0], w_gate[0], w_up[0], w_down[0], final_norm_g, topk)
    return out[:, N_META:t]
```

```python
import functools

import numpy as np
import jax
import jax.numpy as jnp
from jax import lax
from jax.experimental import pallas as pl
from jax.experimental.pallas import tpu as pltpu

F32 = jnp.float32
BF16 = jnp.bfloat16
I32 = jnp.int32

D_MODEL = 1024
N_META = 16
ROPE_THETA = 10000.0
EPS = 1e-6
MLA_HEADS = 8
MLA_Q_RANK = 256
MLA_KV_RANK = 128
MLA_NOPE_DIM = 64
MLA_ROPE_DIM = 32
MLA_QK_DIM = MLA_NOPE_DIM + MLA_ROPE_DIM
MLA_V_DIM = 64
DSA_HEADS = 8
DSA_HEAD_DIM = 64
IDX_HEADS = 8
IDX_DIM = 64
TOPK_MAX = 256
D_FF = 2816

LANES = 128
SUBLANES = 8
N_PAIRS = 4
NEG_BIG = -1e30
INT_MIN = -2 ** 31
FIELD_GUARDS = 0x8000 - 2 ** 31
VMEM_LIMIT = 58 * 1024 * 1024
TQ = 256
TK = 512
TM = TK
HK = TK // 2

_OFF_CQ, _OFF_CKV, _OFF_KR, _OFF_QS, _OFF_KS, _OFF_VS, _OFF_QI, _OFF_KI, _OFF_WI = (
    0, 256, 384, 416, 928, 1440, 1952, 2464, 2528)
_C_IN = 2536
_P_CQ, _P_CKV, _P_QS, _P_KS, _P_VS, _P_QI, _P_KI, _P_KR, _P_WI, _P_END = (
    0, 256, 384, 896, 1408, 1920, 2432, 2560, 2688, 2816)


def _pair_rope_cols(base, hd=64):
    half = hd // 2
    cols = []
    for p in range(N_PAIRS):
        a, b = 2 * p, 2 * p + 1
        cols += [base + a * hd + d for d in range(half)]
        cols += [base + b * hd + d for d in range(half)]
        cols += [base + a * hd + half + d for d in range(half)]
        cols += [base + b * hd + half + d for d in range(half)]
    return cols


def _w_in_cols():
    z = _C_IN
    cols = list(range(_OFF_CQ, _OFF_CQ + 256)) + list(range(_OFF_CKV, _OFF_CKV + 128))
    cols += _pair_rope_cols(_OFF_QS) + _pair_rope_cols(_OFF_KS)
    cols += list(range(_OFF_VS, _OFF_VS + 512))
    cols += _pair_rope_cols(_OFF_QI)
    ki1 = [_OFF_KI + d for d in range(32)]
    ki2 = [_OFF_KI + 32 + d for d in range(32)]
    cols += ki1 + ki1 + ki2 + ki2
    kr1 = [_OFF_KR + d for d in range(16)]
    kr2 = [_OFF_KR + 16 + d for d in range(16)]
    cols += kr1 + kr1 + [z] * 32 + kr2 + kr2 + [z] * 32
    cols += [_OFF_WI + d for d in range(IDX_HEADS)] + [z] * (LANES - IDX_HEADS)
    assert len(cols) == _P_END
    return np.asarray(cols, np.int32)


def _w_uq_cols():
    z = MLA_HEADS * MLA_QK_DIM
    cols = []
    for p in range(N_PAIRS):
        a, b = 2 * p, 2 * p + 1
        cols += [a * MLA_QK_DIM + d for d in range(64)] + [b * MLA_QK_DIM + d for d in range(64)]
        cols += [a * MLA_QK_DIM + 64 + d for d in range(16)] + [b * MLA_QK_DIM + 64 + d for d in range(16)]
        cols += [z] * 32
        cols += [a * MLA_QK_DIM + 80 + d for d in range(16)] + [b * MLA_QK_DIM + 80 + d for d in range(16)]
        cols += [z] * 32
    return np.asarray(cols, np.int32)


def _w_ukv_cols():
    cols = [h * 128 + d for h in range(MLA_HEADS) for d in range(64)]
    cols += [h * 128 + 64 + d for h in range(MLA_HEADS) for d in range(64)]
    return np.asarray(cols, np.int32)


def _take_cols(w, cols):
    wz = jnp.concatenate([w, jnp.zeros((w.shape[0], 1), w.dtype)], axis=1)
    return jnp.take(wz, jnp.asarray(cols), axis=1)


def _rope_tables(tp):
    pos = jnp.arange(tp, dtype=jnp.int32).astype(F32)

    def cs(half):
        inv = jnp.power(ROPE_THETA, -jnp.arange(half, dtype=F32) / half)
        ang = pos[:, None] * inv[None, :]
        return jnp.cos(ang), jnp.sin(ang)

    c32, s32 = cs(32)
    c16, s16 = cs(16)
    z = jnp.zeros((tp, 32), F32)
    c64t = jnp.concatenate([c32, c32, c32, c32], axis=1)
    s64t = jnp.concatenate([-s32, -s32, s32, s32], axis=1)
    c32t = jnp.concatenate([c16, c16, z, c16, c16, z], axis=1)
    s32t = jnp.concatenate([-s16, -s16, z, s16, s16, z], axis=1)
    return c64t, s64t, c32t, s32t


def _rms(x, g):
    return x * lax.rsqrt(jnp.mean(x * x, axis=-1, keepdims=True) + EPS) * g


def _rope_slab(x, c, s):
    return x * c + pltpu.roll(x, 64, axis=1) * s


def _proj_kernel(h_ref, g_ref, win_ref, gq_ref, wuq_ref, gkv_ref, wukv_ref,
                 c64_ref, s64_ref, c32_ref, s32_ref,
                 qmla_ref, kmla_ref, vmla_ref, qs_ref, ks_ref, vs_ref, qi_ref, ki_ref, wi_ref):
    u = _rms(h_ref[0], g_ref[...]).astype(BF16)
    proj = jnp.dot(u, win_ref[...], preferred_element_type=F32)
    c64, s64, c32, s32 = c64_ref[...], s64_ref[...], c32_ref[...], s32_ref[...]

    cq = _rms(proj[:, _P_CQ:_P_CQ + 256], gq_ref[...]).astype(BF16)
    q = jnp.dot(cq, wuq_ref[...], preferred_element_type=F32)
    ckv = _rms(proj[:, _P_CKV:_P_CKV + 128], gkv_ref[...]).astype(BF16)
    kv = jnp.dot(ckv, wukv_ref[...], preferred_element_type=F32)
    kr = _rope_slab(proj[:, _P_KR:_P_KR + 128], c32, s32).astype(BF16)
    q_scale = MLA_QK_DIM ** -0.5
    s_scale = DSA_HEAD_DIM ** -0.5
    i_scale = IDX_DIM ** -0.5
    for p in range(N_PAIRS):
        qn = q[:, p * 256:p * 256 + 128] * q_scale
        qr = _rope_slab(q[:, p * 256 + 128:p * 256 + 256], c32, s32) * q_scale
        qmla_ref[0, p, :, 0:128] = qn.astype(BF16)
        qmla_ref[0, p, :, 128:256] = qr.astype(BF16)
        kmla_ref[0, p, :, 0:128] = kv[:, p * 128:(p + 1) * 128].astype(BF16)
        kmla_ref[0, p, :, 128:256] = kr
        vmla_ref[0, p] = kv[:, 512 + p * 128:512 + (p + 1) * 128].astype(BF16)
        sl = slice(p * 128, (p + 1) * 128)
        qs_ref[0, p] = (_rope_slab(proj[:, _P_QS:_P_KS][:, sl], c64, s64) * s_scale).astype(BF16)
        ks_ref[0, p] = _rope_slab(proj[:, _P_KS:_P_VS][:, sl], c64, s64).astype(BF16)
        vs_ref[0, p] = proj[:, _P_VS:_P_QI][:, sl].astype(BF16)
        qi_ref[0, p] = (_rope_slab(proj[:, _P_QI:_P_KI][:, sl], c64, s64) * i_scale).astype(BF16)
    ki_ref[0] = _rope_slab(proj[:, _P_KI:_P_KR], c64, s64).astype(BF16)
    wi_ref[0] = proj[:, _P_WI:_P_END] * (IDX_HEADS ** -0.5)


def _proj_call(h, g, win, gq, wuq, gkv, wukv, tables, *, interpret=False):
    b, tp, d = h.shape
    tm = TM
    nt = tp // tm
    const = lambda shape: pl.BlockSpec(shape, lambda bi, i: (0,) * len(shape))
    tab = pl.BlockSpec((tm, LANES), lambda bi, i: (i, 0))
    pair = lambda w: pl.BlockSpec((1, N_PAIRS, tm, w), lambda bi, i: (bi, 0, i, 0))
    row = lambda w: pl.BlockSpec((1, tm, w), lambda bi, i: (bi, i, 0))
    sds = jax.ShapeDtypeStruct
    out_shape = (
        sds((b, N_PAIRS, tp, 256), BF16), sds((b, N_PAIRS, tp, 256), BF16), sds((b, N_PAIRS, tp, 128), BF16),
        sds((b, N_PAIRS, tp, 128), BF16), sds((b, N_PAIRS, tp, 128), BF16), sds((b, N_PAIRS, tp, 128), BF16),
        sds((b, N_PAIRS, tp, 128), BF16), sds((b, tp, 128), BF16), sds((b, tp, 128), F32))
    return pl.pallas_call(
        _proj_kernel,
        grid=(b, nt),
        in_specs=[row(d), const((1, d)), const(win.shape), const((1, MLA_Q_RANK)), const(wuq.shape),
                  const((1, MLA_KV_RANK)), const(wukv.shape), tab, tab, tab, tab],
        out_specs=(pair(256), pair(256), pair(128), pair(128), pair(128), pair(128), pair(128),
                   row(128), row(128)),
        out_shape=out_shape,
        compiler_params=pltpu.CompilerParams(
            dimension_semantics=("arbitrary", "arbitrary"), vmem_limit_bytes=VMEM_LIMIT),
        interpret=interpret,
        name="proj",
    )(h, g, win, gq, wuq, gkv, wukv, *tables)


_NT = (((1,), (1,)), ((), ()))
N_STREAMS = 1


def _srl(x, n):
    return lax.shift_right_logical(x, jnp.full(x.shape, n, x.dtype))


def _chunk_counts(i, tq):
    n_chunks = ((i + 1) * tq + TK - 1) // TK
    return n_chunks, (n_chunks + 1) // 2


class _Stream:
    def __init__(self, g, qm, s, p, a, m, l, acc):
        self.qm = qm.at[g]
        self.s = (s.at[0, g], s.at[1, g])
        self.p = (p.at[0, g], p.at[1, g])
        self.a = (a.at[0, g], a.at[1, g])
        self.m, self.l, self.acc = m.at[g], l.at[g], acc.at[g]

    def init(self):
        self.m[...] = jnp.full(self.m.shape, NEG_BIG, F32)
        self.l[...] = jnp.zeros(self.l.shape, F32)
        self.acc[...] = jnp.zeros(self.acc.shape, F32)
        self.p[1][...] = jnp.zeros(self.p[1].shape, BF16)
        self.a[1][...] = jnp.ones(self.a[1].shape, F32)

    def scores(self, k, slot):
        self.s[slot][...] = lax.dot_general(self.qm[...], k, _NT, preferred_element_type=F32)

    def softmax(self, s, slot):
        m_prev = self.m[...]
        m_new = jnp.maximum(m_prev, jnp.max(s, axis=-1, keepdims=True))
        alpha = jnp.exp(m_prev - m_new)
        p = jnp.exp(s - jnp.concatenate([m_new] * (s.shape[1] // LANES), axis=1))
        self.l[...] = alpha * self.l[...] + jnp.sum(p, axis=-1, keepdims=True)
        self.m[...] = m_new
        self.p[slot][...] = p.astype(BF16)
        self.a[slot][...] = alpha

    def pv(self, v, slot):
        self.acc[...] = self.a[slot][...] * self.acc[...] + jnp.dot(self.p[slot][...], v, preferred_element_type=F32)

    def finish(self, tq):
        lane = lax.broadcasted_iota(I32, (tq, LANES), 1)
        oa = self.acc[0:tq] / self.l[0:tq]
        ob = self.acc[tq:2 * tq] / self.l[tq:2 * tq]
        return jnp.where(lane < 64, oa, ob)


def _stream_scratch(tq, width):
    g = N_STREAMS
    return [pltpu.VMEM((g, 2 * tq, width), BF16),
            pltpu.VMEM((2, g, 2 * tq, TK), F32), pltpu.VMEM((2, g, 2 * tq, TK), BF16),
            pltpu.VMEM((2, g, 2 * tq, LANES), F32),
            pltpu.VMEM((g, 2 * tq, LANES), F32), pltpu.VMEM((g, 2 * tq, LANES), F32),
            pltpu.VMEM((g, 2 * tq, LANES), F32)]


def _pair_body(streams, c, prep, last):
    for st, _, v_at in streams:
        st.pv(v_at(c - 1), 1)
    for st, _, _ in streams:
        st.softmax(prep(st.s[0], c), 0)
    for st, k_at, _ in streams:
        st.scores(k_at(c + 1), 1)
    for st, _, v_at in streams:
        st.pv(v_at(c), 0)
    for st, _, _ in streams:
        st.softmax(prep(st.s[1], c + 1), 1)
    for st, k_at, v_at in streams:
        if last:
            st.pv(v_at(c + 1), 1)
        else:
            st.scores(k_at(c + 2), 0)


def _mla_kernel(q_ref, k_ref, v_ref, o_ref, qm, s, p, a, m, l, acc, *, tq, n_q_blocks):
    i = pl.program_id(2)

    @pl.when(i >= n_q_blocks)
    def _pad_rows():
        o_ref[...] = jnp.zeros(o_ref.shape, o_ref.dtype)

    @pl.when(i < n_q_blocks)
    def _attend():
        lane = lax.broadcasted_iota(I32, (tq, 256), 1)
        in_a = (lane < 64) | ((lane >= 128) & (lane < 144)) | ((lane >= 192) & (lane < 208))
        in_b = ((lane >= 64) & (lane < 128)) | ((lane >= 144) & (lane < 160)) | ((lane >= 208) & (lane < 224))
        n_chunks, n_pairs = _chunk_counts(i, tq)

        def rows(c):
            cc = jnp.clip(c, 0, n_chunks - 1)
            return pl.ds(pl.multiple_of(cc * TK, TK), TK)

        streams = []
        for g in range(N_STREAMS):
            st = _Stream(g, qm, s, p, a, m, l, acc)
            q = q_ref[0, g]
            zero = jnp.zeros_like(q)
            st.qm[0:tq] = jnp.where(in_a, q, zero)
            st.qm[tq:2 * tq] = jnp.where(in_b, q, zero)
            st.init()
            streams.append((st, lambda c, g=g: k_ref[0, g, rows(c), :], lambda c, g=g: v_ref[0, g, rows(c), :]))

        def masked(s_ref, c):
            qry = (lax.broadcasted_iota(I32, (2 * tq, TK), 0) & (tq - 1)) + i * tq
            key = lax.broadcasted_iota(I32, (2 * tq, TK), 1) + c * TK
            return jnp.where(key <= qry, s_ref[...], NEG_BIG)

        for st, k_at, _ in streams:
            st.scores(k_at(0), 0)

        def pair(j, carry):
            _pair_body(streams, 2 * j, lambda s_ref, c: s_ref[...], last=False)
            return carry

        lax.fori_loop(0, n_pairs - 1, pair, 0)
        _pair_body(streams, 2 * (n_pairs - 1), masked, last=True)
        for g, (st, _, _) in enumerate(streams):
            o_ref[0, :, g * LANES:(g + 1) * LANES] = st.finish(tq).astype(o_ref.dtype)


def _mla_call(qmla, kmla, vmla, *, tq, n_q_blocks, interpret=False):
    b, _, tp, _ = qmla.shape
    g = N_STREAMS
    return pl.pallas_call(
        functools.partial(_mla_kernel, tq=tq, n_q_blocks=n_q_blocks),
        grid=(b, N_PAIRS // g, tp // tq),
        in_specs=[pl.BlockSpec((1, g, tq, 256), lambda bi, pg, i: (bi, pg, i, 0)),
                  pl.BlockSpec((1, g, tp, 256), lambda bi, pg, i: (bi, pg, 0, 0)),
                  pl.BlockSpec((1, g, tp, 128), lambda bi, pg, i: (bi, pg, 0, 0))],
        out_specs=pl.BlockSpec((1, tq, g * LANES), lambda bi, pg, i: (bi, i, pg)),
        out_shape=jax.ShapeDtypeStruct((b, tp, N_PAIRS * LANES), BF16),
        scratch_shapes=_stream_scratch(tq, 256),
        compiler_params=pltpu.CompilerParams(
            dimension_semantics=("arbitrary", "arbitrary", "arbitrary"), vmem_limit_bytes=VMEM_LIMIT),
        interpret=interpret,
        name="mla_attn",
    )(qmla, kmla, vmla)


def _dsa_kernel(qi_ref, wi_ref, qs_ref, ki_ref, ks_ref, vs_ref, o_ref,
                key_scr, fld_scr, qim_scr, wt_scr,
                qm, s, p, a, m, l, acc, *, tq, topk, n_q_blocks):
    i = pl.program_id(1)
    pg = pl.program_id(2)
    n_chunks, n_pairs = _chunk_counts(i, tq)
    n_tiles = n_chunks * (TK // HK)
    slabs = HK // SUBLANES
    lane = lax.broadcasted_iota(I32, (tq, LANES), 1)
    in_a = (lane < 32) | ((lane >= 64) & (lane < 96))

    @pl.when(i >= n_q_blocks)
    def _pad_rows():
        o_ref[...] = jnp.zeros(o_ref.shape, o_ref.dtype)

    @pl.when((i < n_q_blocks) & (pg == 0))
    def _select():
        for pp in range(N_PAIRS):
            qp = qi_ref[0, pp].astype(F32)
            zero = jnp.zeros_like(qp)
            qim_scr[:, (2 * pp) * tq:(2 * pp + 1) * tq] = jnp.where(in_a, qp, zero).T.astype(BF16)
            qim_scr[:, (2 * pp + 1) * tq:(2 * pp + 2) * tq] = jnp.where(in_a, zero, qp).T.astype(BF16)
        wt_scr[...] = wi_ref[0].T

        def logits(t):
            kc = ki_ref[0, pl.ds(pl.multiple_of(t * HK, HK), HK), :]
            return jnp.dot(kc, qim_scr[...], preferred_element_type=F32)

        def to_keys(lg, t):
            sc = jnp.zeros((HK, tq), F32)
            for hh in range(IDX_HEADS):
                sc = sc + wt_scr[hh:hh + 1, :] * jnp.maximum(lg[:, hh * tq:(hh + 1) * tq], 0.0)
            bits = pltpu.bitcast(sc, I32)
            key = bits ^ ((bits >> 31) & 0x7FFFFFFF)
            key = key - (key >> 31)
            kpos = lax.broadcasted_iota(I32, (HK, tq), 0) + t * HK
            qpos = lax.broadcasted_iota(I32, (HK, tq), 1) + i * tq
            key = jnp.where(kpos <= qpos, key, INT_MIN)
            key_scr[t] = key
            return key

        def pack(fa, fb):
            return (fa << 16) | fb | FIELD_GUARDS

        def score_chunk(c, carry):
            lg0 = logits(2 * c)
            lg1 = logits(2 * c + 1)
            k0 = to_keys(lg0, 2 * c)
            k1 = to_keys(lg1, 2 * c + 1)
            top = lambda key: _srl(key ^ INT_MIN, 17)
            fld_scr[c] = pack(top(k0), top(k1))
            return carry

        lax.fori_loop(0, n_chunks, score_chunk, 0)

        def count_fields(cand):
            both = ((cand << 16) | cand)[None]

            def body(c, acc):
                w = fld_scr[c].reshape(slabs, SUBLANES, tq)
                hit = _srl(w - both, 15) & 0x00010001
                return acc + jnp.sum(hit, axis=0, dtype=I32)
            acc = lax.fori_loop(0, n_chunks, body, jnp.zeros((SUBLANES, tq), I32))
            acc = (acc & 0xFFFF) + _srl(acc, 16)
            return jnp.broadcast_to(jnp.sum(acc, axis=0, keepdims=True, dtype=I32), (SUBLANES, tq))

        def field_search(above, n_ge):
            def step(it, carry):
                t, n = carry
                cand = t | lax.shift_left(jnp.int32(1), lax.convert_element_type(14 - it, I32))
                cnt = count_fields(cand) + above
                return jnp.where(cnt >= topk, cand, t), jnp.where(cnt >= topk, cnt, n)
            return lax.fori_loop(0, 15, step, (jnp.zeros((SUBLANES, tq), I32), n_ge))

        zeros = jnp.zeros((SUBLANES, tq), I32)
        t_top, n_ge = field_search(zeros, jnp.full((SUBLANES, tq), 2 ** 30, I32))
        above = count_fields(t_top + 1)

        def mid_fields(c, carry):
            def mid(key):
                u = (key ^ INT_MIN).reshape(slabs, SUBLANES, tq)
                alive = _srl(u, 17) == t_top[None]
                return jnp.where(alive, _srl(u, 2) & 0x7FFF, 0).reshape(HK, tq)
            fld_scr[c] = pack(mid(key_scr[2 * c]), mid(key_scr[2 * c + 1]))
            return carry

        lax.fori_loop(0, n_chunks, mid_fields, 0)
        t_mid, n_ge = field_search(above, n_ge)

        def count(pred):
            def body(c, acc):
                for u in range(TK // HK):
                    t = c * (TK // HK) + u
                    kk = key_scr[t].reshape(slabs, SUBLANES, tq)
                    acc = acc + jnp.sum(pred(kk, t).astype(I32), axis=0, dtype=I32)
                return acc
            acc = lax.fori_loop(0, n_chunks, body, jnp.zeros((SUBLANES, tq), I32))
            return jnp.broadcast_to(jnp.sum(acc, axis=0, keepdims=True, dtype=I32), (SUBLANES, tq))

        def bit_step(it, carry):
            t, n = carry
            cand = t | lax.shift_left(jnp.int32(1), lax.convert_element_type(1 - it, I32))
            cnt = count(lambda kk, _: kk >= (cand ^ INT_MIN)[None])
            return jnp.where(cnt >= topk, cand, t), jnp.where(cnt >= topk, cnt, n)

        thr_u, n_ge = lax.fori_loop(0, 2, bit_step, ((t_top << 17) | (t_mid << 2), n_ge))
        thr = thr_u ^ INT_MIN
        excess = jnp.max(jnp.where((n_ge > topk) & (thr > INT_MIN), 1, 0))

        def store_bias(t, sel):
            bias = jnp.where(sel, 0.0, NEG_BIG).astype(F32).reshape(HK, tq)
            key_scr[t] = pltpu.bitcast(bias.T, I32)

        @pl.when(excess == 0)
        def _no_ties():
            floor = jnp.maximum(thr, INT_MIN + 1)[None]

            def to_bias(t, carry):
                store_bias(t, key_scr[t].reshape(slabs, SUBLANES, tq) >= floor)
                return carry

            lax.fori_loop(0, n_tiles, to_bias, 0)

        @pl.when(excess > 0)
        def _ties():
            kidx = (lax.broadcasted_iota(I32, (slabs, SUBLANES, tq), 0) * SUBLANES
                    + lax.broadcasted_iota(I32, (slabs, SUBLANES, tq), 1))
            need = topk - count(lambda kk, _: kk > thr[None])

            def idx_step(it, x):
                cand = x + lax.shift_left(jnp.int32(1), lax.convert_element_type(13 - it, I32))
                cnt = count(lambda kk, t: (kk == thr[None]) & ((kidx + t * HK) < cand[None]))
                return jnp.where(cnt < need, cand, x)

            cut = lax.fori_loop(0, 14, idx_step, jnp.zeros((SUBLANES, tq), I32))
            cut = jnp.where(n_ge > topk, cut, 2 ** 30)[None]
            th = thr[None]

            def to_bias(t, carry):
                kk = key_scr[t].reshape(slabs, SUBLANES, tq)
                store_bias(t, (kk > th) | ((kk == th) & ((kidx + t * HK) <= cut) & (th > INT_MIN)))
                return carry

            lax.fori_loop(0, n_tiles, to_bias, 0)

        def fill(t, carry):
            key_scr[t] = pltpu.bitcast(jnp.full((HK, tq), NEG_BIG, F32), I32)
            return carry

        lax.fori_loop(n_tiles, 2 * n_pairs * (TK // HK), fill, 0)

    @pl.when(i < n_q_blocks)
    def _attend():
        def rows(c):
            cc = jnp.clip(c, 0, n_chunks - 1)
            return pl.ds(pl.multiple_of(cc * TK, TK), TK)

        streams = []
        for g in range(N_STREAMS):
            st = _Stream(g, qm, s, p, a, m, l, acc)
            pr = pg * N_STREAMS + g
            q = qs_ref[0, pr]
            zero = jnp.zeros_like(q)
            st.qm[0:tq] = jnp.where(in_a, q, zero)
            st.qm[tq:2 * tq] = jnp.where(in_a, zero, q)
            st.init()
            streams.append((st, lambda c, pr=pr: ks_ref[0, pr, rows(c), :], lambda c, pr=pr: vs_ref[0, pr, rows(c), :]))

        def biased(s_ref, c):
            bias = jnp.concatenate(
                [pltpu.bitcast(key_scr[c * (TK // HK) + t], F32) for t in range(TK // HK)], axis=1)
            return s_ref[...] + jnp.concatenate([bias, bias], axis=0)

        for st, k_at, _ in streams:
            st.scores(k_at(0), 0)

        def pair(j, carry):
            _pair_body(streams, 2 * j, biased, last=False)
            return carry

        lax.fori_loop(0, n_pairs, pair, 0)
        for g, (st, _, v_at) in enumerate(streams):
            st.pv(v_at(2 * n_pairs - 1), 1)
            o_ref[0, :, g * LANES:(g + 1) * LANES] = st.finish(tq).astype(o_ref.dtype)


def _dsa_call(qi, wi, qs, ki, ks, vs, *, tq, topk, n_q_blocks, interpret=False):
    b, _, tp, _ = qs.shape
    assert HK == tq, "mask tiles are transposed in place"
    once = pl.Buffered(1)
    max_pairs = (tp // TK + 1) // 2
    return pl.pallas_call(
        functools.partial(_dsa_kernel, tq=tq, topk=topk, n_q_blocks=n_q_blocks),
        grid=(b, tp // tq, N_PAIRS // N_STREAMS),
        in_specs=[pl.BlockSpec((1, N_PAIRS, tq, LANES), lambda bi, i, p: (bi, 0, i, 0)),
                  pl.BlockSpec((1, tq, LANES), lambda bi, i, p: (bi, i, 0)),
                  pl.BlockSpec((1, N_PAIRS, tq, LANES), lambda bi, i, p: (bi, 0, i, 0)),
                  pl.BlockSpec((1, tp, LANES), lambda bi, i, p: (bi, 0, 0), pipeline_mode=once),
                  pl.BlockSpec((1, N_PAIRS, tp, LANES), lambda bi, i, p: (bi, 0, 0, 0), pipeline_mode=once),
                  pl.BlockSpec((1, N_PAIRS, tp, LANES), lambda bi, i, p: (bi, 0, 0, 0), pipeline_mode=once)],
        out_specs=pl.BlockSpec((1, tq, N_STREAMS * LANES), lambda bi, i, p: (bi, i, p)),
        out_shape=jax.ShapeDtypeStruct((b, tp, N_PAIRS * LANES), BF16),
        scratch_shapes=[pltpu.VMEM((2 * max_pairs * (TK // HK), HK, tq), I32),
                        pltpu.VMEM((tp // TK, HK, tq), I32),
                        pltpu.VMEM((LANES, IDX_HEADS * tq), BF16),
                        pltpu.VMEM((LANES, tq), F32)]
        + _stream_scratch(tq, LANES),
        compiler_params=pltpu.CompilerParams(
            dimension_semantics=("arbitrary", "arbitrary", "arbitrary"), vmem_limit_bytes=VMEM_LIMIT),
        interpret=interpret,
        name="dsa_attn",
    )(qi, wi, qs, ki, ks, vs)


def _ffn_kernel(h_ref, oa_ref, ob_ref, wo_ref, g2_ref, wg_ref, wu_ref, wd_ref, gf_ref, out_ref):
    o = jnp.concatenate([oa_ref[0], ob_ref[0]], axis=1)
    h1 = h_ref[0] + jnp.dot(o, wo_ref[...], preferred_element_type=F32)
    u = _rms(h1, g2_ref[...]).astype(BF16)
    gate = jnp.dot(u, wg_ref[...], preferred_element_type=F32)
    up = jnp.dot(u, wu_ref[...], preferred_element_type=F32)
    act = (gate * jax.nn.sigmoid(gate) * up).astype(BF16)
    h2 = h1 + jnp.dot(act, wd_ref[...], preferred_element_type=F32)
    out_ref[0] = _rms(h2, gf_ref[...])


def _ffn_call(h, oa, ob, wo, g2, wg, wu, wd, gf, *, interpret=False):
    b, tp, d = h.shape
    tm = TM
    once = pl.Buffered(1)
    const = lambda shape: pl.BlockSpec(shape, lambda bi, i: (0,) * len(shape), pipeline_mode=once)
    row = lambda w: pl.BlockSpec((1, tm, w), lambda bi, i: (bi, i, 0))
    return pl.pallas_call(
        _ffn_kernel,
        grid=(b, tp // tm),
        in_specs=[row(d), row(oa.shape[-1]), row(ob.shape[-1]), const(wo.shape), const((1, d)),
                  const(wg.shape), const(wu.shape), const(wd.shape), const((1, d))],
        out_specs=row(d),
        out_shape=jax.ShapeDtypeStruct((b, tp, d), F32),
        compiler_params=pltpu.CompilerParams(
            dimension_semantics=("arbitrary", "arbitrary"), vmem_limit_bytes=VMEM_LIMIT),
        interpret=interpret,
        name="ffn",
    )(h, oa, ob, wo, g2, wg, wu, wd, gf)


def _layer(h, t_real, attn_norm_g, w_in, mla_q_norm_g, w_uq, mla_kv_norm_g, w_ukv, w_o, ffn_norm_g,
           w_gate, w_up, w_down, out_g, topk, *, interpret=False):
    tp = h.shape[1]
    n_q_blocks = -(-t_real // TQ)
    win = _take_cols(w_in, _w_in_cols()).astype(BF16)
    wuq = _take_cols(w_uq, _w_uq_cols()).astype(BF16)
    wukv = jnp.take(w_ukv, jnp.asarray(_w_ukv_cols()), axis=1).astype(BF16)
    tables = _rope_tables(tp)
    qmla, kmla, vmla, qs, ks, vs, qi, ki, wi = _proj_call(
        h, attn_norm_g[None], win, mla_q_norm_g[None], wuq, mla_kv_norm_g[None], wukv, tables,
        interpret=interpret)
    o_mla = _mla_call(qmla, kmla, vmla, tq=TQ, n_q_blocks=n_q_blocks, interpret=interpret)
    o_dsa = _dsa_call(qi, wi, qs, ki, ks, vs, tq=TQ, topk=topk, n_q_blocks=n_q_blocks, interpret=interpret)
    return _ffn_call(h, o_mla, o_dsa, w_o.astype(BF16), ffn_norm_g[None], w_gate.astype(BF16),
                     w_up.astype(BF16), w_down.astype(BF16), out_g[None], interpret=interpret)


def kernel(x, meta_tokens, attn_norm_g, w_in, mla_q_norm_g, w_uq, mla_kv_norm_g, w_ukv, w_o,
           ffn_norm_g, w_gate, w_up, w_down, final_norm_g):
    b, seq, d = x.shape
    depth = w_in.shape[0]
    assert depth == 1, "the final norm is fused into the layer's last kernel"
    topk = min(TOPK_MAX, seq // 4)
    meta = jnp.broadcast_to(meta_tokens[None].astype(x.dtype), (b, N_META, d))
    t = N_META + seq
    tp = -(-t // TK) * TK
    h = jnp.concatenate([meta, x, jnp.zeros((b, tp - t, d), x.dtype)], axis=1)
    out = _layer(h, t, attn_norm_g[0], w_in[0], mla_q_norm_g[0], w_uq[0], mla_kv_norm_g[0], w_ukv[0],
                 w_o[0], ffn_norm_g[0], w_gate[0], w_up[0], w_down[0], final_norm_g, topk)
    return out[:, N_META:t]
```

```python
import functools

import numpy as np
import jax
import jax.numpy as jnp
from jax import lax
from jax.experimental import pallas as pl
from jax.experimental.pallas import tpu as pltpu

F32 = jnp.float32
BF16 = jnp.bfloat16
I32 = jnp.int32

D_MODEL = 1024
N_META = 16
ROPE_THETA = 10000.0
EPS = 1e-6
MLA_HEADS = 8
MLA_Q_RANK = 256
MLA_KV_RANK = 128
MLA_NOPE_DIM = 64
MLA_ROPE_DIM = 32
MLA_QK_DIM = MLA_NOPE_DIM + MLA_ROPE_DIM
MLA_V_DIM = 64
DSA_HEADS = 8
DSA_HEAD_DIM = 64
IDX_HEADS = 8
IDX_DIM = 64
TOPK_MAX = 256
D_FF = 2816

LANES = 128
SUBLANES = 8
N_PAIRS = 4
NEG_BIG = -1e30
INT_MIN = -2 ** 31
FIELD_GUARDS = 0x8000 - 2 ** 31
VMEM_LIMIT = 58 * 1024 * 1024
TQ = 256
TK = 512
TM = TK
HK = TK // 2

_OFF_CQ, _OFF_CKV, _OFF_KR, _OFF_QS, _OFF_KS, _OFF_VS, _OFF_QI, _OFF_KI, _OFF_WI = (
    0, 256, 384, 416, 928, 1440, 1952, 2464, 2528)
_C_IN = 2536
_P_CQ, _P_CKV, _P_QS, _P_KS, _P_VS, _P_QI, _P_KI, _P_KR, _P_WI, _P_END = (
    0, 256, 384, 896, 1408, 1920, 2432, 2560, 2688, 2816)


def _pair_rope_cols(base, hd=64):
    half = hd // 2
    cols = []
    for p in range(N_PAIRS):
        a, b = 2 * p, 2 * p + 1
        cols += [base + a * hd + d for d in range(half)]
        cols += [base + b * hd + d for d in range(half)]
        cols += [base + a * hd + half + d for d in range(half)]
        cols += [base + b * hd + half + d for d in range(half)]
    return cols


def _w_in_cols():
    z = _C_IN
    cols = list(range(_OFF_CQ, _OFF_CQ + 256)) + list(range(_OFF_CKV, _OFF_CKV + 128))
    cols += _pair_rope_cols(_OFF_QS) + _pair_rope_cols(_OFF_KS)
    cols += list(range(_OFF_VS, _OFF_VS + 512))
    cols += _pair_rope_cols(_OFF_QI)
    ki1 = [_OFF_KI + d for d in range(32)]
    ki2 = [_OFF_KI + 32 + d for d in range(32)]
    cols += ki1 + ki1 + ki2 + ki2
    kr1 = [_OFF_KR + d for d in range(16)]
    kr2 = [_OFF_KR + 16 + d for d in range(16)]
    cols += kr1 + kr1 + [z] * 32 + kr2 + kr2 + [z] * 32
    cols += [_OFF_WI + d for d in range(IDX_HEADS)] + [z] * (LANES - IDX_HEADS)
    assert len(cols) == _P_END
    return np.asarray(cols, np.int32)


def _w_uq_cols():
    z = MLA_HEADS * MLA_QK_DIM
    cols = []
    for p in range(N_PAIRS):
        a, b = 2 * p, 2 * p + 1
        cols += [a * MLA_QK_DIM + d for d in range(64)] + [b * MLA_QK_DIM + d for d in range(64)]
        cols += [a * MLA_QK_DIM + 64 + d for d in range(16)] + [b * MLA_QK_DIM + 64 + d for d in range(16)]
        cols += [z] * 32
        cols += [a * MLA_QK_DIM + 80 + d for d in range(16)] + [b * MLA_QK_DIM + 80 + d for d in range(16)]
        cols += [z] * 32
    return np.asarray(cols, np.int32)


def _w_ukv_cols():
    cols = [h * 128 + d for h in range(MLA_HEADS) for d in range(64)]
    cols += [h * 128 + 64 + d for h in range(MLA_HEADS) for d in range(64)]
    return np.asarray(cols, np.int32)


def _take_cols(w, cols):
    wz = jnp.concatenate([w, jnp.zeros((w.shape[0], 1), w.dtype)], axis=1)
    return jnp.take(wz, jnp.asarray(cols), axis=1)


def _rope_tables(tp):
    pos = jnp.arange(tp, dtype=jnp.int32).astype(F32)

    def cs(half):
        inv = jnp.power(ROPE_THETA, -jnp.arange(half, dtype=F32) / half)
        ang = pos[:, None] * inv[None, :]
        return jnp.cos(ang), jnp.sin(ang)

    c32, s32 = cs(32)
    c16, s16 = cs(16)
    z = jnp.zeros((tp, 32), F32)
    c64t = jnp.concatenate([c32, c32, c32, c32], axis=1)
    s64t = jnp.concatenate([-s32, -s32, s32, s32], axis=1)
    c32t = jnp.concatenate([c16, c16, z, c16, c16, z], axis=1)
    s32t = jnp.concatenate([-s16, -s16, z, s16, s16, z], axis=1)
    return c64t, s64t, c32t, s32t


def _rms(x, g):
    return x * lax.rsqrt(jnp.mean(x * x, axis=-1, keepdims=True) + EPS) * g


def _rope_slab(x, c, s):
    return x * c + pltpu.roll(x, 64, axis=1) * s


def _proj_kernel(h_ref, g_ref, win_ref, gq_ref, wuq_ref, gkv_ref, wukv_ref,
                 c64_ref, s64_ref, c32_ref, s32_ref,
                 qmla_ref, kmla_ref, vmla_ref, qs_ref, ks_ref, vs_ref, qi_ref, ki_ref, wi_ref):
    u = _rms(h_ref[0], g_ref[...]).astype(BF16)
    proj = jnp.dot(u, win_ref[...], preferred_element_type=F32)
    c64, s64, c32, s32 = c64_ref[...], s64_ref[...], c32_ref[...], s32_ref[...]

    cq = _rms(proj[:, _P_CQ:_P_CQ + 256], gq_ref[...]).astype(BF16)
    q = jnp.dot(cq, wuq_ref[...], preferred_element_type=F32)
    ckv = _rms(proj[:, _P_CKV:_P_CKV + 128], gkv_ref[...]).astype(BF16)
    kv = jnp.dot(ckv, wukv_ref[...], preferred_element_type=F32)
    kr = _rope_slab(proj[:, _P_KR:_P_KR + 128], c32, s32).astype(BF16)
    q_scale = MLA_QK_DIM ** -0.5
    s_scale = DSA_HEAD_DIM ** -0.5
    i_scale = IDX_DIM ** -0.5
    for p in range(N_PAIRS):
        qn = q[:, p * 256:p * 256 + 128] * q_scale
        qr = _rope_slab(q[:, p * 256 + 128:p * 256 + 256], c32, s32) * q_scale
        qmla_ref[0, p, :, 0:128] = qn.astype(BF16)
        qmla_ref[0, p, :, 128:256] = qr.astype(BF16)
        kmla_ref[0, p, :, 0:128] = kv[:, p * 128:(p + 1) * 128].astype(BF16)
        kmla_ref[0, p, :, 128:256] = kr
        vmla_ref[0, p] = kv[:, 512 + p * 128:512 + (p + 1) * 128].astype(BF16)
        sl = slice(p * 128, (p + 1) * 128)
        qs_ref[0, p] = (_rope_slab(proj[:, _P_QS:_P_KS][:, sl], c64, s64) * s_scale).astype(BF16)
        ks_ref[0, p] = _rope_slab(proj[:, _P_KS:_P_VS][:, sl], c64, s64).astype(BF16)
        vs_ref[0, p] = proj[:, _P_VS:_P_QI][:, sl].astype(BF16)
        qi_ref[0, p] = (_rope_slab(proj[:, _P_QI:_P_KI][:, sl], c64, s64) * i_scale).astype(BF16)
    ki_ref[0] = _rope_slab(proj[:, _P_KI:_P_KR], c64, s64).astype(BF16)
    wi_ref[0] = proj[:, _P_WI:_P_END] * (IDX_HEADS ** -0.5)


def _proj_call(h, g, win, gq, wuq, gkv, wukv, tables, *, interpret=False):
    b, tp, d = h.shape
    tm = TM
    nt = tp // tm
    const = lambda shape: pl.BlockSpec(shape, lambda bi, i: (0,) * len(shape))
    tab = pl.BlockSpec((tm, LANES), lambda bi, i: (i, 0))
    pair = lambda w: pl.BlockSpec((1, N_PAIRS, tm, w), lambda bi, i: (bi, 0, i, 0))
    row = lambda w: pl.BlockSpec((1, tm, w), lambda bi, i: (bi, i, 0))
    sds = jax.ShapeDtypeStruct
    out_shape = (
        sds((b, N_PAIRS, tp, 256), BF16), sds((b, N_PAIRS, tp, 256), BF16), sds((b, N_PAIRS, tp, 128), BF16),
        sds((b, N_PAIRS, tp, 128), BF16), sds((b, N_PAIRS, tp, 128), BF16), sds((b, N_PAIRS, tp, 128), BF16),
        sds((b, N_PAIRS, tp, 128), BF16), sds((b, tp, 128), BF16), sds((b, tp, 128), F32))
    return pl.pallas_call(
        _proj_kernel,
        grid=(b, nt),
        in_specs=[row(d), const((1, d)), const(win.shape), const((1, MLA_Q_RANK)), const(wuq.shape),
                  const((1, MLA_KV_RANK)), const(wukv.shape), tab, tab, tab, tab],
        out_specs=(pair(256), pair(256), pair(128), pair(128), pair(128), pair(128), pair(128),
                   row(128), row(128)),
        out_shape=out_shape,
        compiler_params=pltpu.CompilerParams(
            dimension_semantics=("arbitrary", "arbitrary"), vmem_limit_bytes=VMEM_LIMIT),
        interpret=interpret,
        name="proj",
    )(h, g, win, gq, wuq, gkv, wukv, *tables)


_NT = (((1,), (1,)), ((), ()))
N_STREAMS = 2


def _srl(x, n):
    return lax.shift_right_logical(x, jnp.full(x.shape, n, x.dtype))


def _chunk_counts(i, tq):
    n_chunks = ((i + 1) * tq + TK - 1) // TK
    return n_chunks, (n_chunks + 1) // 2


class _Stream:
    def __init__(self, g, qm, s, p, a, m, l, acc):
        self.qm = qm.at[g]
        self.s = (s.at[0, g], s.at[1, g])
        self.p = (p.at[0, g], p.at[1, g])
        self.a = (a.at[0, g], a.at[1, g])
        self.m, self.l, self.acc = m.at[g], l.at[g], acc.at[g]

    def init(self):
        self.m[...] = jnp.full(self.m.shape, NEG_BIG, F32)
        self.l[...] = jnp.zeros(self.l.shape, F32)
        self.acc[...] = jnp.zeros(self.acc.shape, F32)
        self.p[1][...] = jnp.zeros(self.p[1].shape, BF16)
        self.a[1][...] = jnp.ones(self.a[1].shape, F32)

    def scores(self, k, slot):
        self.s[slot][...] = lax.dot_general(self.qm[...], k, _NT, preferred_element_type=F32)

    def softmax(self, s, slot):
        m_prev = self.m[...]
        m_new = jnp.maximum(m_prev, jnp.max(s, axis=-1, keepdims=True))
        alpha = jnp.exp(m_prev - m_new)
        p = jnp.exp(s - jnp.concatenate([m_new] * (s.shape[1] // LANES), axis=1))
        self.l[...] = alpha * self.l[...] + jnp.sum(p, axis=-1, keepdims=True)
        self.m[...] = m_new
        self.p[slot][...] = p.astype(BF16)
        self.a[slot][...] = alpha

    def pv(self, v, slot):
        self.acc[...] = self.a[slot][...] * self.acc[...] + jnp.dot(self.p[slot][...], v, preferred_element_type=F32)

    def finish(self, tq):
        lane = lax.broadcasted_iota(I32, (tq, LANES), 1)
        oa = self.acc[0:tq] / self.l[0:tq]
        ob = self.acc[tq:2 * tq] / self.l[tq:2 * tq]
        return jnp.where(lane < 64, oa, ob)


def _stream_scratch(tq, width):
    g = N_STREAMS
    return [pltpu.VMEM((g, 2 * tq, width), BF16),
            pltpu.VMEM((2, g, 2 * tq, TK), F32), pltpu.VMEM((2, g, 2 * tq, TK), BF16),
            pltpu.VMEM((2, g, 2 * tq, LANES), F32),
            pltpu.VMEM((g, 2 * tq, LANES), F32), pltpu.VMEM((g, 2 * tq, LANES), F32),
            pltpu.VMEM((g, 2 * tq, LANES), F32)]


def _pair_body(streams, c, prep, last):
    for st, _, v_at in streams:
        st.pv(v_at(c - 1), 1)
    for st, _, _ in streams:
        st.softmax(prep(st.s[0], c), 0)
    for st, k_at, _ in streams:
        st.scores(k_at(c + 1), 1)
    for st, _, v_at in streams:
        st.pv(v_at(c), 0)
    for st, _, _ in streams:
        st.softmax(prep(st.s[1], c + 1), 1)
    for st, k_at, v_at in streams:
        if last:
            st.pv(v_at(c + 1), 1)
        else:
            st.scores(k_at(c + 2), 0)


def _mla_kernel(q_ref, k_ref, v_ref, o_ref, qm, s, p, a, m, l, acc, *, tq, n_q_blocks):
    i = pl.program_id(2)

    @pl.when(i >= n_q_blocks)
    def _pad_rows():
        o_ref[...] = jnp.zeros(o_ref.shape, o_ref.dtype)

    @pl.when(i < n_q_blocks)
    def _attend():
        lane = lax.broadcasted_iota(I32, (tq, 256), 1)
        in_a = (lane < 64) | ((lane >= 128) & (lane < 144)) | ((lane >= 192) & (lane < 208))
        in_b = ((lane >= 64) & (lane < 128)) | ((lane >= 144) & (lane < 160)) | ((lane >= 208) & (lane < 224))
        n_chunks, n_pairs = _chunk_counts(i, tq)

        def rows(c):
            cc = jnp.clip(c, 0, n_chunks - 1)
            return pl.ds(pl.multiple_of(cc * TK, TK), TK)

        streams = []
        for g in range(N_STREAMS):
            st = _Stream(g, qm, s, p, a, m, l, acc)
            q = q_ref[0, g]
            zero = jnp.zeros_like(q)
            st.qm[0:tq] = jnp.where(in_a, q, zero)
            st.qm[tq:2 * tq] = jnp.where(in_b, q, zero)
            st.init()
            streams.append((st, lambda c, g=g: k_ref[0, g, rows(c), :], lambda c, g=g: v_ref[0, g, rows(c), :]))

        def masked(s_ref, c):
            qry = (lax.broadcasted_iota(I32, (2 * tq, TK), 0) & (tq - 1)) + i * tq
            key = lax.broadcasted_iota(I32, (2 * tq, TK), 1) + c * TK
            return jnp.where(key <= qry, s_ref[...], NEG_BIG)

        for st, k_at, _ in streams:
            st.scores(k_at(0), 0)

        def pair(j, carry):
            _pair_body(streams, 2 * j, lambda s_ref, c: s_ref[...], last=False)
            return carry

        lax.fori_loop(0, n_pairs - 1, pair, 0)
        _pair_body(streams, 2 * (n_pairs - 1), masked, last=True)
        for g, (st, _, _) in enumerate(streams):
            o_ref[0, :, g * LANES:(g + 1) * LANES] = st.finish(tq).astype(o_ref.dtype)


def _mla_call(qmla, kmla, vmla, *, tq, n_q_blocks, interpret=False):
    b, _, tp, _ = qmla.shape
    g = N_STREAMS
    return pl.pallas_call(
        functools.partial(_mla_kernel, tq=tq, n_q_blocks=n_q_blocks),
        grid=(b, N_PAIRS // g, tp // tq),
        in_specs=[pl.BlockSpec((1, g, tq, 256), lambda bi, pg, i: (bi, pg, i, 0)),
                  pl.BlockSpec((1, g, tp, 256), lambda bi, pg, i: (bi, pg, 0, 0)),
                  pl.BlockSpec((1, g, tp, 128), lambda bi, pg, i: (bi, pg, 0, 0))],
        out_specs=pl.BlockSpec((1, tq, g * LANES), lambda bi, pg, i: (bi, i, pg)),
        out_shape=jax.ShapeDtypeStruct((b, tp, N_PAIRS * LANES), BF16),
        scratch_shapes=_stream_scratch(tq, 256),
        compiler_params=pltpu.CompilerParams(
            dimension_semantics=("arbitrary", "arbitrary", "arbitrary"), vmem_limit_bytes=VMEM_LIMIT),
        interpret=interpret,
        name="mla_attn",
    )(qmla, kmla, vmla)


def _dsa_kernel(qi_ref, wi_ref, qs_ref, ki_ref, ks_ref, vs_ref, o_ref,
                key_scr, fld_scr, qim_scr, wt_scr,
                qm, s, p, a, m, l, acc, *, tq, topk, n_q_blocks):
    i = pl.program_id(1)
    pg = pl.program_id(2)
    n_chunks, n_pairs = _chunk_counts(i, tq)
    n_tiles = n_chunks * (TK // HK)
    slabs = HK // SUBLANES
    lane = lax.broadcasted_iota(I32, (tq, LANES), 1)
    in_a = (lane < 32) | ((lane >= 64) & (lane < 96))

    @pl.when(i >= n_q_blocks)
    def _pad_rows():
        o_ref[...] = jnp.zeros(o_ref.shape, o_ref.dtype)

    @pl.when((i < n_q_blocks) & (pg == 0))
    def _select():
        for pp in range(N_PAIRS):
            qp = qi_ref[0, pp].astype(F32)
            zero = jnp.zeros_like(qp)
            qim_scr[:, (2 * pp) * tq:(2 * pp + 1) * tq] = jnp.where(in_a, qp, zero).T.astype(BF16)
            qim_scr[:, (2 * pp + 1) * tq:(2 * pp + 2) * tq] = jnp.where(in_a, zero, qp).T.astype(BF16)
        wt_scr[...] = wi_ref[0].T

        def logits(t):
            kc = ki_ref[0, pl.ds(pl.multiple_of(t * HK, HK), HK), :]
            return jnp.dot(kc, qim_scr[...], preferred_element_type=F32)

        def to_keys(lg, t):
            sc = jnp.zeros((HK, tq), F32)
            for hh in range(IDX_HEADS):
                sc = sc + wt_scr[hh:hh + 1, :] * jnp.maximum(lg[:, hh * tq:(hh + 1) * tq], 0.0)
            bits = pltpu.bitcast(sc, I32)
            key = bits ^ ((bits >> 31) & 0x7FFFFFFF)
            key = key - (key >> 31)
            kpos = lax.broadcasted_iota(I32, (HK, tq), 0) + t * HK
            qpos = lax.broadcasted_iota(I32, (HK, tq), 1) + i * tq
            key = jnp.where(kpos <= qpos, key, INT_MIN)
            key_scr[t] = key
            return key

        def pack(fa, fb):
            return (fa << 16) | fb | FIELD_GUARDS

        def score_chunk(c, carry):
            lg0 = logits(2 * c)
            lg1 = logits(2 * c + 1)
            k0 = to_keys(lg0, 2 * c)
            k1 = to_keys(lg1, 2 * c + 1)
            top = lambda key: _srl(key ^ INT_MIN, 17)
            fld_scr[c] = pack(top(k0), top(k1))
            return carry

        lax.fori_loop(0, n_chunks, score_chunk, 0)

        def count_fields(cand):
            both = ((cand << 16) | cand)[None]

            def body(c, acc):
                w = fld_scr[c].reshape(slabs, SUBLANES, tq)
                hit = _srl(w - both, 15) & 0x00010001
                return acc + jnp.sum(hit, axis=0, dtype=I32)
            acc = lax.fori_loop(0, n_chunks, body, jnp.zeros((SUBLANES, tq), I32))
            acc = (acc & 0xFFFF) + _srl(acc, 16)
            return jnp.broadcast_to(jnp.sum(acc, axis=0, keepdims=True, dtype=I32), (SUBLANES, tq))

        def field_search(above, n_ge):
            def step(it, carry):
                t, n = carry
                cand = t | lax.shift_left(jnp.int32(1), lax.convert_element_type(14 - it, I32))
                cnt = count_fields(cand) + above
                return jnp.where(cnt >= topk, cand, t), jnp.where(cnt >= topk, cnt, n)
            return lax.fori_loop(0, 15, step, (jnp.zeros((SUBLANES, tq), I32), n_ge))

        zeros = jnp.zeros((SUBLANES, tq), I32)
        t_top, n_ge = field_search(zeros, jnp.full((SUBLANES, tq), 2 ** 30, I32))
        above = count_fields(t_top + 1)

        def mid_fields(c, carry):
            def mid(key):
                u = (key ^ INT_MIN).reshape(slabs, SUBLANES, tq)
                alive = _srl(u, 17) == t_top[None]
                return jnp.where(alive, _srl(u, 2) & 0x7FFF, 0).reshape(HK, tq)
            fld_scr[c] = pack(mid(key_scr[2 * c]), mid(key_scr[2 * c + 1]))
            return carry

        lax.fori_loop(0, n_chunks, mid_fields, 0)
        t_mid, n_ge = field_search(above, n_ge)

        def count(pred):
            def body(c, acc):
                for u in range(TK // HK):
                    t = c * (TK // HK) + u
                    kk = key_scr[t].reshape(slabs, SUBLANES, tq)
                    acc = acc + jnp.sum(pred(kk, t).astype(I32), axis=0, dtype=I32)
                return acc
            acc = lax.fori_loop(0, n_chunks, body, jnp.zeros((SUBLANES, tq), I32))
            return jnp.broadcast_to(jnp.sum(acc, axis=0, keepdims=True, dtype=I32), (SUBLANES, tq))

        def bit_step(it, carry):
            t, n = carry
            cand = t | lax.shift_left(jnp.int32(1), lax.convert_element_type(1 - it, I32))
            cnt = count(lambda kk, _: kk >= (cand ^ INT_MIN)[None])
            return jnp.where(cnt >= topk, cand, t), jnp.where(cnt >= topk, cnt, n)

        thr_u, n_ge = lax.fori_loop(0, 2, bit_step, ((t_top << 17) | (t_mid << 2), n_ge))
        thr = thr_u ^ INT_MIN
        excess = jnp.max(jnp.where((n_ge > topk) & (thr > INT_MIN), 1, 0))

        def store_bias(t, sel):
            bias = jnp.where(sel, 0.0, NEG_BIG).astype(F32).reshape(HK, tq)
            key_scr[t] = pltpu.bitcast(bias.T, I32)

        @pl.when(excess == 0)
        def _no_ties():
            floor = jnp.maximum(thr, INT_MIN + 1)[None]

            def to_bias(t, carry):
                store_bias(t, key_scr[t].reshape(slabs, SUBLANES, tq) >= floor)
                return carry

            lax.fori_loop(0, n_tiles, to_bias, 0)

        @pl.when(excess > 0)
        def _ties():
            kidx = (lax.broadcasted_iota(I32, (slabs, SUBLANES, tq), 0) * SUBLANES
                    + lax.broadcasted_iota(I32, (slabs, SUBLANES, tq), 1))
            need = topk - count(lambda kk, _: kk > thr[None])

            def idx_step(it, x):
                cand = x + lax.shift_left(jnp.int32(1), lax.convert_element_type(13 - it, I32))
                cnt = count(lambda kk, t: (kk == thr[None]) & ((kidx + t * HK) < cand[None]))
                return jnp.where(cnt < need, cand, x)

            cut = lax.fori_loop(0, 14, idx_step, jnp.zeros((SUBLANES, tq), I32))
            cut = jnp.where(n_ge > topk, cut, 2 ** 30)[None]
            th = thr[None]

            def to_bias(t, carry):
                kk = key_scr[t].reshape(slabs, SUBLANES, tq)
                store_bias(t, (kk > th) | ((kk == th) & ((kidx + t * HK) <= cut) & (th > INT_MIN)))
                return carry

            lax.fori_loop(0, n_tiles, to_bias, 0)

        def fill(t, carry):
            key_scr[t] = pltpu.bitcast(jnp.full((HK, tq), NEG_BIG, F32), I32)
            return carry

        lax.fori_loop(n_tiles, 2 * n_pairs * (TK // HK), fill, 0)

    @pl.when(i < n_q_blocks)
    def _attend():
        def rows(c):
            cc = jnp.clip(c, 0, n_chunks - 1)
            return pl.ds(pl.multiple_of(cc * TK, TK), TK)

        streams = []
        for g in range(N_STREAMS):
            st = _Stream(g, qm, s, p, a, m, l, acc)
            pr = pg * N_STREAMS + g
            q = qs_ref[0, pr]
            zero = jnp.zeros_like(q)
            st.qm[0:tq] = jnp.where(in_a, q, zero)
            st.qm[tq:2 * tq] = jnp.where(in_a, zero, q)
            st.init()
            streams.append((st, lambda c, pr=pr: ks_ref[0, pr, rows(c), :], lambda c, pr=pr: vs_ref[0, pr, rows(c), :]))

        def biased(s_ref, c):
            bias = jnp.concatenate(
                [pltpu.bitcast(key_scr[c * (TK // HK) + t], F32) for t in range(TK // HK)], axis=1)
            return s_ref[...] + jnp.concatenate([bias, bias], axis=0)

        for st, k_at, _ in streams:
            st.scores(k_at(0), 0)

        def pair(j, carry):
            _pair_body(streams, 2 * j, biased, last=False)
            return carry

        lax.fori_loop(0, n_pairs, pair, 0)
        for g, (st, _, v_at) in enumerate(streams):
            st.pv(v_at(2 * n_pairs - 1), 1)
            o_ref[0, :, g * LANES:(g + 1) * LANES] = st.finish(tq).astype(o_ref.dtype)


def _dsa_call(qi, wi, qs, ki, ks, vs, *, tq, topk, n_q_blocks, interpret=False):
    b, _, tp, _ = qs.shape
    assert HK == tq, "mask tiles are transposed in place"
    once = pl.Buffered(1)
    max_pairs = (tp // TK + 1) // 2
    return pl.pallas_call(
        functools.partial(_dsa_kernel, tq=tq, topk=topk, n_q_blocks=n_q_blocks),
        grid=(b, tp // tq, N_PAIRS // N_STREAMS),
        in_specs=[pl.BlockSpec((1, N_PAIRS, tq, LANES), lambda bi, i, p: (bi, 0, i, 0)),
                  pl.BlockSpec((1, tq, LANES), lambda bi, i, p: (bi, i, 0)),
                  pl.BlockSpec((1, N_PAIRS, tq, LANES), lambda bi, i, p: (bi, 0, i, 0)),
                  pl.BlockSpec((1, tp, LANES), lambda bi, i, p: (bi, 0, 0), pipeline_mode=once),
                  pl.BlockSpec((1, N_PAIRS, tp, LANES), lambda bi, i, p: (bi, 0, 0, 0), pipeline_mode=once),
                  pl.BlockSpec((1, N_PAIRS, tp, LANES), lambda bi, i, p: (bi, 0, 0, 0), pipeline_mode=once)],
        out_specs=pl.BlockSpec((1, tq, N_STREAMS * LANES), lambda bi, i, p: (bi, i, p)),
        out_shape=jax.ShapeDtypeStruct((b, tp, N_PAIRS * LANES), BF16),
        scratch_shapes=[pltpu.VMEM((2 * max_pairs * (TK // HK), HK, tq), I32),
                        pltpu.VMEM((tp // TK, HK, tq), I32),
                        pltpu.VMEM((LANES, IDX_HEADS * tq), BF16),
                        pltpu.VMEM((LANES, tq), F32)]
        + _stream_scratch(tq, LANES),
        compiler_params=pltpu.CompilerParams(
            dimension_semantics=("arbitrary", "arbitrary", "arbitrary"), vmem_limit_bytes=VMEM_LIMIT),
        interpret=interpret,
        name="dsa_attn",
    )(qi, wi, qs, ki, ks, vs)


def _ffn_kernel(h_ref, oa_ref, ob_ref, wo_ref, g2_ref, wg_ref, wu_ref, wd_ref, gf_ref, out_ref):
    o = jnp.concatenate([oa_ref[0], ob_ref[0]], axis=1)
    h1 = h_ref[0] + jnp.dot(o, wo_ref[...], preferred_element_type=F32)
    u = _rms(h1, g2_ref[...]).astype(BF16)
    gate = jnp.dot(u, wg_ref[...], preferred_element_type=F32)
    up = jnp.dot(u, wu_ref[...], preferred_element_type=F32)
    act = (gate * jax.nn.sigmoid(gate) * up).astype(BF16)
    h2 = h1 + jnp.dot(act, wd_ref[...], preferred_element_type=F32)
    out_ref[0] = _rms(h2, gf_ref[...])


def _ffn_call(h, oa, ob, wo, g2, wg, wu, wd, gf, *, interpret=False):
    b, tp, d = h.shape
    tm = TM
    once = pl.Buffered(1)
    const = lambda shape: pl.BlockSpec(shape, lambda bi, i: (0,) * len(shape), pipeline_mode=once)
    row = lambda w: pl.BlockSpec((1, tm, w), lambda bi, i: (bi, i, 0))
    return pl.pallas_call(
        _ffn_kernel,
        grid=(b, tp // tm),
        in_specs=[row(d), row(oa.shape[-1]), row(ob.shape[-1]), const(wo.shape), const((1, d)),
                  const(wg.shape), const(wu.shape), const(wd.shape), const((1, d))],
        out_specs=row(d),
        out_shape=jax.ShapeDtypeStruct((b, tp, d), F32),
        compiler_params=pltpu.CompilerParams(
            dimension_semantics=("arbitrary", "arbitrary"), vmem_limit_bytes=VMEM_LIMIT),
        interpret=interpret,
        name="ffn",
    )(h, oa, ob, wo, g2, wg, wu, wd, gf)


def _layer(h, t_real, attn_norm_g, w_in, mla_q_norm_g, w_uq, mla_kv_norm_g, w_ukv, w_o, ffn_norm_g,
           w_gate, w_up, w_down, out_g, topk, *, interpret=False):
    tp = h.shape[1]
    n_q_blocks = -(-t_real // TQ)
    win = _take_cols(w_in, _w_in_cols()).astype(BF16)
    wuq = _take_cols(w_uq, _w_uq_cols()).astype(BF16)
    wukv = jnp.take(w_ukv, jnp.asarray(_w_ukv_cols()), axis=1).astype(BF16)
    tables = _rope_tables(tp)
    qmla, kmla, vmla, qs, ks, vs, qi, ki, wi = _proj_call(
        h, attn_norm_g[None], win, mla_q_norm_g[None], wuq, mla_kv_norm_g[None], wukv, tables,
        interpret=interpret)
    o_mla = _mla_call(qmla, kmla, vmla, tq=TQ, n_q_blocks=n_q_blocks, interpret=interpret)
    o_dsa = _dsa_call(qi, wi, qs, ki, ks, vs, tq=TQ, topk=topk, n_q_blocks=n_q_blocks, interpret=interpret)
    return _ffn_call(h, o_mla, o_dsa, w_o.astype(BF16), ffn_norm_g[None], w_gate.astype(BF16),
                     w_up.astype(BF16), w_down.astype(BF16), out_g[None], interpret=interpret)


def kernel(x, meta_tokens, attn_norm_g, w_in, mla_q_norm_g, w_uq, mla_kv_norm_g, w_ukv, w_o,
           ffn_norm_g, w_gate, w_up, w_down, final_norm_g):
    b, seq, d = x.shape
    depth = w_in.shape[0]
    assert depth == 1, "the final norm is fused into the layer's last kernel"
    topk = min(TOPK_MAX, seq // 4)
    meta = jnp.broadcast_to(meta_tokens[None].astype(x.dtype), (b, N_META, d))
    t = N_META + seq
    tp = -(-t // TK) * TK
    h = jnp.concatenate([meta, x, jnp.zeros((b, tp - t, d), x.dtype)], axis=1)
    out = _layer(h, t, attn_norm_g[0], w_in[0], mla_q_norm_g[0], w_uq[0], mla_kv_norm_g[0], w_ukv[0],
                 w_o[0], ffn_norm_g[0], w_gate[0], w_up[0], w_down[0], final_norm_g, topk)
    return out[:, N_META:t]
```

```python
import functools

import numpy as np
import jax
import jax.numpy as jnp
from jax import lax
from jax.experimental import pallas as pl
from jax.experimental.pallas import tpu as pltpu

F32 = jnp.float32
BF16 = jnp.bfloat16
I32 = jnp.int32

D_MODEL = 1024
N_META = 16
ROPE_THETA = 10000.0
EPS = 1e-6
MLA_HEADS = 8
MLA_Q_RANK = 256
MLA_KV_RANK = 128
MLA_NOPE_DIM = 64
MLA_ROPE_DIM = 32
MLA_QK_DIM = MLA_NOPE_DIM + MLA_ROPE_DIM
MLA_V_DIM = 64
DSA_HEADS = 8
DSA_HEAD_DIM = 64
IDX_HEADS = 8
IDX_DIM = 64
TOPK_MAX = 256
D_FF = 2816

LANES = 128
SUBLANES = 8
N_PAIRS = 4
NEG_BIG = -1e30
INT_MIN = -2 ** 31
FIELD_GUARDS = 0x8000 - 2 ** 31
VMEM_LIMIT = 58 * 1024 * 1024
TQ = 256
TK = 512
TM = TK
HK = TK // 2

_OFF_CQ, _OFF_CKV, _OFF_KR, _OFF_QS, _OFF_KS, _OFF_VS, _OFF_QI, _OFF_KI, _OFF_WI = (
    0, 256, 384, 416, 928, 1440, 1952, 2464, 2528)
_C_IN = 2536
_P_CQ, _P_CKV, _P_QS, _P_KS, _P_VS, _P_QI, _P_KI, _P_KR, _P_WI, _P_END = (
    0, 256, 384, 896, 1408, 1920, 2432, 2560, 2688, 2816)


def _pair_rope_cols(base, hd=64):
    half = hd // 2
    cols = []
    for p in range(N_PAIRS):
        a, b = 2 * p, 2 * p + 1
        cols += [base + a * hd + d for d in range(half)]
        cols += [base + b * hd + d for d in range(half)]
        cols += [base + a * hd + half + d for d in range(half)]
        cols += [base + b * hd + half + d for d in range(half)]
    return cols


def _w_in_cols():
    z = _C_IN
    cols = list(range(_OFF_CQ, _OFF_CQ + 256)) + list(range(_OFF_CKV, _OFF_CKV + 128))
    cols += _pair_rope_cols(_OFF_QS) + _pair_rope_cols(_OFF_KS)
    cols += list(range(_OFF_VS, _OFF_VS + 512))
    cols += _pair_rope_cols(_OFF_QI)
    ki1 = [_OFF_KI + d for d in range(32)]
    ki2 = [_OFF_KI + 32 + d for d in range(32)]
    cols += ki1 + ki1 + ki2 + ki2
    kr1 = [_OFF_KR + d for d in range(16)]
    kr2 = [_OFF_KR + 16 + d for d in range(16)]
    cols += kr1 + kr1 + [z] * 32 + kr2 + kr2 + [z] * 32
    cols += [_OFF_WI + d for d in range(IDX_HEADS)] + [z] * (LANES - IDX_HEADS)
    assert len(cols) == _P_END
    return np.asarray(cols, np.int32)


def _w_uq_cols():
    z = MLA_HEADS * MLA_QK_DIM
    cols = []
    for p in range(N_PAIRS):
        a, b = 2 * p, 2 * p + 1
        cols += [a * MLA_QK_DIM + d for d in range(64)] + [b * MLA_QK_DIM + d for d in range(64)]
        cols += [a * MLA_QK_DIM + 64 + d for d in range(16)] + [b * MLA_QK_DIM + 64 + d for d in range(16)]
        cols += [z] * 32
        cols += [a * MLA_QK_DIM + 80 + d for d in range(16)] + [b * MLA_QK_DIM + 80 + d for d in range(16)]
        cols += [z] * 32
    return np.asarray(cols, np.int32)


def _w_ukv_cols():
    cols = [h * 128 + d for h in range(MLA_HEADS) for d in range(64)]
    cols += [h * 128 + 64 + d for h in range(MLA_HEADS) for d in range(64)]
    return np.asarray(cols, np.int32)


def _take_cols(w, cols):
    wz = jnp.concatenate([w, jnp.zeros((w.shape[0], 1), w.dtype)], axis=1)
    return jnp.take(wz, jnp.asarray(cols), axis=1)


def _rope_tables(tp):
    pos = jnp.arange(tp, dtype=jnp.int32).astype(F32)

    def cs(half):
        inv = jnp.power(ROPE_THETA, -jnp.arange(half, dtype=F32) / half)
        ang = pos[:, None] * inv[None, :]
        return jnp.cos(ang), jnp.sin(ang)

    c32, s32 = cs(32)
    c16, s16 = cs(16)
    z = jnp.zeros((tp, 32), F32)
    c64t = jnp.concatenate([c32, c32, c32, c32], axis=1)
    s64t = jnp.concatenate([-s32, -s32, s32, s32], axis=1)
    c32t = jnp.concatenate([c16, c16, z, c16, c16, z], axis=1)
    s32t = jnp.concatenate([-s16, -s16, z, s16, s16, z], axis=1)
    return c64t, s64t, c32t, s32t


def _rms(x, g):
    return x * lax.rsqrt(jnp.mean(x * x, axis=-1, keepdims=True) + EPS) * g


def _rope_slab(x, c, s):
    return x * c + pltpu.roll(x, 64, axis=1) * s


def _proj_kernel(h_ref, g_ref, win_ref, gq_ref, wuq_ref, gkv_ref, wukv_ref,
                 c64_ref, s64_ref, c32_ref, s32_ref,
                 qmla_ref, kmla_ref, vmla_ref, qs_ref, ks_ref, vs_ref, qi_ref, ki_ref, wi_ref):
    u = _rms(h_ref[0], g_ref[...]).astype(BF16)
    proj = jnp.dot(u, win_ref[...], preferred_element_type=F32)
    c64, s64, c32, s32 = c64_ref[...], s64_ref[...], c32_ref[...], s32_ref[...]

    cq = _rms(proj[:, _P_CQ:_P_CQ + 256], gq_ref[...]).astype(BF16)
    q = jnp.dot(cq, wuq_ref[...], preferred_element_type=F32)
    ckv = _rms(proj[:, _P_CKV:_P_CKV + 128], gkv_ref[...]).astype(BF16)
    kv = jnp.dot(ckv, wukv_ref[...], preferred_element_type=F32)
    kr = _rope_slab(proj[:, _P_KR:_P_KR + 128], c32, s32).astype(BF16)
    q_scale = MLA_QK_DIM ** -0.5
    s_scale = DSA_HEAD_DIM ** -0.5
    i_scale = IDX_DIM ** -0.5
    for p in range(N_PAIRS):
        qn = q[:, p * 256:p * 256 + 128] * q_scale
        qr = _rope_slab(q[:, p * 256 + 128:p * 256 + 256], c32, s32) * q_scale
        qmla_ref[0, p, :, 0:128] = qn.astype(BF16)
        qmla_ref[0, p, :, 128:256] = qr.astype(BF16)
        kmla_ref[0, p, :, 0:128] = kv[:, p * 128:(p + 1) * 128].astype(BF16)
        kmla_ref[0, p, :, 128:256] = kr
        vmla_ref[0, p] = kv[:, 512 + p * 128:512 + (p + 1) * 128].astype(BF16)
        sl = slice(p * 128, (p + 1) * 128)
        qs_ref[0, p] = (_rope_slab(proj[:, _P_QS:_P_KS][:, sl], c64, s64) * s_scale).astype(BF16)
        ks_ref[0, p] = _rope_slab(proj[:, _P_KS:_P_VS][:, sl], c64, s64).astype(BF16)
        vs_ref[0, p] = proj[:, _P_VS:_P_QI][:, sl].astype(BF16)
        qi_ref[0, p] = (_rope_slab(proj[:, _P_QI:_P_KI][:, sl], c64, s64) * i_scale).astype(BF16)
    ki_ref[0] = _rope_slab(proj[:, _P_KI:_P_KR], c64, s64).astype(BF16)
    wi_ref[0] = proj[:, _P_WI:_P_END] * (IDX_HEADS ** -0.5)


def _proj_call(h, g, win, gq, wuq, gkv, wukv, tables, *, interpret=False):
    b, tp, d = h.shape
    tm = TM
    nt = tp // tm
    const = lambda shape: pl.BlockSpec(shape, lambda bi, i: (0,) * len(shape))
    tab = pl.BlockSpec((tm, LANES), lambda bi, i: (i, 0))
    pair = lambda w: pl.BlockSpec((1, N_PAIRS, tm, w), lambda bi, i: (bi, 0, i, 0))
    row = lambda w: pl.BlockSpec((1, tm, w), lambda bi, i: (bi, i, 0))
    sds = jax.ShapeDtypeStruct
    out_shape = (
        sds((b, N_PAIRS, tp, 256), BF16), sds((b, N_PAIRS, tp, 256), BF16), sds((b, N_PAIRS, tp, 128), BF16),
        sds((b, N_PAIRS, tp, 128), BF16), sds((b, N_PAIRS, tp, 128), BF16), sds((b, N_PAIRS, tp, 128), BF16),
        sds((b, N_PAIRS, tp, 128), BF16), sds((b, tp, 128), BF16), sds((b, tp, 128), F32))
    return pl.pallas_call(
        _proj_kernel,
        grid=(b, nt),
        in_specs=[row(d), const((1, d)), const(win.shape), const((1, MLA_Q_RANK)), const(wuq.shape),
                  const((1, MLA_KV_RANK)), const(wukv.shape), tab, tab, tab, tab],
        out_specs=(pair(256), pair(256), pair(128), pair(128), pair(128), pair(128), pair(128),
                   row(128), row(128)),
        out_shape=out_shape,
        compiler_params=pltpu.CompilerParams(
            dimension_semantics=("arbitrary", "arbitrary"), vmem_limit_bytes=VMEM_LIMIT),
        interpret=interpret,
        name="proj",
    )(h, g, win, gq, wuq, gkv, wukv, *tables)


_NT = (((1,), (1,)), ((), ()))
N_STREAMS = 2


def _srl(x, n):
    return lax.shift_right_logical(x, jnp.full(x.shape, n, x.dtype))


def _chunk_counts(i, tq):
    n_chunks = ((i + 1) * tq + TK - 1) // TK
    return n_chunks, (n_chunks + 1) // 2


class _Stream:
    def __init__(self, g, qm, s, p, a, m, l, acc):
        self.qm = qm.at[g]
        self.s = (s.at[0, g], s.at[1, g])
        self.p = (p.at[0, g], p.at[1, g])
        self.a = (a.at[0, g], a.at[1, g])
        self.m, self.l, self.acc = m.at[g], l.at[g], acc.at[g]

    def init(self):
        self.m[...] = jnp.full(self.m.shape, NEG_BIG, F32)
        self.l[...] = jnp.zeros(self.l.shape, F32)
        self.acc[...] = jnp.zeros(self.acc.shape, F32)
        self.p[1][...] = jnp.zeros(self.p[1].shape, BF16)
        self.a[1][...] = jnp.ones(self.a[1].shape, F32)

    def scores(self, k, slot):
        self.s[slot][...] = lax.dot_general(self.qm[...], k, _NT, preferred_element_type=F32)

    def softmax(self, s, slot):
        m_prev = self.m[...]
        m_new = jnp.maximum(m_prev, jnp.max(s, axis=-1, keepdims=True))
        alpha = jnp.exp(m_prev - m_new)
        p = jnp.exp(s - jnp.concatenate([m_new] * (s.shape[1] // LANES), axis=1))
        self.l[...] = alpha * self.l[...] + jnp.sum(p, axis=-1, keepdims=True)
        self.m[...] = m_new
        self.p[slot][...] = p.astype(BF16)
        self.a[slot][...] = alpha

    def pv(self, v, slot):
        self.acc[...] = self.a[slot][...] * self.acc[...] + jnp.dot(self.p[slot][...], v, preferred_element_type=F32)

    def finish(self, tq):
        lane = lax.broadcasted_iota(I32, (tq, LANES), 1)
        oa = self.acc[0:tq] / self.l[0:tq]
        ob = self.acc[tq:2 * tq] / self.l[tq:2 * tq]
        return jnp.where(lane < 64, oa, ob)


def _stream_scratch(tq, width):
    g = N_STREAMS
    return [pltpu.VMEM((g, 2 * tq, width), BF16),
            pltpu.VMEM((2, g, 2 * tq, TK), F32), pltpu.VMEM((2, g, 2 * tq, TK), BF16),
            pltpu.VMEM((2, g, 2 * tq, LANES), F32),
            pltpu.VMEM((g, 2 * tq, LANES), F32), pltpu.VMEM((g, 2 * tq, LANES), F32),
            pltpu.VMEM((g, 2 * tq, LANES), F32)]


def _pair_body(streams, c, prep, last):
    for st, _, v_at in streams:
        st.pv(v_at(c - 1), 1)
    for st, _, _ in streams:
        st.softmax(prep(st.s[0], c), 0)
    for st, k_at, _ in streams:
        st.scores(k_at(c + 1), 1)
    for st, _, v_at in streams:
        st.pv(v_at(c), 0)
    for st, _, _ in streams:
        st.softmax(prep(st.s[1], c + 1), 1)
    for st, k_at, v_at in streams:
        if last:
            st.pv(v_at(c + 1), 1)
        else:
            st.scores(k_at(c + 2), 0)


def _mla_kernel(q_ref, k_ref, v_ref, o_ref, qm, s, p, a, m, l, acc, *, tq, n_q_blocks):
    i = pl.program_id(2)

    @pl.when(i >= n_q_blocks)
    def _pad_rows():
        o_ref[...] = jnp.zeros(o_ref.shape, o_ref.dtype)

    @pl.when(i < n_q_blocks)
    def _attend():
        lane = lax.broadcasted_iota(I32, (tq, 256), 1)
        in_a = (lane < 64) | ((lane >= 128) & (lane < 144)) | ((lane >= 192) & (lane < 208))
        in_b = ((lane >= 64) & (lane < 128)) | ((lane >= 144) & (lane < 160)) | ((lane >= 208) & (lane < 224))
        n_chunks, n_pairs = _chunk_counts(i, tq)

        def rows(c):
            cc = jnp.clip(c, 0, n_chunks - 1)
            return pl.ds(pl.multiple_of(cc * TK, TK), TK)

        streams = []
        for g in range(N_STREAMS):
            st = _Stream(g, qm, s, p, a, m, l, acc)
            q = q_ref[0, g]
            zero = jnp.zeros_like(q)
            st.qm[0:tq] = jnp.where(in_a, q, zero)
            st.qm[tq:2 * tq] = jnp.where(in_b, q, zero)
            st.init()
            streams.append((st, lambda c, g=g: k_ref[0, g, rows(c), :], lambda c, g=g: v_ref[0, g, rows(c), :]))

        def masked(s_ref, c):
            qry = (lax.broadcasted_iota(I32, (2 * tq, TK), 0) & (tq - 1)) + i * tq
            key = lax.broadcasted_iota(I32, (2 * tq, TK), 1) + c * TK
            return jnp.where(key <= qry, s_ref[...], NEG_BIG)

        for st, k_at, _ in streams:
            st.scores(k_at(0), 0)

        def pair(j, carry):
            _pair_body(streams, 2 * j, lambda s_ref, c: s_ref[...], last=False)
            return carry

        lax.fori_loop(0, n_pairs - 1, pair, 0)
        _pair_body(streams, 2 * (n_pairs - 1), masked, last=True)
        for g, (st, _, _) in enumerate(streams):
            o_ref[0, :, g * LANES:(g + 1) * LANES] = st.finish(tq).astype(o_ref.dtype)


def _mla_call(qmla, kmla, vmla, *, tq, n_q_blocks, interpret=False):
    b, _, tp, _ = qmla.shape
    g = N_STREAMS
    return pl.pallas_call(
        functools.partial(_mla_kernel, tq=tq, n_q_blocks=n_q_blocks),
        grid=(b, N_PAIRS // g, tp // tq),
        in_specs=[pl.BlockSpec((1, g, tq, 256), lambda bi, pg, i: (bi, pg, i, 0)),
                  pl.BlockSpec((1, g, tp, 256), lambda bi, pg, i: (bi, pg, 0, 0)),
                  pl.BlockSpec((1, g, tp, 128), lambda bi, pg, i: (bi, pg, 0, 0))],
        out_specs=pl.BlockSpec((1, tq, g * LANES), lambda bi, pg, i: (bi, i, pg)),
        out_shape=jax.ShapeDtypeStruct((b, tp, N_PAIRS * LANES), BF16),
        scratch_shapes=_stream_scratch(tq, 256),
        compiler_params=pltpu.CompilerParams(
            dimension_semantics=("arbitrary", "arbitrary", "arbitrary"), vmem_limit_bytes=VMEM_LIMIT),
        interpret=interpret,
        name="mla_attn",
    )(qmla, kmla, vmla)


def _dsa_kernel(qi_ref, wi_ref, qs_ref, ki_ref, ks_ref, vs_ref, o_ref,
                key_scr, fld_scr, qim_scr, wt_scr,
                qm, s, p, a, m, l, acc, *, tq, topk, n_q_blocks):
    i = pl.program_id(1)
    pg = pl.program_id(2)
    n_chunks, n_pairs = _chunk_counts(i, tq)
    n_tiles = n_chunks * (TK // HK)
    slabs = HK // SUBLANES
    lane = lax.broadcasted_iota(I32, (tq, LANES), 1)
    in_a = (lane < 32) | ((lane >= 64) & (lane < 96))

    @pl.when(i >= n_q_blocks)
    def _pad_rows():
        o_ref[...] = jnp.zeros(o_ref.shape, o_ref.dtype)

    @pl.when((i < n_q_blocks) & (pg == 0))
    def _select():
        for pp in range(N_PAIRS):
            qp = qi_ref[0, pp].astype(F32)
            zero = jnp.zeros_like(qp)
            qim_scr[:, (2 * pp) * tq:(2 * pp + 1) * tq] = jnp.where(in_a, qp, zero).T.astype(BF16)
            qim_scr[:, (2 * pp + 1) * tq:(2 * pp + 2) * tq] = jnp.where(in_a, zero, qp).T.astype(BF16)
        wt_scr[...] = wi_ref[0].T

        def logits(t):
            kc = ki_ref[0, pl.ds(pl.multiple_of(t * HK, HK), HK), :]
            return jnp.dot(kc, qim_scr[...], preferred_element_type=F32)

        def to_keys(lg, t):
            sc = jnp.zeros((HK, tq), F32)
            for hh in range(IDX_HEADS):
                sc = sc + wt_scr[hh:hh + 1, :] * jnp.maximum(lg[:, hh * tq:(hh + 1) * tq], 0.0)
            bits = pltpu.bitcast(sc, I32)
            key = bits ^ ((bits >> 31) & 0x7FFFFFFF)
            key = key - (key >> 31)
            kpos = lax.broadcasted_iota(I32, (HK, tq), 0) + t * HK
            qpos = lax.broadcasted_iota(I32, (HK, tq), 1) + i * tq
            key = jnp.where(kpos <= qpos, key, INT_MIN)
            key_scr[t] = key
            return key

        def pack(fa, fb):
            return (fa << 16) | fb | FIELD_GUARDS

        def score_chunk(c, carry):
            lg0 = logits(2 * c)
            lg1 = logits(2 * c + 1)
            k0 = to_keys(lg0, 2 * c)
            k1 = to_keys(lg1, 2 * c + 1)
            top = lambda key: _srl(key ^ INT_MIN, 17)
            fld_scr[c] = pack(top(k0), top(k1))
            return carry

        lax.fori_loop(0, n_chunks, score_chunk, 0)

        def count_fields(cand):
            both = ((cand << 16) | cand)[None]

            def body(c, acc):
                w = fld_scr[c].reshape(slabs, SUBLANES, tq)
                hit = _srl(w - both, 15) & 0x00010001
                return acc + jnp.sum(hit, axis=0, dtype=I32)
            acc = lax.fori_loop(0, n_chunks, body, jnp.zeros((SUBLANES, tq), I32))
            acc = (acc & 0xFFFF) + _srl(acc, 16)
            return jnp.broadcast_to(jnp.sum(acc, axis=0, keepdims=True, dtype=I32), (SUBLANES, tq))

        def field_search(above, n_ge):
            def step(it, carry):
                t, n = carry
                cand = t | lax.shift_left(jnp.int32(1), lax.convert_element_type(14 - it, I32))
                cnt = count_fields(cand) + above
                return jnp.where(cnt >= topk, cand, t), jnp.where(cnt >= topk, cnt, n)
            return lax.fori_loop(0, 15, step, (jnp.zeros((SUBLANES, tq), I32), n_ge))

        zeros = jnp.zeros((SUBLANES, tq), I32)
        t_top, n_ge = field_search(zeros, jnp.full((SUBLANES, tq), 2 ** 30, I32))
        above = count_fields(t_top + 1)

        def mid_fields(c, carry):
            def mid(key):
                u = (key ^ INT_MIN).reshape(slabs, SUBLANES, tq)
                alive = _srl(u, 17) == t_top[None]
                return jnp.where(alive, _srl(u, 2) & 0x7FFF, 0).reshape(HK, tq)
            fld_scr[c] = pack(mid(key_scr[2 * c]), mid(key_scr[2 * c + 1]))
            return carry

        lax.fori_loop(0, n_chunks, mid_fields, 0)
        t_mid, n_ge = field_search(above, n_ge)

        def count(pred):
            def body(c, acc):
                for u in range(TK // HK):
                    t = c * (TK // HK) + u
                    kk = key_scr[t].reshape(slabs, SUBLANES, tq)
                    acc = acc + jnp.sum(pred(kk, t).astype(I32), axis=0, dtype=I32)
                return acc
            acc = lax.fori_loop(0, n_chunks, body, jnp.zeros((SUBLANES, tq), I32))
            return jnp.broadcast_to(jnp.sum(acc, axis=0, keepdims=True, dtype=I32), (SUBLANES, tq))

        def bit_step(it, carry):
            t, n = carry
            cand = t | lax.shift_left(jnp.int32(1), lax.convert_element_type(1 - it, I32))
            cnt = count(lambda kk, _: kk >= (cand ^ INT_MIN)[None])
            return jnp.where(cnt >= topk, cand, t), jnp.where(cnt >= topk, cnt, n)

        thr_u, n_ge = lax.fori_loop(0, 2, bit_step, ((t_top << 17) | (t_mid << 2), n_ge))
        thr = thr_u ^ INT_MIN
        excess = jnp.max(jnp.where((n_ge > topk) & (thr > INT_MIN), 1, 0))

        def store_bias(t, sel):
            bias = jnp.where(sel, 0.0, NEG_BIG).astype(F32).reshape(HK, tq)
            key_scr[t] = pltpu.bitcast(bias.T, I32)

        @pl.when(excess == 0)
        def _no_ties():
            floor = jnp.maximum(thr, INT_MIN + 1)[None]

            def to_bias(t, carry):
                store_bias(t, key_scr[t].reshape(slabs, SUBLANES, tq) >= floor)
                return carry

            lax.fori_loop(0, n_tiles, to_bias, 0)

        @pl.when(excess > 0)
        def _ties():
            kidx = (lax.broadcasted_iota(I32, (slabs, SUBLANES, tq), 0) * SUBLANES
                    + lax.broadcasted_iota(I32, (slabs, SUBLANES, tq), 1))
            need = topk - count(lambda kk, _: kk > thr[None])

            def idx_step(it, x):
                cand = x + lax.shift_left(jnp.int32(1), lax.convert_element_type(13 - it, I32))
                cnt = count(lambda kk, t: (kk == thr[None]) & ((kidx + t * HK) < cand[None]))
                return jnp.where(cnt < need, cand, x)

            cut = lax.fori_loop(0, 14, idx_step, jnp.zeros((SUBLANES, tq), I32))
            cut = jnp.where(n_ge > topk, cut, 2 ** 30)[None]
            th = thr[None]

            def to_bias(t, carry):
                kk = key_scr[t].reshape(slabs, SUBLANES, tq)
                store_bias(t, (kk > th) | ((kk == th) & ((kidx + t * HK) <= cut) & (th > INT_MIN)))
                return carry

            lax.fori_loop(0, n_tiles, to_bias, 0)

        def fill(t, carry):
            key_scr[t] = pltpu.bitcast(jnp.full((HK, tq), NEG_BIG, F32), I32)
            return carry

        lax.fori_loop(n_tiles, 2 * n_pairs * (TK // HK), fill, 0)

    @pl.when(i < n_q_blocks)
    def _attend():
        def rows(c):
            cc = jnp.clip(c, 0, n_chunks - 1)
            return pl.ds(pl.multiple_of(cc * TK, TK), TK)

        streams = []
        for g in range(N_STREAMS):
            st = _Stream(g, qm, s, p, a, m, l, acc)
            pr = pg * N_STREAMS + g
            q = qs_ref[0, pr]
            zero = jnp.zeros_like(q)
            st.qm[0:tq] = jnp.where(in_a, q, zero)
            st.qm[tq:2 * tq] = jnp.where(in_a, zero, q)
            st.init()
            streams.append((st, lambda c, pr=pr: ks_ref[0, pr, rows(c), :], lambda c, pr=pr: vs_ref[0, pr, rows(c), :]))

        def biased(s_ref, c):
            bias = jnp.concatenate(
                [pltpu.bitcast(key_scr[c * (TK // HK) + t], F32) for t in range(TK // HK)], axis=1)
            return s_ref[...] + jnp.concatenate([bias, bias], axis=0)

        for st, k_at, _ in streams:
            st.scores(k_at(0), 0)

        def pair(j, carry):
            _pair_body(streams, 2 * j, biased, last=False)
            return carry

        lax.fori_loop(0, n_pairs, pair, 0)
        for g, (st, _, v_at) in enumerate(streams):
            st.pv(v_at(2 * n_pairs - 1), 1)
            o_ref[0, :, g * LANES:(g + 1) * LANES] = st.finish(tq).astype(o_ref.dtype)


def _dsa_call(qi, wi, qs, ki, ks, vs, *, tq, topk, n_q_blocks, interpret=False):
    b, _, tp, _ = qs.shape
    assert HK == tq, "mask tiles are transposed in place"
    once = pl.Buffered(1)
    max_pairs = (tp // TK + 1) // 2
    return pl.pallas_call(
        functools.partial(_dsa_kernel, tq=tq, topk=topk, n_q_blocks=n_q_blocks),
        grid=(b, tp // tq, N_PAIRS // N_STREAMS),
        in_specs=[pl.BlockSpec((1, N_PAIRS, tq, LANES), lambda bi, i, p: (bi, 0, i, 0)),
                  pl.BlockSpec((1, tq, LANES), lambda bi, i, p: (bi, i, 0)),
                  pl.BlockSpec((1, N_PAIRS, tq, LANES), lambda bi, i, p: (bi, 0, i, 0)),
                  pl.BlockSpec((1, tp, LANES), lambda bi, i, p: (bi, 0, 0), pipeline_mode=once),
                  pl.BlockSpec((1, N_PAIRS, tp, LANES), lambda bi, i, p: (bi, 0, 0, 0), pipeline_mode=once),
                  pl.BlockSpec((1, N_PAIRS, tp, LANES), lambda bi, i, p: (bi, 0, 0, 0), pipeline_mode=once)],
        out_specs=pl.BlockSpec((1, tq, N_STREAMS * LANES), lambda bi, i, p: (bi, i, p)),
        out_shape=jax.ShapeDtypeStruct((b, tp, N_PAIRS * LANES), BF16),
        scratch_shapes=[pltpu.VMEM((2 * max_pairs * (TK // HK), HK, tq), I32),
                        pltpu.VMEM((tp // TK, HK, tq), I32),
                        pltpu.VMEM((LANES, IDX_HEADS * tq), BF16),
                        pltpu.VMEM((LANES, tq), F32)]
        + _stream_scratch(tq, LANES),
        compiler_params=pltpu.CompilerParams(
            dimension_semantics=("arbitrary", "arbitrary", "arbitrary"), vmem_limit_bytes=VMEM_LIMIT),
        interpret=interpret,
        name="dsa_attn",
    )(qi, wi, qs, ki, ks, vs)


def _ffn_kernel(h_ref, oa_ref, ob_ref, wo_ref, g2_ref, wg_ref, wu_ref, wd_ref, gf_ref, out_ref):
    o = jnp.concatenate([oa_ref[0], ob_ref[0]], axis=1)
    h1 = h_ref[0] + jnp.dot(o, wo_ref[...], preferred_element_type=F32)
    u = _rms(h1, g2_ref[...]).astype(BF16)
    gate = jnp.dot(u, wg_ref[...], preferred_element_type=F32)
    up = jnp.dot(u, wu_ref[...], preferred_element_type=F32)
    act = (gate * jax.nn.sigmoid(gate) * up).astype(BF16)
    h2 = h1 + jnp.dot(act, wd_ref[...], preferred_element_type=F32)
    out_ref[0] = _rms(h2, gf_ref[...])


def _ffn_call(h, oa, ob, wo, g2, wg, wu, wd, gf, *, row0, n_tiles, interpret=False):
    b, tp, d = h.shape
    tm = TM
    assert row0 % 16 == 0 and row0 + n_tiles * tm <= tp
    once = pl.Buffered(1)
    const = lambda shape: pl.BlockSpec(shape, lambda bi, i: (0,) * len(shape), pipeline_mode=once)
    rows_in = lambda w: pl.BlockSpec((pl.Element(1), pl.Element(tm), pl.Element(w)),
                                     lambda bi, i: (bi, pl.multiple_of(row0 + i * tm, 16), 0))
    return pl.pallas_call(
        _ffn_kernel,
        grid=(b, n_tiles),
        in_specs=[rows_in(d), rows_in(oa.shape[-1]), rows_in(ob.shape[-1]), const(wo.shape), const((1, d)),
                  const(wg.shape), const(wu.shape), const(wd.shape), const((1, d))],
        out_specs=pl.BlockSpec((1, tm, d), lambda bi, i: (bi, i, 0)),
        out_shape=jax.ShapeDtypeStruct((b, n_tiles * tm, d), F32),
        compiler_params=pltpu.CompilerParams(
            dimension_semantics=("arbitrary", "arbitrary"), vmem_limit_bytes=VMEM_LIMIT),
        interpret=interpret,
        name="ffn",
    )(h, oa, ob, wo, g2, wg, wu, wd, gf)


def _layer(h, t_real, attn_norm_g, w_in, mla_q_norm_g, w_uq, mla_kv_norm_g, w_ukv, w_o, ffn_norm_g,
           w_gate, w_up, w_down, out_g, topk, *, row0, n_out_tiles, interpret=False):
    tp = h.shape[1]
    n_q_blocks = -(-t_real // TQ)
    win = _take_cols(w_in, _w_in_cols()).astype(BF16)
    wuq = _take_cols(w_uq, _w_uq_cols()).astype(BF16)
    wukv = jnp.take(w_ukv, jnp.asarray(_w_ukv_cols()), axis=1).astype(BF16)
    tables = _rope_tables(tp)
    qmla, kmla, vmla, qs, ks, vs, qi, ki, wi = _proj_call(
        h, attn_norm_g[None], win, mla_q_norm_g[None], wuq, mla_kv_norm_g[None], wukv, tables,
        interpret=interpret)
    o_mla = _mla_call(qmla, kmla, vmla, tq=TQ, n_q_blocks=n_q_blocks, interpret=interpret)
    o_dsa = _dsa_call(qi, wi, qs, ki, ks, vs, tq=TQ, topk=topk, n_q_blocks=n_q_blocks, interpret=interpret)
    return _ffn_call(h, o_mla, o_dsa, w_o.astype(BF16), ffn_norm_g[None], w_gate.astype(BF16),
                     w_up.astype(BF16), w_down.astype(BF16), out_g[None], row0=row0, n_tiles=n_out_tiles,
                     interpret=interpret)


def kernel(x, meta_tokens, attn_norm_g, w_in, mla_q_norm_g, w_uq, mla_kv_norm_g, w_ukv, w_o,
           ffn_norm_g, w_gate, w_up, w_down, final_norm_g):
    b, seq, d = x.shape
    depth = w_in.shape[0]
    assert depth == 1, "the final norm is fused into the layer's last kernel"
    topk = min(TOPK_MAX, seq // 4)
    meta = jnp.broadcast_to(meta_tokens[None].astype(x.dtype), (b, N_META, d))
    t = N_META + seq
    n_out_tiles = -(-seq // TM)
    tp = -(-max(t, N_META + n_out_tiles * TM) // TK) * TK
    h = jnp.concatenate([meta, x, jnp.zeros((b, tp - t, d), x.dtype)], axis=1)
    out = _layer(h, t, attn_norm_g[0], w_in[0], mla_q_norm_g[0], w_uq[0], mla_kv_norm_g[0], w_ukv[0],
                 w_o[0], ffn_norm_g[0], w_gate[0], w_up[0], w_down[0], final_norm_g, topk,
                 row0=N_META, n_out_tiles=n_out_tiles)
    return out[:, :seq]
```

```python
import functools

import numpy as np
import jax
import jax.numpy as jnp
from jax import lax
from jax.experimental import pallas as pl
from jax.experimental.pallas import tpu as pltpu

F32 = jnp.float32
BF16 = jnp.bfloat16
I32 = jnp.int32

D_MODEL = 1024
N_META = 16
ROPE_THETA = 10000.0
EPS = 1e-6
MLA_HEADS = 8
MLA_Q_RANK = 256
MLA_KV_RANK = 128
MLA_NOPE_DIM = 64
MLA_ROPE_DIM = 32
MLA_QK_DIM = MLA_NOPE_DIM + MLA_ROPE_DIM
MLA_V_DIM = 64
DSA_HEADS = 8
DSA_HEAD_DIM = 64
IDX_HEADS = 8
IDX_DIM = 64
TOPK_MAX = 256
D_FF = 2816

LANES = 128
SUBLANES = 8
N_PAIRS = 4
NEG_BIG = -1e30
INT_MIN = -2 ** 31
FIELD_GUARDS = 0x8000 - 2 ** 31
VMEM_LIMIT = 58 * 1024 * 1024
TQ = 256
TK = 512
TM = TK
HK = TK // 2

_OFF_CQ, _OFF_CKV, _OFF_KR, _OFF_QS, _OFF_KS, _OFF_VS, _OFF_QI, _OFF_KI, _OFF_WI = (
    0, 256, 384, 416, 928, 1440, 1952, 2464, 2528)
_C_IN = 2536
_P_CQ, _P_CKV, _P_QS, _P_KS, _P_VS, _P_QI, _P_KI, _P_KR, _P_WI, _P_END = (
    0, 256, 384, 896, 1408, 1920, 2432, 2560, 2688, 2816)


def _pair_rope_cols(base, hd=64):
    half = hd // 2
    cols = []
    for p in range(N_PAIRS):
        a, b = 2 * p, 2 * p + 1
        cols += [base + a * hd + d for d in range(half)]
        cols += [base + b * hd + d for d in range(half)]
        cols += [base + a * hd + half + d for d in range(half)]
        cols += [base + b * hd + half + d for d in range(half)]
    return cols


def _w_in_cols():
    z = _C_IN
    cols = list(range(_OFF_CQ, _OFF_CQ + 256)) + list(range(_OFF_CKV, _OFF_CKV + 128))
    cols += _pair_rope_cols(_OFF_QS) + _pair_rope_cols(_OFF_KS)
    cols += list(range(_OFF_VS, _OFF_VS + 512))
    cols += _pair_rope_cols(_OFF_QI)
    ki1 = [_OFF_KI + d for d in range(32)]
    ki2 = [_OFF_KI + 32 + d for d in range(32)]
    cols += ki1 + ki1 + ki2 + ki2
    kr1 = [_OFF_KR + d for d in range(16)]
    kr2 = [_OFF_KR + 16 + d for d in range(16)]
    cols += kr1 + kr1 + [z] * 32 + kr2 + kr2 + [z] * 32
    cols += [_OFF_WI + d for d in range(IDX_HEADS)] + [z] * (LANES - IDX_HEADS)
    assert len(cols) == _P_END
    return np.asarray(cols, np.int32)


def _w_uq_cols():
    z = MLA_HEADS * MLA_QK_DIM
    cols = []
    for p in range(N_PAIRS):
        a, b = 2 * p, 2 * p + 1
        cols += [a * MLA_QK_DIM + d for d in range(64)] + [b * MLA_QK_DIM + d for d in range(64)]
        cols += [a * MLA_QK_DIM + 64 + d for d in range(16)] + [b * MLA_QK_DIM + 64 + d for d in range(16)]
        cols += [z] * 32
        cols += [a * MLA_QK_DIM + 80 + d for d in range(16)] + [b * MLA_QK_DIM + 80 + d for d in range(16)]
        cols += [z] * 32
    return np.asarray(cols, np.int32)


def _w_ukv_cols():
    cols = [h * 128 + d for h in range(MLA_HEADS) for d in range(64)]
    cols += [h * 128 + 64 + d for h in range(MLA_HEADS) for d in range(64)]
    return np.asarray(cols, np.int32)


def _take_cols(w, cols):
    wz = jnp.concatenate([w, jnp.zeros((w.shape[0], 1), w.dtype)], axis=1)
    return jnp.take(wz, jnp.asarray(cols), axis=1)


def _rope_tables(tp):
    pos = jnp.arange(tp, dtype=jnp.int32).astype(F32)

    def cs(half):
        inv = jnp.power(ROPE_THETA, -jnp.arange(half, dtype=F32) / half)
        ang = pos[:, None] * inv[None, :]
        return jnp.cos(ang), jnp.sin(ang)

    c32, s32 = cs(32)
    c16, s16 = cs(16)
    z = jnp.zeros((tp, 32), F32)
    c64t = jnp.concatenate([c32, c32, c32, c32], axis=1)
    s64t = jnp.concatenate([-s32, -s32, s32, s32], axis=1)
    c32t = jnp.concatenate([c16, c16, z, c16, c16, z], axis=1)
    s32t = jnp.concatenate([-s16, -s16, z, s16, s16, z], axis=1)
    return c64t, s64t, c32t, s32t


def _rms(x, g):
    return x * lax.rsqrt(jnp.mean(x * x, axis=-1, keepdims=True) + EPS) * g


def _rope_slab(x, c, s):
    return x * c + pltpu.roll(x, 64, axis=1) * s


def _proj_kernel(x_ref, edge_ref, g_ref, win_ref, gq_ref, wuq_ref, gkv_ref, wukv_ref,
                 c64_ref, s64_ref, c32_ref, s32_ref,
                 qmla_ref, kmla_ref, vmla_ref, qs_ref, ks_ref, vs_ref, qi_ref, ki_ref, wi_ref, *, first_tail):
    i = pl.program_id(1)
    h = jnp.where((i == 0) | (i >= first_tail), edge_ref[0, 0], x_ref[0])
    u = _rms(h, g_ref[...]).astype(BF16)
    proj = jnp.dot(u, win_ref[...], preferred_element_type=F32)
    c64, s64, c32, s32 = c64_ref[...], s64_ref[...], c32_ref[...], s32_ref[...]

    cq = _rms(proj[:, _P_CQ:_P_CQ + 256], gq_ref[...]).astype(BF16)
    q = jnp.dot(cq, wuq_ref[...], preferred_element_type=F32)
    ckv = _rms(proj[:, _P_CKV:_P_CKV + 128], gkv_ref[...]).astype(BF16)
    kv = jnp.dot(ckv, wukv_ref[...], preferred_element_type=F32)
    kr = _rope_slab(proj[:, _P_KR:_P_KR + 128], c32, s32).astype(BF16)
    q_scale = MLA_QK_DIM ** -0.5
    s_scale = DSA_HEAD_DIM ** -0.5
    i_scale = IDX_DIM ** -0.5
    for p in range(N_PAIRS):
        qn = q[:, p * 256:p * 256 + 128] * q_scale
        qr = _rope_slab(q[:, p * 256 + 128:p * 256 + 256], c32, s32) * q_scale
        qmla_ref[0, p, :, 0:128] = qn.astype(BF16)
        qmla_ref[0, p, :, 128:256] = qr.astype(BF16)
        kmla_ref[0, p, :, 0:128] = kv[:, p * 128:(p + 1) * 128].astype(BF16)
        kmla_ref[0, p, :, 128:256] = kr
        vmla_ref[0, p] = kv[:, 512 + p * 128:512 + (p + 1) * 128].astype(BF16)
        sl = slice(p * 128, (p + 1) * 128)
        qs_ref[0, p] = (_rope_slab(proj[:, _P_QS:_P_KS][:, sl], c64, s64) * s_scale).astype(BF16)
        ks_ref[0, p] = _rope_slab(proj[:, _P_KS:_P_VS][:, sl], c64, s64).astype(BF16)
        vs_ref[0, p] = proj[:, _P_VS:_P_QI][:, sl].astype(BF16)
        qi_ref[0, p] = (_rope_slab(proj[:, _P_QI:_P_KI][:, sl], c64, s64) * i_scale).astype(BF16)
    ki_ref[0] = _rope_slab(proj[:, _P_KI:_P_KR], c64, s64).astype(BF16)
    wi_ref[0] = proj[:, _P_WI:_P_END] * (IDX_HEADS ** -0.5)


def _proj_call(x, edges, g, win, gq, wuq, gkv, wukv, tables, *, tp, first_tail, interpret=False):
    b, seq, d = x.shape
    tm = TM
    nt = tp // tm
    assert seq >= tm and (seq - tm) % 16 == 0 and N_META % 16 == 0
    x_rows = pl.BlockSpec(
        (pl.Element(1), pl.Element(tm), pl.Element(d)),
        lambda bi, i: (bi, pl.multiple_of(jnp.clip(i * tm - N_META, 0, seq - tm), 16), 0))
    edge = pl.BlockSpec((1, 1, tm, d), lambda bi, i: (bi, jnp.where(i == 0, 0, 1 + jnp.maximum(i - first_tail, 0)), 0, 0))
    const = lambda shape: pl.BlockSpec(shape, lambda bi, i: (0,) * len(shape))
    tab = pl.BlockSpec((tm, LANES), lambda bi, i: (i, 0))
    pair = lambda w: pl.BlockSpec((1, N_PAIRS, tm, w), lambda bi, i: (bi, 0, i, 0))
    row = lambda w: pl.BlockSpec((1, tm, w), lambda bi, i: (bi, i, 0))
    sds = jax.ShapeDtypeStruct
    out_shape = (
        sds((b, N_PAIRS, tp, 256), BF16), sds((b, N_PAIRS, tp, 256), BF16), sds((b, N_PAIRS, tp, 128), BF16),
        sds((b, N_PAIRS, tp, 128), BF16), sds((b, N_PAIRS, tp, 128), BF16), sds((b, N_PAIRS, tp, 128), BF16),
        sds((b, N_PAIRS, tp, 128), BF16), sds((b, tp, 128), BF16), sds((b, tp, 128), F32))
    return pl.pallas_call(
        functools.partial(_proj_kernel, first_tail=first_tail),
        grid=(b, nt),
        in_specs=[x_rows, edge, const((1, d)), const(win.shape), const((1, MLA_Q_RANK)), const(wuq.shape),
                  const((1, MLA_KV_RANK)), const(wukv.shape), tab, tab, tab, tab],
        out_specs=(pair(256), pair(256), pair(128), pair(128), pair(128), pair(128), pair(128),
                   row(128), row(128)),
        out_shape=out_shape,
        compiler_params=pltpu.CompilerParams(
            dimension_semantics=("arbitrary", "arbitrary"), vmem_limit_bytes=VMEM_LIMIT),
        interpret=interpret,
        name="proj",
    )(x, edges, g, win, gq, wuq, gkv, wukv, *tables)


_NT = (((1,), (1,)), ((), ()))
N_STREAMS = 2


def _srl(x, n):
    return lax.shift_right_logical(x, jnp.full(x.shape, n, x.dtype))


def _chunk_counts(i, tq):
    n_chunks = ((i + 1) * tq + TK - 1) // TK
    return n_chunks, (n_chunks + 1) // 2


class _Stream:
    def __init__(self, g, qm, s, p, a, m, l, acc):
        self.qm = qm.at[g]
        self.s = (s.at[0, g], s.at[1, g])
        self.p = (p.at[0, g], p.at[1, g])
        self.a = (a.at[0, g], a.at[1, g])
        self.m, self.l, self.acc = m.at[g], l.at[g], acc.at[g]

    def init(self):
        self.m[...] = jnp.full(self.m.shape, NEG_BIG, F32)
        self.l[...] = jnp.zeros(self.l.shape, F32)
        self.acc[...] = jnp.zeros(self.acc.shape, F32)
        self.p[1][...] = jnp.zeros(self.p[1].shape, BF16)
        self.a[1][...] = jnp.ones(self.a[1].shape, F32)

    def scores(self, k, slot):
        self.s[slot][...] = lax.dot_general(self.qm[...], k, _NT, preferred_element_type=F32)

    def softmax(self, s, slot):
        m_prev = self.m[...]
        m_new = jnp.maximum(m_prev, jnp.max(s, axis=-1, keepdims=True))
        alpha = jnp.exp(m_prev - m_new)
        p = jnp.exp(s - jnp.concatenate([m_new] * (s.shape[1] // LANES), axis=1))
        self.l[...] = alpha * self.l[...] + jnp.sum(p, axis=-1, keepdims=True)
        self.m[...] = m_new
        self.p[slot][...] = p.astype(BF16)
        self.a[slot][...] = alpha

    def pv(self, v, slot):
        self.acc[...] = self.a[slot][...] * self.acc[...] + jnp.dot(self.p[slot][...], v, preferred_element_type=F32)

    def finish(self, tq):
        lane = lax.broadcasted_iota(I32, (tq, LANES), 1)
        oa = self.acc[0:tq] / self.l[0:tq]
        ob = self.acc[tq:2 * tq] / self.l[tq:2 * tq]
        return jnp.where(lane < 64, oa, ob)


def _stream_scratch(tq, width):
    g = N_STREAMS
    return [pltpu.VMEM((g, 2 * tq, width), BF16),
            pltpu.VMEM((2, g, 2 * tq, TK), F32), pltpu.VMEM((2, g, 2 * tq, TK), BF16),
            pltpu.VMEM((2, g, 2 * tq, LANES), F32),
            pltpu.VMEM((g, 2 * tq, LANES), F32), pltpu.VMEM((g, 2 * tq, LANES), F32),
            pltpu.VMEM((g, 2 * tq, LANES), F32)]


def _pair_body(streams, c, prep, last):
    for st, _, v_at in streams:
        st.pv(v_at(c - 1), 1)
    for st, _, _ in streams:
        st.softmax(prep(st.s[0], c), 0)
    for st, k_at, _ in streams:
        st.scores(k_at(c + 1), 1)
    for st, _, v_at in streams:
        st.pv(v_at(c), 0)
    for st, _, _ in streams:
        st.softmax(prep(st.s[1], c + 1), 1)
    for st, k_at, v_at in streams:
        if last:
            st.pv(v_at(c + 1), 1)
        else:
            st.scores(k_at(c + 2), 0)


def _mla_kernel(q_ref, k_ref, v_ref, o_ref, qm, s, p, a, m, l, acc, *, tq, n_q_blocks):
    i = pl.program_id(2)

    @pl.when(i >= n_q_blocks)
    def _pad_rows():
        o_ref[...] = jnp.zeros(o_ref.shape, o_ref.dtype)

    @pl.when(i < n_q_blocks)
    def _attend():
        lane = lax.broadcasted_iota(I32, (tq, 256), 1)
        in_a = (lane < 64) | ((lane >= 128) & (lane < 144)) | ((lane >= 192) & (lane < 208))
        in_b = ((lane >= 64) & (lane < 128)) | ((lane >= 144) & (lane < 160)) | ((lane >= 208) & (lane < 224))
        n_chunks, n_pairs = _chunk_counts(i, tq)

        def rows(c):
            cc = jnp.clip(c, 0, n_chunks - 1)
            return pl.ds(pl.multiple_of(cc * TK, TK), TK)

        streams = []
        for g in range(N_STREAMS):
            st = _Stream(g, qm, s, p, a, m, l, acc)
            q = q_ref[0, g]
            zero = jnp.zeros_like(q)
            st.qm[0:tq] = jnp.where(in_a, q, zero)
            st.qm[tq:2 * tq] = jnp.where(in_b, q, zero)
            st.init()
            streams.append((st, lambda c, g=g: k_ref[0, g, rows(c), :], lambda c, g=g: v_ref[0, g, rows(c), :]))

        def masked(s_ref, c):
            qry = (lax.broadcasted_iota(I32, (2 * tq, TK), 0) & (tq - 1)) + i * tq
            key = lax.broadcasted_iota(I32, (2 * tq, TK), 1) + c * TK
            return jnp.where(key <= qry, s_ref[...], NEG_BIG)

        for st, k_at, _ in streams:
            st.scores(k_at(0), 0)

        def pair(j, carry):
            _pair_body(streams, 2 * j, lambda s_ref, c: s_ref[...], last=False)
            return carry

        lax.fori_loop(0, n_pairs - 1, pair, 0)
        _pair_body(streams, 2 * (n_pairs - 1), masked, last=True)
        for g, (st, _, _) in enumerate(streams):
            o_ref[0, :, g * LANES:(g + 1) * LANES] = st.finish(tq).astype(o_ref.dtype)


def _mla_call(qmla, kmla, vmla, *, tq, n_q_blocks, interpret=False):
    b, _, tp, _ = qmla.shape
    g = N_STREAMS
    return pl.pallas_call(
        functools.partial(_mla_kernel, tq=tq, n_q_blocks=n_q_blocks),
        grid=(b, N_PAIRS // g, tp // tq),
        in_specs=[pl.BlockSpec((1, g, tq, 256), lambda bi, pg, i: (bi, pg, i, 0)),
                  pl.BlockSpec((1, g, tp, 256), lambda bi, pg, i: (bi, pg, 0, 0)),
                  pl.BlockSpec((1, g, tp, 128), lambda bi, pg, i: (bi, pg, 0, 0))],
        out_specs=pl.BlockSpec((1, tq, g * LANES), lambda bi, pg, i: (bi, i, pg)),
        out_shape=jax.ShapeDtypeStruct((b, tp, N_PAIRS * LANES), BF16),
        scratch_shapes=_stream_scratch(tq, 256),
        compiler_params=pltpu.CompilerParams(
            dimension_semantics=("arbitrary", "arbitrary", "arbitrary"), vmem_limit_bytes=VMEM_LIMIT),
        interpret=interpret,
        name="mla_attn",
    )(qmla, kmla, vmla)


def _dsa_kernel(qi_ref, wi_ref, qs_ref, ki_ref, ks_ref, vs_ref, o_ref,
                key_scr, fld_scr, qim_scr, wt_scr,
                qm, s, p, a, m, l, acc, *, tq, topk, n_q_blocks):
    i = pl.program_id(1)
    pg = pl.program_id(2)
    n_chunks, n_pairs = _chunk_counts(i, tq)
    n_tiles = n_chunks * (TK // HK)
    slabs = HK // SUBLANES
    lane = lax.broadcasted_iota(I32, (tq, LANES), 1)
    in_a = (lane < 32) | ((lane >= 64) & (lane < 96))

    @pl.when(i >= n_q_blocks)
    def _pad_rows():
        o_ref[...] = jnp.zeros(o_ref.shape, o_ref.dtype)

    @pl.when((i < n_q_blocks) & (pg == 0))
    def _select():
        for pp in range(N_PAIRS):
            qp = qi_ref[0, pp].astype(F32)
            zero = jnp.zeros_like(qp)
            qim_scr[:, (2 * pp) * tq:(2 * pp + 1) * tq] = jnp.where(in_a, qp, zero).T.astype(BF16)
            qim_scr[:, (2 * pp + 1) * tq:(2 * pp + 2) * tq] = jnp.where(in_a, zero, qp).T.astype(BF16)
        wt_scr[...] = wi_ref[0].T

        def logits(t):
            kc = ki_ref[0, pl.ds(pl.multiple_of(t * HK, HK), HK), :]
            return jnp.dot(kc, qim_scr[...], preferred_element_type=F32)

        def to_keys(lg, t):
            sc = jnp.zeros((HK, tq), F32)
            for hh in range(IDX_HEADS):
                sc = sc + wt_scr[hh:hh + 1, :] * jnp.maximum(lg[:, hh * tq:(hh + 1) * tq], 0.0)
            bits = pltpu.bitcast(sc, I32)
            key = bits ^ ((bits >> 31) & 0x7FFFFFFF)
            key = key - (key >> 31)
            kpos = lax.broadcasted_iota(I32, (HK, tq), 0) + t * HK
            qpos = lax.broadcasted_iota(I32, (HK, tq), 1) + i * tq
            key = jnp.where(kpos <= qpos, key, INT_MIN)
            key_scr[t] = key
            return key

        def pack(fa, fb):
            return (fa << 16) | fb | FIELD_GUARDS

        def score_chunk(c, carry):
            lg0 = logits(2 * c)
            lg1 = logits(2 * c + 1)
            k0 = to_keys(lg0, 2 * c)
            k1 = to_keys(lg1, 2 * c + 1)
            top = lambda key: _srl(key ^ INT_MIN, 17)
            fld_scr[c] = pack(top(k0), top(k1))
            return carry

        lax.fori_loop(0, n_chunks, score_chunk, 0)

        def count_fields(cand):
            both = ((cand << 16) | cand)[None]

            def body(c, acc):
                w = fld_scr[c].reshape(slabs, SUBLANES, tq)
                hit = _srl(w - both, 15) & 0x00010001
                return acc + jnp.sum(hit, axis=0, dtype=I32)
            acc = lax.fori_loop(0, n_chunks, body, jnp.zeros((SUBLANES, tq), I32))
            acc = (acc & 0xFFFF) + _srl(acc, 16)
            return jnp.broadcast_to(jnp.sum(acc, axis=0, keepdims=True, dtype=I32), (SUBLANES, tq))

        def field_search(above, n_ge):
            def step(it, carry):
                t, n = carry
                cand = t | lax.shift_left(jnp.int32(1), lax.convert_element_type(14 - it, I32))
                cnt = count_fields(cand) + above
                return jnp.where(cnt >= topk, cand, t), jnp.where(cnt >= topk, cnt, n)
            return lax.fori_loop(0, 15, step, (jnp.zeros((SUBLANES, tq), I32), n_ge))

        zeros = jnp.zeros((SUBLANES, tq), I32)
        t_top, n_ge = field_search(zeros, jnp.full((SUBLANES, tq), 2 ** 30, I32))
        above = count_fields(t_top + 1)

        def mid_fields(c, carry):
            def mid(key):
                u = (key ^ INT_MIN).reshape(slabs, SUBLANES, tq)
                alive = _srl(u, 17) == t_top[None]
                return jnp.where(alive, _srl(u, 2) & 0x7FFF, 0).reshape(HK, tq)
            fld_scr[c] = pack(mid(key_scr[2 * c]), mid(key_scr[2 * c + 1]))
            return carry

        lax.fori_loop(0, n_chunks, mid_fields, 0)
        t_mid, n_ge = field_search(above, n_ge)

        def count(pred):
            def body(c, acc):
                for u in range(TK // HK):
                    t = c * (TK // HK) + u
                    kk = key_scr[t].reshape(slabs, SUBLANES, tq)
                    acc = acc + jnp.sum(pred(kk, t).astype(I32), axis=0, dtype=I32)
                return acc
            acc = lax.fori_loop(0, n_chunks, body, jnp.zeros((SUBLANES, tq), I32))
            return jnp.broadcast_to(jnp.sum(acc, axis=0, keepdims=True, dtype=I32), (SUBLANES, tq))

        def bit_step(it, carry):
            t, n = carry
            cand = t | lax.shift_left(jnp.int32(1), lax.convert_element_type(1 - it, I32))
            cnt = count(lambda kk, _: kk >= (cand ^ INT_MIN)[None])
            return jnp.where(cnt >= topk, cand, t), jnp.where(cnt >= topk, cnt, n)

        thr_u, n_ge = lax.fori_loop(0, 2, bit_step, ((t_top << 17) | (t_mid << 2), n_ge))
        thr = thr_u ^ INT_MIN
        excess = jnp.max(jnp.where((n_ge > topk) & (thr > INT_MIN), 1, 0))

        def store_bias(t, sel):
            bias = jnp.where(sel, 0.0, NEG_BIG).astype(F32).reshape(HK, tq)
            key_scr[t] = pltpu.bitcast(bias.T, I32)

        @pl.when(excess == 0)
        def _no_ties():
            floor = jnp.maximum(thr, INT_MIN + 1)[None]

            def to_bias(t, carry):
                store_bias(t, key_scr[t].reshape(slabs, SUBLANES, tq) >= floor)
                return carry

            lax.fori_loop(0, n_tiles, to_bias, 0)

        @pl.when(excess > 0)
        def _ties():
            kidx = (lax.broadcasted_iota(I32, (slabs, SUBLANES, tq), 0) * SUBLANES
                    + lax.broadcasted_iota(I32, (slabs, SUBLANES, tq), 1))
            need = topk - count(lambda kk, _: kk > thr[None])

            def idx_step(it, x):
                cand = x + lax.shift_left(jnp.int32(1), lax.convert_element_type(13 - it, I32))
                cnt = count(lambda kk, t: (kk == thr[None]) & ((kidx + t * HK) < cand[None]))
                return jnp.where(cnt < need, cand, x)

            cut = lax.fori_loop(0, 14, idx_step, jnp.zeros((SUBLANES, tq), I32))
            cut = jnp.where(n_ge > topk, cut, 2 ** 30)[None]
            th = thr[None]

            def to_bias(t, carry):
                kk = key_scr[t].reshape(slabs, SUBLANES, tq)
                store_bias(t, (kk > th) | ((kk == th) & ((kidx + t * HK) <= cut) & (th > INT_MIN)))
                return carry

            lax.fori_loop(0, n_tiles, to_bias, 0)

        def fill(t, carry):
            key_scr[t] = pltpu.bitcast(jnp.full((HK, tq), NEG_BIG, F32), I32)
            return carry

        lax.fori_loop(n_tiles, 2 * n_pairs * (TK // HK), fill, 0)

    @pl.when(i < n_q_blocks)
    def _attend():
        def rows(c):
            cc = jnp.clip(c, 0, n_chunks - 1)
            return pl.ds(pl.multiple_of(cc * TK, TK), TK)

        streams = []
        for g in range(N_STREAMS):
            st = _Stream(g, qm, s, p, a, m, l, acc)
            pr = pg * N_STREAMS + g
            q = qs_ref[0, pr]
            zero = jnp.zeros_like(q)
            st.qm[0:tq] = jnp.where(in_a, q, zero)
            st.qm[tq:2 * tq] = jnp.where(in_a, zero, q)
            st.init()
            streams.append((st, lambda c, pr=pr: ks_ref[0, pr, rows(c), :], lambda c, pr=pr: vs_ref[0, pr, rows(c), :]))

        def biased(s_ref, c):
            bias = jnp.concatenate(
                [pltpu.bitcast(key_scr[c * (TK // HK) + t], F32) for t in range(TK // HK)], axis=1)
            return s_ref[...] + jnp.concatenate([bias, bias], axis=0)

        for st, k_at, _ in streams:
            st.scores(k_at(0), 0)

        def pair(j, carry):
            _pair_body(streams, 2 * j, biased, last=False)
            return carry

        lax.fori_loop(0, n_pairs, pair, 0)
        for g, (st, _, v_at) in enumerate(streams):
            st.pv(v_at(2 * n_pairs - 1), 1)
            o_ref[0, :, g * LANES:(g + 1) * LANES] = st.finish(tq).astype(o_ref.dtype)


def _dsa_call(qi, wi, qs, ki, ks, vs, *, tq, topk, n_q_blocks, interpret=False):
    b, _, tp, _ = qs.shape
    assert HK == tq, "mask tiles are transposed in place"
    once = pl.Buffered(1)
    max_pairs = (tp // TK + 1) // 2
    return pl.pallas_call(
        functools.partial(_dsa_kernel, tq=tq, topk=topk, n_q_blocks=n_q_blocks),
        grid=(b, tp // tq, N_PAIRS // N_STREAMS),
        in_specs=[pl.BlockSpec((1, N_PAIRS, tq, LANES), lambda bi, i, p: (bi, 0, i, 0)),
                  pl.BlockSpec((1, tq, LANES), lambda bi, i, p: (bi, i, 0)),
                  pl.BlockSpec((1, N_PAIRS, tq, LANES), lambda bi, i, p: (bi, 0, i, 0)),
                  pl.BlockSpec((1, tp, LANES), lambda bi, i, p: (bi, 0, 0), pipeline_mode=once),
                  pl.BlockSpec((1, N_PAIRS, tp, LANES), lambda bi, i, p: (bi, 0, 0, 0), pipeline_mode=once),
                  pl.BlockSpec((1, N_PAIRS, tp, LANES), lambda bi, i, p: (bi, 0, 0, 0), pipeline_mode=once)],
        out_specs=pl.BlockSpec((1, tq, N_STREAMS * LANES), lambda bi, i, p: (bi, i, p)),
        out_shape=jax.ShapeDtypeStruct((b, tp, N_PAIRS * LANES), BF16),
        scratch_shapes=[pltpu.VMEM((2 * max_pairs * (TK // HK), HK, tq), I32),
                        pltpu.VMEM((tp // TK, HK, tq), I32),
                        pltpu.VMEM((LANES, IDX_HEADS * tq), BF16),
                        pltpu.VMEM((LANES, tq), F32)]
        + _stream_scratch(tq, LANES),
        compiler_params=pltpu.CompilerParams(
            dimension_semantics=("arbitrary", "arbitrary", "arbitrary"), vmem_limit_bytes=VMEM_LIMIT),
        interpret=interpret,
        name="dsa_attn",
    )(qi, wi, qs, ki, ks, vs)


def _ffn_kernel(x_ref, oa_ref, ob_ref, wo_ref, g2_ref, wg_ref, wu_ref, wd_ref, gf_ref, out_ref):
    o = jnp.concatenate([oa_ref[0], ob_ref[0]], axis=1)
    h1 = x_ref[0] + jnp.dot(o, wo_ref[...], preferred_element_type=F32)
    u = _rms(h1, g2_ref[...]).astype(BF16)
    gate = jnp.dot(u, wg_ref[...], preferred_element_type=F32)
    up = jnp.dot(u, wu_ref[...], preferred_element_type=F32)
    act = (gate * jax.nn.sigmoid(gate) * up).astype(BF16)
    h2 = h1 + jnp.dot(act, wd_ref[...], preferred_element_type=F32)
    out_ref[0] = _rms(h2, gf_ref[...])


def _ffn_call(x, oa, ob, wo, g2, wg, wu, wd, gf, *, row0, n_tiles, interpret=False):
    b, _, d = x.shape
    tm = TM
    assert row0 % 16 == 0 and row0 + n_tiles * tm <= oa.shape[1]
    once = pl.Buffered(1)
    const = lambda shape: pl.BlockSpec(shape, lambda bi, i: (0,) * len(shape), pipeline_mode=once)
    rows_in = lambda w: pl.BlockSpec((pl.Element(1), pl.Element(tm), pl.Element(w)),
                                     lambda bi, i: (bi, pl.multiple_of(row0 + i * tm, 16), 0))
    return pl.pallas_call(
        _ffn_kernel,
        grid=(b, n_tiles),
        in_specs=[pl.BlockSpec((1, tm, d), lambda bi, i: (bi, i, 0)),
                  rows_in(oa.shape[-1]), rows_in(ob.shape[-1]), const(wo.shape), const((1, d)),
                  const(wg.shape), const(wu.shape), const(wd.shape), const((1, d))],
        out_specs=pl.BlockSpec((1, tm, d), lambda bi, i: (bi, i, 0)),
        out_shape=jax.ShapeDtypeStruct((b, n_tiles * tm, d), F32),
        compiler_params=pltpu.CompilerParams(
            dimension_semantics=("arbitrary", "arbitrary"), vmem_limit_bytes=VMEM_LIMIT),
        interpret=interpret,
        name="ffn",
    )(x, oa, ob, wo, g2, wg, wu, wd, gf)


def _layer(x, meta, attn_norm_g, w_in, mla_q_norm_g, w_uq, mla_kv_norm_g, w_ukv, w_o, ffn_norm_g,
           w_gate, w_up, w_down, out_g, topk, *, interpret=False):
    b, seq, d = x.shape
    t = N_META + seq
    n_out_tiles = -(-seq // TM)
    tp = -(-max(t, N_META + n_out_tiles * TM) // TK) * TK
    first_tail = t // TM
    tail = jnp.concatenate([x[:, first_tail * TM - N_META:], jnp.zeros((b, tp - t, d), x.dtype)], axis=1)
    head = jnp.concatenate([jnp.broadcast_to(meta[None].astype(x.dtype), (b, N_META, d)), x[:, :TM - N_META]], axis=1)
    edges = jnp.concatenate([head[:, None], tail.reshape(b, tp // TM - first_tail, TM, d)], axis=1)
    n_q_blocks = -(-t // TQ)
    win = _take_cols(w_in, _w_in_cols()).astype(BF16)
    wuq = _take_cols(w_uq, _w_uq_cols()).astype(BF16)
    wukv = jnp.take(w_ukv, jnp.asarray(_w_ukv_cols()), axis=1).astype(BF16)
    tables = _rope_tables(tp)
    qmla, kmla, vmla, qs, ks, vs, qi, ki, wi = _proj_call(
        x, edges, attn_norm_g[None], win, mla_q_norm_g[None], wuq, mla_kv_norm_g[None], wukv, tables,
        tp=tp, first_tail=first_tail, interpret=interpret)
    o_mla = _mla_call(qmla, kmla, vmla, tq=TQ, n_q_blocks=n_q_blocks, interpret=interpret)
    o_dsa = _dsa_call(qi, wi, qs, ki, ks, vs, tq=TQ, topk=topk, n_q_blocks=n_q_blocks, interpret=interpret)
    return _ffn_call(x, o_mla, o_dsa, w_o.astype(BF16), ffn_norm_g[None], w_gate.astype(BF16),
                     w_up.astype(BF16), w_down.astype(BF16), out_g[None], row0=N_META, n_tiles=n_out_tiles,
                     interpret=interpret)


def kernel(x, meta_tokens, attn_norm_g, w_in, mla_q_norm_g, w_uq, mla_kv_norm_g, w_ukv, w_o,
           ffn_norm_g, w_gate, w_up, w_down, final_norm_g):
    seq = x.shape[1]
    depth = w_in.shape[0]
    assert depth == 1, "the final norm is fused into the layer's last kernel"
    topk = min(TOPK_MAX, seq // 4)
    out = _layer(x, meta_tokens, attn_norm_g[0], w_in[0], mla_q_norm_g[0], w_uq[0], mla_kv_norm_g[0], w_ukv[0],
                 w_o[0], ffn_norm_g[0], w_gate[0], w_up[0], w_down[0], final_norm_g, topk)
    return out[:, :seq]
```

```python
import functools

import numpy as np
import jax
import jax.numpy as jnp
from jax import lax
from jax.experimental import pallas as pl
from jax.experimental.pallas import tpu as pltpu

F32 = jnp.float32
BF16 = jnp.bfloat16
I32 = jnp.int32

D_MODEL = 1024
N_META = 16
ROPE_THETA = 10000.0
EPS = 1e-6
MLA_HEADS = 8
MLA_Q_RANK = 256
MLA_KV_RANK = 128
MLA_NOPE_DIM = 64
MLA_ROPE_DIM = 32
MLA_QK_DIM = MLA_NOPE_DIM + MLA_ROPE_DIM
MLA_V_DIM = 64
DSA_HEADS = 8
DSA_HEAD_DIM = 64
IDX_HEADS = 8
IDX_DIM = 64
TOPK_MAX = 256
D_FF = 2816

LANES = 128
SUBLANES = 8
N_PAIRS = 4
NEG_BIG = -1e30
INT_MIN = -2 ** 31
FIELD_GUARDS = 0x8000 - 2 ** 31
VMEM_LIMIT = 58 * 1024 * 1024
TQ = 256
TK = 512
TM = 512
HK = TK // 2

_OFF_CQ, _OFF_CKV, _OFF_KR, _OFF_QS, _OFF_KS, _OFF_VS, _OFF_QI, _OFF_KI, _OFF_WI = (
    0, 256, 384, 416, 928, 1440, 1952, 2464, 2528)
_C_IN = 2536
_P_CQ, _P_CKV, _P_QS, _P_KS, _P_VS, _P_QI, _P_KI, _P_KR, _P_WI, _P_END = (
    0, 256, 384, 896, 1408, 1920, 2432, 2560, 2688, 2816)


def _pair_rope_cols(base, hd=64):
    half = hd // 2
    cols = []
    for p in range(N_PAIRS):
        a, b = 2 * p, 2 * p + 1
        cols += [base + a * hd + d for d in range(half)]
        cols += [base + b * hd + d for d in range(half)]
        cols += [base + a * hd + half + d for d in range(half)]
        cols += [base + b * hd + half + d for d in range(half)]
    return cols


def _w_in_cols():
    z = _C_IN
    cols = list(range(_OFF_CQ, _OFF_CQ + 256)) + list(range(_OFF_CKV, _OFF_CKV + 128))
    cols += _pair_rope_cols(_OFF_QS) + _pair_rope_cols(_OFF_KS)
    cols += list(range(_OFF_VS, _OFF_VS + 512))
    cols += _pair_rope_cols(_OFF_QI)
    ki1 = [_OFF_KI + d for d in range(32)]
    ki2 = [_OFF_KI + 32 + d for d in range(32)]
    cols += ki1 + ki1 + ki2 + ki2
    kr1 = [_OFF_KR + d for d in range(16)]
    kr2 = [_OFF_KR + 16 + d for d in range(16)]
    cols += kr1 + kr1 + [z] * 32 + kr2 + kr2 + [z] * 32
    cols += [_OFF_WI + d for d in range(IDX_HEADS)] + [z] * (LANES - IDX_HEADS)
    assert len(cols) == _P_END
    return np.asarray(cols, np.int32)


def _w_uq_cols():
    z = MLA_HEADS * MLA_QK_DIM
    cols = []
    for p in range(N_PAIRS):
        a, b = 2 * p, 2 * p + 1
        cols += [a * MLA_QK_DIM + d for d in range(64)] + [b * MLA_QK_DIM + d for d in range(64)]
        cols += [a * MLA_QK_DIM + 64 + d for d in range(16)] + [b * MLA_QK_DIM + 64 + d for d in range(16)]
        cols += [z] * 32
        cols += [a * MLA_QK_DIM + 80 + d for d in range(16)] + [b * MLA_QK_DIM + 80 + d for d in range(16)]
        cols += [z] * 32
    return np.asarray(cols, np.int32)


def _w_ukv_cols():
    cols = [h * 128 + d for h in range(MLA_HEADS) for d in range(64)]
    cols += [h * 128 + 64 + d for h in range(MLA_HEADS) for d in range(64)]
    return np.asarray(cols, np.int32)


def _take_cols(w, cols):
    wz = jnp.concatenate([w, jnp.zeros((w.shape[0], 1), w.dtype)], axis=1)
    return jnp.take(wz, jnp.asarray(cols), axis=1)


def _rope_tables(tp):
    pos = jnp.arange(tp, dtype=jnp.int32).astype(F32)

    def cs(half):
        inv = jnp.power(ROPE_THETA, -jnp.arange(half, dtype=F32) / half)
        ang = pos[:, None] * inv[None, :]
        return jnp.cos(ang), jnp.sin(ang)

    c32, s32 = cs(32)
    c16, s16 = cs(16)
    z = jnp.zeros((tp, 32), F32)
    c64t = jnp.concatenate([c32, c32, c32, c32], axis=1)
    s64t = jnp.concatenate([-s32, -s32, s32, s32], axis=1)
    c32t = jnp.concatenate([c16, c16, z, c16, c16, z], axis=1)
    s32t = jnp.concatenate([-s16, -s16, z, s16, s16, z], axis=1)
    return c64t, s64t, c32t, s32t


def _rms(x, g):
    return x * lax.rsqrt(jnp.mean(x * x, axis=-1, keepdims=True) + EPS) * g


def _rope_slab(x, c, s):
    return x * c + pltpu.roll(x, 64, axis=1) * s


def _proj_kernel(x_ref, edge_ref, g_ref, win_ref, gq_ref, wuq_ref, gkv_ref, wukv_ref,
                 c64_ref, s64_ref, c32_ref, s32_ref,
                 qmla_ref, kmla_ref, vmla_ref, qs_ref, ks_ref, vs_ref, qi_ref, ki_ref, wi_ref, *, first_tail):
    i = pl.program_id(1)
    h = jnp.where((i == 0) | (i >= first_tail), edge_ref[0, 0], x_ref[0])
    u = _rms(h, g_ref[...]).astype(BF16)
    proj = jnp.dot(u, win_ref[...], preferred_element_type=F32)
    c64, s64, c32, s32 = c64_ref[...], s64_ref[...], c32_ref[...], s32_ref[...]

    cq = _rms(proj[:, _P_CQ:_P_CQ + 256], gq_ref[...]).astype(BF16)
    q = jnp.dot(cq, wuq_ref[...], preferred_element_type=F32)
    ckv = _rms(proj[:, _P_CKV:_P_CKV + 128], gkv_ref[...]).astype(BF16)
    kv = jnp.dot(ckv, wukv_ref[...], preferred_element_type=F32)
    kr = _rope_slab(proj[:, _P_KR:_P_KR + 128], c32, s32).astype(BF16)
    q_scale = MLA_QK_DIM ** -0.5
    s_scale = DSA_HEAD_DIM ** -0.5
    i_scale = IDX_DIM ** -0.5
    for p in range(N_PAIRS):
        qn = q[:, p * 256:p * 256 + 128] * q_scale
        qr = _rope_slab(q[:, p * 256 + 128:p * 256 + 256], c32, s32) * q_scale
        qmla_ref[0, p, :, 0:128] = qn.astype(BF16)
        qmla_ref[0, p, :, 128:256] = qr.astype(BF16)
        kmla_ref[0, p, :, 0:128] = kv[:, p * 128:(p + 1) * 128].astype(BF16)
        kmla_ref[0, p, :, 128:256] = kr
        vmla_ref[0, p] = kv[:, 512 + p * 128:512 + (p + 1) * 128].astype(BF16)
        sl = slice(p * 128, (p + 1) * 128)
        qs_ref[0, p] = (_rope_slab(proj[:, _P_QS:_P_KS][:, sl], c64, s64) * s_scale).astype(BF16)
        ks_ref[0, p] = _rope_slab(proj[:, _P_KS:_P_VS][:, sl], c64, s64).astype(BF16)
        vs_ref[0, p] = proj[:, _P_VS:_P_QI][:, sl].astype(BF16)
        qi_ref[0, p] = (_rope_slab(proj[:, _P_QI:_P_KI][:, sl], c64, s64) * i_scale).astype(BF16)
    ki_ref[0] = _rope_slab(proj[:, _P_KI:_P_KR], c64, s64).astype(BF16)
    wi_ref[0] = proj[:, _P_WI:_P_END] * (IDX_HEADS ** -0.5)


def _proj_call(x, edges, g, win, gq, wuq, gkv, wukv, tables, *, tp, first_tail, interpret=False):
    b, seq, d = x.shape
    tm = TM
    nt = tp // tm
    assert seq >= tm and (seq - tm) % 16 == 0 and N_META % 16 == 0
    x_rows = pl.BlockSpec(
        (pl.Element(1), pl.Element(tm), pl.Element(d)),
        lambda bi, i: (bi, pl.multiple_of(jnp.clip(i * tm - N_META, 0, seq - tm), 16), 0))
    edge = pl.BlockSpec((1, 1, tm, d), lambda bi, i: (bi, jnp.where(i == 0, 0, 1 + jnp.maximum(i - first_tail, 0)), 0, 0))
    const = lambda shape: pl.BlockSpec(shape, lambda bi, i: (0,) * len(shape))
    tab = pl.BlockSpec((tm, LANES), lambda bi, i: (i, 0))
    pair = lambda w: pl.BlockSpec((1, N_PAIRS, tm, w), lambda bi, i: (bi, 0, i, 0))
    row = lambda w: pl.BlockSpec((1, tm, w), lambda bi, i: (bi, i, 0))
    sds = jax.ShapeDtypeStruct
    out_shape = (
        sds((b, N_PAIRS, tp, 256), BF16), sds((b, N_PAIRS, tp, 256), BF16), sds((b, N_PAIRS, tp, 128), BF16),
        sds((b, N_PAIRS, tp, 128), BF16), sds((b, N_PAIRS, tp, 128), BF16), sds((b, N_PAIRS, tp, 128), BF16),
        sds((b, N_PAIRS, tp, 128), BF16), sds((b, tp, 128), BF16), sds((b, tp, 128), F32))
    return pl.pallas_call(
        functools.partial(_proj_kernel, first_tail=first_tail),
        grid=(b, nt),
        in_specs=[x_rows, edge, const((1, d)), const(win.shape), const((1, MLA_Q_RANK)), const(wuq.shape),
                  const((1, MLA_KV_RANK)), const(wukv.shape), tab, tab, tab, tab],
        out_specs=(pair(256), pair(256), pair(128), pair(128), pair(128), pair(128), pair(128),
                   row(128), row(128)),
        out_shape=out_shape,
        compiler_params=pltpu.CompilerParams(
            dimension_semantics=("arbitrary", "arbitrary"), vmem_limit_bytes=VMEM_LIMIT),
        interpret=interpret,
        name="proj",
    )(x, edges, g, win, gq, wuq, gkv, wukv, *tables)


_NT = (((1,), (1,)), ((), ()))
N_STREAMS = 2


def _srl(x, n):
    return lax.shift_right_logical(x, jnp.full(x.shape, n, x.dtype))


def _chunk_counts(i, tq):
    n_chunks = ((i + 1) * tq + TK - 1) // TK
    return n_chunks, (n_chunks + 1) // 2


class _Stream:
    def __init__(self, g, qm, s, p, a, m, l, acc):
        self.qm = qm.at[g]
        self.s = (s.at[0, g], s.at[1, g])
        self.p = (p.at[0, g], p.at[1, g])
        self.a = (a.at[0, g], a.at[1, g])
        self.m, self.l, self.acc = m.at[g], l.at[g], acc.at[g]

    def init(self):
        self.m[...] = jnp.full(self.m.shape, NEG_BIG, F32)
        self.l[...] = jnp.zeros(self.l.shape, F32)
        self.acc[...] = jnp.zeros(self.acc.shape, F32)
        self.p[1][...] = jnp.zeros(self.p[1].shape, BF16)
        self.a[1][...] = jnp.ones(self.a[1].shape, F32)

    def scores(self, k, slot):
        self.s[slot][...] = lax.dot_general(self.qm[...], k, _NT, preferred_element_type=F32)

    def softmax(self, s, slot):
        m_prev = self.m[...]
        m_new = jnp.maximum(m_prev, jnp.max(s, axis=-1, keepdims=True))
        alpha = jnp.exp(m_prev - m_new)
        p = jnp.exp(s - jnp.concatenate([m_new] * (s.shape[1] // LANES), axis=1))
        self.l[...] = alpha * self.l[...] + jnp.sum(p, axis=-1, keepdims=True)
        self.m[...] = m_new
        self.p[slot][...] = p.astype(BF16)
        self.a[slot][...] = alpha

    def pv(self, v, slot):
        self.acc[...] = self.a[slot][...] * self.acc[...] + jnp.dot(self.p[slot][...], v, preferred_element_type=F32)

    def finish(self, tq):
        lane = lax.broadcasted_iota(I32, (tq, LANES), 1)
        oa = self.acc[0:tq] / self.l[0:tq]
        ob = self.acc[tq:2 * tq] / self.l[tq:2 * tq]
        return jnp.where(lane < 64, oa, ob)


def _stream_scratch(tq, width):
    g = N_STREAMS
    return [pltpu.VMEM((g, 2 * tq, width), BF16),
            pltpu.VMEM((2, g, 2 * tq, TK), F32), pltpu.VMEM((2, g, 2 * tq, TK), BF16),
            pltpu.VMEM((2, g, 2 * tq, LANES), F32),
            pltpu.VMEM((g, 2 * tq, LANES), F32), pltpu.VMEM((g, 2 * tq, LANES), F32),
            pltpu.VMEM((g, 2 * tq, LANES), F32)]


def _pair_body(streams, c, prep, last):
    for st, _, v_at in streams:
        st.pv(v_at(c - 1), 1)
    for st, _, _ in streams:
        st.softmax(prep(st.s[0], c), 0)
    for st, k_at, _ in streams:
        st.scores(k_at(c + 1), 1)
    for st, _, v_at in streams:
        st.pv(v_at(c), 0)
    for st, _, _ in streams:
        st.softmax(prep(st.s[1], c + 1), 1)
    for st, k_at, v_at in streams:
        if last:
            st.pv(v_at(c + 1), 1)
        else:
            st.scores(k_at(c + 2), 0)


def _mla_kernel(q_ref, k_ref, v_ref, o_ref, qm, s, p, a, m, l, acc, *, tq, n_q_blocks):
    i = pl.program_id(2)

    @pl.when(i >= n_q_blocks)
    def _pad_rows():
        o_ref[...] = jnp.zeros(o_ref.shape, o_ref.dtype)

    @pl.when(i < n_q_blocks)
    def _attend():
        lane = lax.broadcasted_iota(I32, (tq, 256), 1)
        in_a = (lane < 64) | ((lane >= 128) & (lane < 144)) | ((lane >= 192) & (lane < 208))
        in_b = ((lane >= 64) & (lane < 128)) | ((lane >= 144) & (lane < 160)) | ((lane >= 208) & (lane < 224))
        n_chunks, n_pairs = _chunk_counts(i, tq)

        def rows(c):
            cc = jnp.clip(c, 0, n_chunks - 1)
            return pl.ds(pl.multiple_of(cc * TK, TK), TK)

        streams = []
        for g in range(N_STREAMS):
            st = _Stream(g, qm, s, p, a, m, l, acc)
            q = q_ref[0, g]
            zero = jnp.zeros_like(q)
            st.qm[0:tq] = jnp.where(in_a, q, zero)
            st.qm[tq:2 * tq] = jnp.where(in_b, q, zero)
            st.init()
            streams.append((st, lambda c, g=g: k_ref[0, g, rows(c), :], lambda c, g=g: v_ref[0, g, rows(c), :]))

        def masked(s_ref, c):
            qry = (lax.broadcasted_iota(I32, (2 * tq, TK), 0) & (tq - 1)) + i * tq
            key = lax.broadcasted_iota(I32, (2 * tq, TK), 1) + c * TK
            return jnp.where(key <= qry, s_ref[...], NEG_BIG)

        for st, k_at, _ in streams:
            st.scores(k_at(0), 0)

        def pair(j, carry):
            _pair_body(streams, 2 * j, lambda s_ref, c: s_ref[...], last=False)
            return carry

        lax.fori_loop(0, n_pairs - 1, pair, 0)
        _pair_body(streams, 2 * (n_pairs - 1), masked, last=True)
        for g, (st, _, _) in enumerate(streams):
            o_ref[0, :, g * LANES:(g + 1) * LANES] = st.finish(tq).astype(o_ref.dtype)


def _mla_call(qmla, kmla, vmla, *, tq, n_q_blocks, interpret=False):
    b, _, tp, _ = qmla.shape
    g = N_STREAMS
    return pl.pallas_call(
        functools.partial(_mla_kernel, tq=tq, n_q_blocks=n_q_blocks),
        grid=(b, N_PAIRS // g, tp // tq),
        in_specs=[pl.BlockSpec((1, g, tq, 256), lambda bi, pg, i: (bi, pg, i, 0)),
                  pl.BlockSpec((1, g, tp, 256), lambda bi, pg, i: (bi, pg, 0, 0)),
                  pl.BlockSpec((1, g, tp, 128), lambda bi, pg, i: (bi, pg, 0, 0))],
        out_specs=pl.BlockSpec((1, tq, g * LANES), lambda bi, pg, i: (bi, i, pg)),
        out_shape=jax.ShapeDtypeStruct((b, tp, N_PAIRS * LANES), BF16),
        scratch_shapes=_stream_scratch(tq, 256),
        compiler_params=pltpu.CompilerParams(
            dimension_semantics=("arbitrary", "arbitrary", "arbitrary"), vmem_limit_bytes=VMEM_LIMIT),
        interpret=interpret,
        name="mla_attn",
    )(qmla, kmla, vmla)


def _dsa_kernel(qi_ref, wi_ref, qs_ref, ki_ref, ks_ref, vs_ref, o_ref,
                key_scr, fld_scr, qim_scr, wt_scr,
                qm, s, p, a, m, l, acc, *, tq, topk, n_q_blocks):
    i = pl.program_id(1)
    pg = pl.program_id(2)
    n_chunks, n_pairs = _chunk_counts(i, tq)
    n_tiles = n_chunks * (TK // HK)
    slabs = HK // SUBLANES
    lane = lax.broadcasted_iota(I32, (tq, LANES), 1)
    in_a = (lane < 32) | ((lane >= 64) & (lane < 96))

    @pl.when(i >= n_q_blocks)
    def _pad_rows():
        o_ref[...] = jnp.zeros(o_ref.shape, o_ref.dtype)

    @pl.when((i < n_q_blocks) & (pg == 0))
    def _select():
        for pp in range(N_PAIRS):
            qp = qi_ref[0, pp].astype(F32)
            zero = jnp.zeros_like(qp)
            qim_scr[:, (2 * pp) * tq:(2 * pp + 1) * tq] = jnp.where(in_a, qp, zero).T.astype(BF16)
            qim_scr[:, (2 * pp + 1) * tq:(2 * pp + 2) * tq] = jnp.where(in_a, zero, qp).T.astype(BF16)
        wt_scr[...] = wi_ref[0].T

        def logits(t):
            kc = ki_ref[0, pl.ds(pl.multiple_of(t * HK, HK), HK), :]
            return jnp.dot(kc, qim_scr[...], preferred_element_type=F32)

        def to_keys(lg, t, diagonal):
            sc = jnp.zeros((HK, tq), F32)
            for hh in range(IDX_HEADS):
                sc = sc + wt_scr[hh:hh + 1, :] * jnp.maximum(lg[:, hh * tq:(hh + 1) * tq], 0.0)
            bits = pltpu.bitcast(sc, I32)
            key = bits ^ ((bits >> 31) & 0x7FFFFFFF)
            key = key - (key >> 31)
            if diagonal:
                kpos = lax.broadcasted_iota(I32, (HK, tq), 0) + t * HK
                qpos = lax.broadcasted_iota(I32, (HK, tq), 1) + i * tq
                key = jnp.where(kpos <= qpos, key, INT_MIN)
            key_scr[t] = key
            return key

        def pack(fa, fb):
            return (fa << 16) | fb | FIELD_GUARDS

        def score_chunk(c, diagonal):
            lg0 = logits(2 * c)
            lg1 = logits(2 * c + 1)
            k0 = to_keys(lg0, 2 * c, diagonal)
            k1 = to_keys(lg1, 2 * c + 1, diagonal)
            top = lambda key: _srl(key ^ INT_MIN, 17)
            fld_scr[c] = pack(top(k0), top(k1))

        n_below = (i * tq) // TK

        def below(c, carry):
            score_chunk(c, False)
            return carry

        def across(c, carry):
            score_chunk(c, True)
            return carry

        lax.fori_loop(0, n_below, below, 0)
        lax.fori_loop(n_below, n_chunks, across, 0)

        def count_fields(cand):
            both = ((cand << 16) | cand)[None]

            def body(c, acc):
                w = fld_scr[c].reshape(slabs, SUBLANES, tq)
                hit = _srl(w - both, 15) & 0x00010001
                return acc + jnp.sum(hit, axis=0, dtype=I32)
            acc = lax.fori_loop(0, n_chunks, body, jnp.zeros((SUBLANES, tq), I32))
            acc = (acc & 0xFFFF) + _srl(acc, 16)
            return jnp.broadcast_to(jnp.sum(acc, axis=0, keepdims=True, dtype=I32), (SUBLANES, tq))

        def field_search(above, n_ge):
            def step(it, carry):
                t, n = carry
                cand = t | lax.shift_left(jnp.int32(1), lax.convert_element_type(14 - it, I32))
                cnt = count_fields(cand) + above
                return jnp.where(cnt >= topk, cand, t), jnp.where(cnt >= topk, cnt, n)
            return lax.fori_loop(0, 15, step, (jnp.zeros((SUBLANES, tq), I32), n_ge))

        zeros = jnp.zeros((SUBLANES, tq), I32)
        t_top, n_ge = field_search(zeros, jnp.full((SUBLANES, tq), 2 ** 30, I32))
        above = count_fields(t_top + 1)

        def mid_fields(c, carry):
            def mid(key):
                u = (key ^ INT_MIN).reshape(slabs, SUBLANES, tq)
                alive = _srl(u, 17) == t_top[None]
                return jnp.where(alive, _srl(u, 2) & 0x7FFF, 0).reshape(HK, tq)
            fld_scr[c] = pack(mid(key_scr[2 * c]), mid(key_scr[2 * c + 1]))
            return carry

        lax.fori_loop(0, n_chunks, mid_fields, 0)
        t_mid, n_ge = field_search(above, n_ge)

        def count(pred):
            def body(c, acc):
                for u in range(TK // HK):
                    t = c * (TK // HK) + u
                    kk = key_scr[t].reshape(slabs, SUBLANES, tq)
                    acc = acc + jnp.sum(pred(kk, t).astype(I32), axis=0, dtype=I32)
                return acc
            acc = lax.fori_loop(0, n_chunks, body, jnp.zeros((SUBLANES, tq), I32))
            return jnp.broadcast_to(jnp.sum(acc, axis=0, keepdims=True, dtype=I32), (SUBLANES, tq))

        def bit_step(it, carry):
            t, n = carry
            cand = t | lax.shift_left(jnp.int32(1), lax.convert_element_type(1 - it, I32))
            cnt = count(lambda kk, _: kk >= (cand ^ INT_MIN)[None])
            return jnp.where(cnt >= topk, cand, t), jnp.where(cnt >= topk, cnt, n)

        thr_u, n_ge = lax.fori_loop(0, 2, bit_step, ((t_top << 17) | (t_mid << 2), n_ge))
        thr = thr_u ^ INT_MIN
        excess = jnp.max(jnp.where((n_ge > topk) & (thr > INT_MIN), 1, 0))

        def store_bias(t, sel):
            bias = jnp.where(sel, 0.0, NEG_BIG).astype(F32).reshape(HK, tq)
            key_scr[t] = pltpu.bitcast(bias.T, I32)

        @pl.when(excess == 0)
        def _no_ties():
            floor = jnp.maximum(thr, INT_MIN + 1)[None]

            def to_bias(t, carry):
                store_bias(t, key_scr[t].reshape(slabs, SUBLANES, tq) >= floor)
                return carry

            lax.fori_loop(0, n_tiles, to_bias, 0)

        @pl.when(excess > 0)
        def _ties():
            kidx = (lax.broadcasted_iota(I32, (slabs, SUBLANES, tq), 0) * SUBLANES
                    + lax.broadcasted_iota(I32, (slabs, SUBLANES, tq), 1))
            need = topk - count(lambda kk, _: kk > thr[None])

            def idx_step(it, x):
                cand = x + lax.shift_left(jnp.int32(1), lax.convert_element_type(13 - it, I32))
                cnt = count(lambda kk, t: (kk == thr[None]) & ((kidx + t * HK) < cand[None]))
                return jnp.where(cnt < need, cand, x)

            cut = lax.fori_loop(0, 14, idx_step, jnp.zeros((SUBLANES, tq), I32))
            cut = jnp.where(n_ge > topk, cut, 2 ** 30)[None]
            th = thr[None]

            def to_bias(t, carry):
                kk = key_scr[t].reshape(slabs, SUBLANES, tq)
                store_bias(t, (kk > th) | ((kk == th) & ((kidx + t * HK) <= cut) & (th > INT_MIN)))
                return carry

            lax.fori_loop(0, n_tiles, to_bias, 0)

        def fill(t, carry):
            key_scr[t] = pltpu.bitcast(jnp.full((HK, tq), NEG_BIG, F32), I32)
            return carry

        lax.fori_loop(n_tiles, 2 * n_pairs * (TK // HK), fill, 0)

    @pl.when(i < n_q_blocks)
    def _attend():
        def rows(c):
            cc = jnp.clip(c, 0, n_chunks - 1)
            return pl.ds(pl.multiple_of(cc * TK, TK), TK)

        streams = []
        for g in range(N_STREAMS):
            st = _Stream(g, qm, s, p, a, m, l, acc)
            pr = pg * N_STREAMS + g
            q = qs_ref[0, pr]
            zero = jnp.zeros_like(q)
            st.qm[0:tq] = jnp.where(in_a, q, zero)
            st.qm[tq:2 * tq] = jnp.where(in_a, zero, q)
            st.init()
            streams.append((st, lambda c, pr=pr: ks_ref[0, pr, rows(c), :], lambda c, pr=pr: vs_ref[0, pr, rows(c), :]))

        def biased(s_ref, c):
            bias = jnp.concatenate(
                [pltpu.bitcast(key_scr[c * (TK // HK) + t], F32) for t in range(TK // HK)], axis=1)
            return s_ref[...] + jnp.concatenate([bias, bias], axis=0)

        for st, k_at, _ in streams:
            st.scores(k_at(0), 0)

        def pair(j, carry):
            _pair_body(streams, 2 * j, biased, last=False)
            return carry

        lax.fori_loop(0, n_pairs, pair, 0)
        for g, (st, _, v_at) in enumerate(streams):
            st.pv(v_at(2 * n_pairs - 1), 1)
            o_ref[0, :, g * LANES:(g + 1) * LANES] = st.finish(tq).astype(o_ref.dtype)


def _dsa_call(qi, wi, qs, ki, ks, vs, *, tq, topk, n_q_blocks, interpret=False):
    b, _, tp, _ = qs.shape
    assert HK == tq, "mask tiles are transposed in place"
    once = pl.Buffered(1)
    max_pairs = (tp // TK + 1) // 2
    return pl.pallas_call(
        functools.partial(_dsa_kernel, tq=tq, topk=topk, n_q_blocks=n_q_blocks),
        grid=(b, tp // tq, N_PAIRS // N_STREAMS),
        in_specs=[pl.BlockSpec((1, N_PAIRS, tq, LANES), lambda bi, i, p: (bi, 0, i, 0)),
                  pl.BlockSpec((1, tq, LANES), lambda bi, i, p: (bi, i, 0)),
                  pl.BlockSpec((1, N_PAIRS, tq, LANES), lambda bi, i, p: (bi, 0, i, 0)),
                  pl.BlockSpec((1, tp, LANES), lambda bi, i, p: (bi, 0, 0), pipeline_mode=once),
                  pl.BlockSpec((1, N_PAIRS, tp, LANES), lambda bi, i, p: (bi, 0, 0, 0), pipeline_mode=once),
                  pl.BlockSpec((1, N_PAIRS, tp, LANES), lambda bi, i, p: (bi, 0, 0, 0), pipeline_mode=once)],
        out_specs=pl.BlockSpec((1, tq, N_STREAMS * LANES), lambda bi, i, p: (bi, i, p)),
        out_shape=jax.ShapeDtypeStruct((b, tp, N_PAIRS * LANES), BF16),
        scratch_shapes=[pltpu.VMEM((2 * max_pairs * (TK // HK), HK, tq), I32),
                        pltpu.VMEM((tp // TK, HK, tq), I32),
                        pltpu.VMEM((LANES, IDX_HEADS * tq), BF16),
                        pltpu.VMEM((LANES, tq), F32)]
        + _stream_scratch(tq, LANES),
        compiler_params=pltpu.CompilerParams(
            dimension_semantics=("arbitrary", "arbitrary", "arbitrary"), vmem_limit_bytes=VMEM_LIMIT),
        interpret=interpret,
        name="dsa_attn",
    )(qi, wi, qs, ki, ks, vs)


def _ffn_kernel(x_ref, oa_ref, ob_ref, wo_ref, g2_ref, wg_ref, wu_ref, wd_ref, gf_ref, out_ref):
    o = jnp.concatenate([oa_ref[0], ob_ref[0]], axis=1)
    h1 = x_ref[0] + jnp.dot(o, wo_ref[...], preferred_element_type=F32)
    u = _rms(h1, g2_ref[...]).astype(BF16)
    gate = jnp.dot(u, wg_ref[...], preferred_element_type=F32)
    up = jnp.dot(u, wu_ref[...], preferred_element_type=F32)
    act = (gate * jax.nn.sigmoid(gate) * up).astype(BF16)
    h2 = h1 + jnp.dot(act, wd_ref[...], preferred_element_type=F32)
    out_ref[0] = _rms(h2, gf_ref[...])


def _ffn_call(x, oa, ob, wo, g2, wg, wu, wd, gf, *, row0, n_tiles, interpret=False):
    b, _, d = x.shape
    tm = TM
    assert row0 % 16 == 0 and row0 + n_tiles * tm <= oa.shape[1]
    once = pl.Buffered(1)
    const = lambda shape: pl.BlockSpec(shape, lambda bi, i: (0,) * len(shape), pipeline_mode=once)
    rows_in = lambda w: pl.BlockSpec((pl.Element(1), pl.Element(tm), pl.Element(w)),
                                     lambda bi, i: (bi, pl.multiple_of(row0 + i * tm, 16), 0))
    return pl.pallas_call(
        _ffn_kernel,
        grid=(b, n_tiles),
        in_specs=[pl.BlockSpec((1, tm, d), lambda bi, i: (bi, i, 0)),
                  rows_in(oa.shape[-1]), rows_in(ob.shape[-1]), const(wo.shape), const((1, d)),
                  const(wg.shape), const(wu.shape), const(wd.shape), const((1, d))],
        out_specs=pl.BlockSpec((1, tm, d), lambda bi, i: (bi, i, 0)),
        out_shape=jax.ShapeDtypeStruct((b, n_tiles * tm, d), F32),
        compiler_params=pltpu.CompilerParams(
            dimension_semantics=("arbitrary", "arbitrary"), vmem_limit_bytes=VMEM_LIMIT),
        interpret=interpret,
        name="ffn",
    )(x, oa, ob, wo, g2, wg, wu, wd, gf)


def _layer(x, meta, attn_norm_g, w_in, mla_q_norm_g, w_uq, mla_kv_norm_g, w_ukv, w_o, ffn_norm_g,
           w_gate, w_up, w_down, out_g, topk, *, interpret=False):
    b, seq, d = x.shape
    t = N_META + seq
    n_out_tiles = -(-seq // TM)
    tp = -(-max(t, N_META + n_out_tiles * TM) // TK) * TK
    first_tail = t // TM
    tail = jnp.concatenate([x[:, first_tail * TM - N_META:], jnp.zeros((b, tp - t, d), x.dtype)], axis=1)
    head = jnp.concatenate([jnp.broadcast_to(meta[None].astype(x.dtype), (b, N_META, d)), x[:, :TM - N_META]], axis=1)
    edges = jnp.concatenate([head[:, None], tail.reshape(b, tp // TM - first_tail, TM, d)], axis=1)
    n_q_blocks = -(-t // TQ)
    win = _take_cols(w_in, _w_in_cols()).astype(BF16)
    wuq = _take_cols(w_uq, _w_uq_cols()).astype(BF16)
    wukv = jnp.take(w_ukv, jnp.asarray(_w_ukv_cols()), axis=1).astype(BF16)
    tables = _rope_tables(tp)
    qmla, kmla, vmla, qs, ks, vs, qi, ki, wi = _proj_call(
        x, edges, attn_norm_g[None], win, mla_q_norm_g[None], wuq, mla_kv_norm_g[None], wukv, tables,
        tp=tp, first_tail=first_tail, interpret=interpret)
    o_mla = _mla_call(qmla, kmla, vmla, tq=TQ, n_q_blocks=n_q_blocks, interpret=interpret)
    o_dsa = _dsa_call(qi, wi, qs, ki, ks, vs, tq=TQ, topk=topk, n_q_blocks=n_q_blocks, interpret=interpret)
    return _ffn_call(x, o_mla, o_dsa, w_o.astype(BF16), ffn_norm_g[None], w_gate.astype(BF16),
                     w_up.astype(BF16), w_down.astype(BF16), out_g[None], row0=N_META, n_tiles=n_out_tiles,
                     interpret=interpret)


def kernel(x, meta_tokens, attn_norm_g, w_in, mla_q_norm_g, w_uq, mla_kv_norm_g, w_ukv, w_o,
           ffn_norm_g, w_gate, w_up, w_down, final_norm_g):
    seq = x.shape[1]
    depth = w_in.shape[0]
    assert depth == 1, "the final norm is fused into the layer's last kernel"
    topk = min(TOPK_MAX, seq // 4)
    out = _layer(x, meta_tokens, attn_norm_g[0], w_in[0], mla_q_norm_g[0], w_uq[0], mla_kv_norm_g[0], w_ukv[0],
                 w_o[0], ffn_norm_g[0], w_gate[0], w_up[0], w_down[0], final_norm_g, topk)
    return out[:, :seq]
```

```python
import functools

import numpy as np
import jax
import jax.numpy as jnp
from jax import lax
from jax.experimental import pallas as pl
from jax.experimental.pallas import tpu as pltpu

F32 = jnp.float32
BF16 = jnp.bfloat16
I32 = jnp.int32

D_MODEL = 1024
N_META = 16
ROPE_THETA = 10000.0
EPS = 1e-6
MLA_HEADS = 8
MLA_Q_RANK = 256
MLA_KV_RANK = 128
MLA_NOPE_DIM = 64
MLA_ROPE_DIM = 32
MLA_QK_DIM = MLA_NOPE_DIM + MLA_ROPE_DIM
MLA_V_DIM = 64
DSA_HEADS = 8
DSA_HEAD_DIM = 64
IDX_HEADS = 8
IDX_DIM = 64
TOPK_MAX = 256
D_FF = 2816

LANES = 128
SUBLANES = 8
N_PAIRS = 4
NEG_BIG = -1e30
INT_MIN = -2 ** 31
FIELD_GUARDS = 0x8000 - 2 ** 31
VMEM_LIMIT = 58 * 1024 * 1024
TQ = 256
TK = 512
TM = 512
HK = TK // 2

_OFF_CQ, _OFF_CKV, _OFF_KR, _OFF_QS, _OFF_KS, _OFF_VS, _OFF_QI, _OFF_KI, _OFF_WI = (
    0, 256, 384, 416, 928, 1440, 1952, 2464, 2528)
_C_IN = 2536
_P_CQ, _P_CKV, _P_QS, _P_KS, _P_VS, _P_QI, _P_KI, _P_KR, _P_WI, _P_END = (
    0, 256, 384, 896, 1408, 1920, 2432, 2560, 2688, 2816)


def _pair_rope_cols(base, hd=64):
    half = hd // 2
    cols = []
    for p in range(N_PAIRS):
        a, b = 2 * p, 2 * p + 1
        cols += [base + a * hd + d for d in range(half)]
        cols += [base + b * hd + d for d in range(half)]
        cols += [base + a * hd + half + d for d in range(half)]
        cols += [base + b * hd + half + d for d in range(half)]
    return cols


def _w_in_cols():
    z = _C_IN
    cols = list(range(_OFF_CQ, _OFF_CQ + 256)) + list(range(_OFF_CKV, _OFF_CKV + 128))
    cols += _pair_rope_cols(_OFF_QS) + _pair_rope_cols(_OFF_KS)
    cols += list(range(_OFF_VS, _OFF_VS + 512))
    cols += _pair_rope_cols(_OFF_QI)
    ki1 = [_OFF_KI + d for d in range(32)]
    ki2 = [_OFF_KI + 32 + d for d in range(32)]
    cols += ki1 + ki1 + ki2 + ki2
    kr1 = [_OFF_KR + d for d in range(16)]
    kr2 = [_OFF_KR + 16 + d for d in range(16)]
    cols += kr1 + kr1 + [z] * 32 + kr2 + kr2 + [z] * 32
    cols += [_OFF_WI + d for d in range(IDX_HEADS)] + [z] * (LANES - IDX_HEADS)
    assert len(cols) == _P_END
    return np.asarray(cols, np.int32)


def _w_uq_cols():
    z = MLA_HEADS * MLA_QK_DIM
    cols = []
    for p in range(N_PAIRS):
        a, b = 2 * p, 2 * p + 1
        cols += [a * MLA_QK_DIM + d for d in range(64)] + [b * MLA_QK_DIM + d for d in range(64)]
        cols += [a * MLA_QK_DIM + 64 + d for d in range(16)] + [b * MLA_QK_DIM + 64 + d for d in range(16)]
        cols += [z] * 32
        cols += [a * MLA_QK_DIM + 80 + d for d in range(16)] + [b * MLA_QK_DIM + 80 + d for d in range(16)]
        cols += [z] * 32
    return np.asarray(cols, np.int32)


def _w_ukv_cols():
    cols = [h * 128 + d for h in range(MLA_HEADS) for d in range(64)]
    cols += [h * 128 + 64 + d for h in range(MLA_HEADS) for d in range(64)]
    return np.asarray(cols, np.int32)


def _take_cols(w, cols):
    wz = jnp.concatenate([w, jnp.zeros((w.shape[0], 1), w.dtype)], axis=1)
    return jnp.take(wz, jnp.asarray(cols), axis=1)


def _rope_tables(tp):
    pos = jnp.arange(tp, dtype=jnp.int32).astype(F32)

    def cs(half):
        inv = jnp.power(ROPE_THETA, -jnp.arange(half, dtype=F32) / half)
        ang = pos[:, None] * inv[None, :]
        return jnp.cos(ang), jnp.sin(ang)

    c32, s32 = cs(32)
    c16, s16 = cs(16)
    z = jnp.zeros((tp, 32), F32)
    c64t = jnp.concatenate([c32, c32, c32, c32], axis=1)
    s64t = jnp.concatenate([-s32, -s32, s32, s32], axis=1)
    c32t = jnp.concatenate([c16, c16, z, c16, c16, z], axis=1)
    s32t = jnp.concatenate([-s16, -s16, z, s16, s16, z], axis=1)
    return c64t, s64t, c32t, s32t


def _rms(x, g):
    return x * lax.rsqrt(jnp.mean(x * x, axis=-1, keepdims=True) + EPS) * g


def _rope_slab(x, c, s):
    return x * c + pltpu.roll(x, 64, axis=1) * s


def _proj_kernel(x_ref, edge_ref, g_ref, win_ref, gq_ref, wuq_ref, gkv_ref, wukv_ref,
                 c64_ref, s64_ref, c32_ref, s32_ref,
                 qmla_ref, kmla_ref, vmla_ref, qs_ref, ks_ref, vs_ref, qi_ref, ki_ref, wi_ref, *, first_tail):
    i = pl.program_id(1)
    h = jnp.where((i == 0) | (i >= first_tail), edge_ref[0, 0], x_ref[0])
    u = _rms(h, g_ref[...]).astype(BF16)
    proj = jnp.dot(u, win_ref[...], preferred_element_type=F32)
    c64, s64, c32, s32 = c64_ref[...], s64_ref[...], c32_ref[...], s32_ref[...]

    cq = _rms(proj[:, _P_CQ:_P_CQ + 256], gq_ref[...]).astype(BF16)
    q = jnp.dot(cq, wuq_ref[...], preferred_element_type=F32)
    ckv = _rms(proj[:, _P_CKV:_P_CKV + 128], gkv_ref[...]).astype(BF16)
    kv = jnp.dot(ckv, wukv_ref[...], preferred_element_type=F32)
    kr = _rope_slab(proj[:, _P_KR:_P_KR + 128], c32, s32).astype(BF16)
    q_scale = MLA_QK_DIM ** -0.5
    s_scale = DSA_HEAD_DIM ** -0.5
    i_scale = IDX_DIM ** -0.5
    for p in range(N_PAIRS):
        qn = q[:, p * 256:p * 256 + 128] * q_scale
        qr = _rope_slab(q[:, p * 256 + 128:p * 256 + 256], c32, s32) * q_scale
        qmla_ref[0, p, :, 0:128] = qn.astype(BF16)
        qmla_ref[0, p, :, 128:256] = qr.astype(BF16)
        kmla_ref[0, p, :, 0:128] = kv[:, p * 128:(p + 1) * 128].astype(BF16)
        kmla_ref[0, p, :, 128:256] = kr
        vmla_ref[0, p] = kv[:, 512 + p * 128:512 + (p + 1) * 128].astype(BF16)
        sl = slice(p * 128, (p + 1) * 128)
        qs_ref[0, p] = (_rope_slab(proj[:, _P_QS:_P_KS][:, sl], c64, s64) * s_scale).astype(BF16)
        ks_ref[0, p] = _rope_slab(proj[:, _P_KS:_P_VS][:, sl], c64, s64).astype(BF16)
        vs_ref[0, p] = proj[:, _P_VS:_P_QI][:, sl].astype(BF16)
        qi_ref[0, p] = (_rope_slab(proj[:, _P_QI:_P_KI][:, sl], c64, s64) * i_scale).astype(BF16)
    ki_ref[0] = _rope_slab(proj[:, _P_KI:_P_KR], c64, s64).astype(BF16)
    wi_ref[0] = proj[:, _P_WI:_P_END] * (IDX_HEADS ** -0.5)


def _proj_call(x, edges, g, win, gq, wuq, gkv, wukv, tables, *, tp, first_tail, interpret=False):
    b, seq, d = x.shape
    tm = TM
    nt = tp // tm
    assert seq >= tm and (seq - tm) % 16 == 0 and N_META % 16 == 0
    x_rows = pl.BlockSpec(
        (pl.Element(1), pl.Element(tm), pl.Element(d)),
        lambda bi, i: (bi, pl.multiple_of(jnp.clip(i * tm - N_META, 0, seq - tm), 16), 0))
    edge = pl.BlockSpec((1, 1, tm, d), lambda bi, i: (bi, jnp.where(i == 0, 0, 1 + jnp.maximum(i - first_tail, 0)), 0, 0))
    const = lambda shape: pl.BlockSpec(shape, lambda bi, i: (0,) * len(shape))
    tab = pl.BlockSpec((tm, LANES), lambda bi, i: (i, 0))
    pair = lambda w: pl.BlockSpec((1, N_PAIRS, tm, w), lambda bi, i: (bi, 0, i, 0))
    row = lambda w: pl.BlockSpec((1, tm, w), lambda bi, i: (bi, i, 0))
    sds = jax.ShapeDtypeStruct
    out_shape = (
        sds((b, N_PAIRS, tp, 256), BF16), sds((b, N_PAIRS, tp, 256), BF16), sds((b, N_PAIRS, tp, 128), BF16),
        sds((b, N_PAIRS, tp, 128), BF16), sds((b, N_PAIRS, tp, 128), BF16), sds((b, N_PAIRS, tp, 128), BF16),
        sds((b, N_PAIRS, tp, 128), BF16), sds((b, tp, 128), BF16), sds((b, tp, 128), F32))
    return pl.pallas_call(
        functools.partial(_proj_kernel, first_tail=first_tail),
        grid=(b, nt),
        in_specs=[x_rows, edge, const((1, d)), const(win.shape), const((1, MLA_Q_RANK)), const(wuq.shape),
                  const((1, MLA_KV_RANK)), const(wukv.shape), tab, tab, tab, tab],
        out_specs=(pair(256), pair(256), pair(128), pair(128), pair(128), pair(128), pair(128),
                   row(128), row(128)),
        out_shape=out_shape,
        compiler_params=pltpu.CompilerParams(
            dimension_semantics=("arbitrary", "arbitrary"), vmem_limit_bytes=VMEM_LIMIT),
        interpret=interpret,
        name="proj",
    )(x, edges, g, win, gq, wuq, gkv, wukv, *tables)


_NT = (((1,), (1,)), ((), ()))
N_STREAMS = 2


def _srl(x, n):
    return lax.shift_right_logical(x, jnp.full(x.shape, n, x.dtype))


def _chunk_counts(i, tq):
    n_chunks = ((i + 1) * tq + TK - 1) // TK
    return n_chunks, (n_chunks + 1) // 2


class _Stream:
    def __init__(self, g, qm, s, p, a, m, l, acc):
        self.qm = qm.at[g]
        self.s = (s.at[0, g], s.at[1, g])
        self.p = (p.at[0, g], p.at[1, g])
        self.a = (a.at[0, g], a.at[1, g])
        self.m, self.l, self.acc = m.at[g], l.at[g], acc.at[g]

    def init(self):
        self.m[...] = jnp.full(self.m.shape, NEG_BIG, F32)
        self.l[...] = jnp.zeros(self.l.shape, F32)
        self.acc[...] = jnp.zeros(self.acc.shape, F32)
        self.p[1][...] = jnp.zeros(self.p[1].shape, BF16)
        self.a[1][...] = jnp.ones(self.a[1].shape, F32)

    def scores(self, k, slot):
        self.s[slot][...] = lax.dot_general(self.qm[...], k, _NT, preferred_element_type=F32)

    def softmax(self, s, slot):
        m_prev = self.m[...]
        m_new = jnp.maximum(m_prev, jnp.max(s, axis=-1, keepdims=True))
        alpha = jnp.exp(m_prev - m_new)
        p = jnp.exp(s - jnp.concatenate([m_new] * (s.shape[1] // LANES), axis=1))
        self.l[...] = alpha * self.l[...] + jnp.sum(p, axis=-1, keepdims=True)
        self.m[...] = m_new
        self.p[slot][...] = p.astype(BF16)
        self.a[slot][...] = alpha

    def pv(self, v, slot):
        self.acc[...] = self.a[slot][...] * self.acc[...] + jnp.dot(self.p[slot][...], v, preferred_element_type=F32)

    def finish(self, tq):
        lane = lax.broadcasted_iota(I32, (tq, LANES), 1)
        oa = self.acc[0:tq] / self.l[0:tq]
        ob = self.acc[tq:2 * tq] / self.l[tq:2 * tq]
        return jnp.where(lane < 64, oa, ob)


def _stream_scratch(tq, width):
    g = N_STREAMS
    return [pltpu.VMEM((g, 2 * tq, width), BF16),
            pltpu.VMEM((2, g, 2 * tq, TK), F32), pltpu.VMEM((2, g, 2 * tq, TK), BF16),
            pltpu.VMEM((2, g, 2 * tq, LANES), F32),
            pltpu.VMEM((g, 2 * tq, LANES), F32), pltpu.VMEM((g, 2 * tq, LANES), F32),
            pltpu.VMEM((g, 2 * tq, LANES), F32)]


def _pair_body(streams, c, prep, last):
    for st, _, v_at in streams:
        st.pv(v_at(c - 1), 1)
    for st, _, _ in streams:
        st.softmax(prep(st.s[0], c), 0)
    for st, k_at, _ in streams:
        st.scores(k_at(c + 1), 1)
    for st, _, v_at in streams:
        st.pv(v_at(c), 0)
    for st, _, _ in streams:
        st.softmax(prep(st.s[1], c + 1), 1)
    for st, k_at, v_at in streams:
        if last:
            st.pv(v_at(c + 1), 1)
        else:
            st.scores(k_at(c + 2), 0)


def _mla_kernel(q_ref, k_ref, v_ref, o_ref, qm, s, p, a, m, l, acc, *, tq, n_q_blocks):
    i = pl.program_id(2)

    @pl.when(i >= n_q_blocks)
    def _pad_rows():
        o_ref[...] = jnp.zeros(o_ref.shape, o_ref.dtype)

    @pl.when(i < n_q_blocks)
    def _attend():
        lane = lax.broadcasted_iota(I32, (tq, 256), 1)
        in_a = (lane < 64) | ((lane >= 128) & (lane < 144)) | ((lane >= 192) & (lane < 208))
        in_b = ((lane >= 64) & (lane < 128)) | ((lane >= 144) & (lane < 160)) | ((lane >= 208) & (lane < 224))
        n_chunks, n_pairs = _chunk_counts(i, tq)

        def rows(c):
            cc = jnp.clip(c, 0, n_chunks - 1)
            return pl.ds(pl.multiple_of(cc * TK, TK), TK)

        streams = []
        for g in range(N_STREAMS):
            st = _Stream(g, qm, s, p, a, m, l, acc)
            q = q_ref[0, g]
            zero = jnp.zeros_like(q)
            st.qm[0:tq] = jnp.where(in_a, q, zero)
            st.qm[tq:2 * tq] = jnp.where(in_b, q, zero)
            st.init()
            streams.append((st, lambda c, g=g: k_ref[0, g, rows(c), :], lambda c, g=g: v_ref[0, g, rows(c), :]))

        def masked(s_ref, c):
            qry = (lax.broadcasted_iota(I32, (2 * tq, TK), 0) & (tq - 1)) + i * tq
            key = lax.broadcasted_iota(I32, (2 * tq, TK), 1) + c * TK
            return jnp.where(key <= qry, s_ref[...], NEG_BIG)

        for st, k_at, _ in streams:
            st.scores(k_at(0), 0)

        def pair(j, carry):
            _pair_body(streams, 2 * j, lambda s_ref, c: s_ref[...], last=False)
            return carry

        lax.fori_loop(0, n_pairs - 1, pair, 0)
        _pair_body(streams, 2 * (n_pairs - 1), masked, last=True)
        for g, (st, _, _) in enumerate(streams):
            o_ref[0, :, g * LANES:(g + 1) * LANES] = st.finish(tq).astype(o_ref.dtype)


def _mla_call(qmla, kmla, vmla, *, tq, n_q_blocks, interpret=False):
    b, _, tp, _ = qmla.shape
    g = N_STREAMS
    return pl.pallas_call(
        functools.partial(_mla_kernel, tq=tq, n_q_blocks=n_q_blocks),
        grid=(b, N_PAIRS // g, tp // tq),
        in_specs=[pl.BlockSpec((1, g, tq, 256), lambda bi, pg, i: (bi, pg, i, 0)),
                  pl.BlockSpec((1, g, tp, 256), lambda bi, pg, i: (bi, pg, 0, 0)),
                  pl.BlockSpec((1, g, tp, 128), lambda bi, pg, i: (bi, pg, 0, 0))],
        out_specs=pl.BlockSpec((1, tq, g * LANES), lambda bi, pg, i: (bi, i, pg)),
        out_shape=jax.ShapeDtypeStruct((b, tp, N_PAIRS * LANES), BF16),
        scratch_shapes=_stream_scratch(tq, 256),
        compiler_params=pltpu.CompilerParams(
            dimension_semantics=("arbitrary", "arbitrary", "arbitrary"), vmem_limit_bytes=VMEM_LIMIT),
        interpret=interpret,
        name="mla_attn",
    )(qmla, kmla, vmla)


def _dsa_kernel(qi_ref, wi_ref, qs_ref, ki_ref, ks_ref, vs_ref, o_ref,
                key_scr, fld_scr, qim_scr, wt_scr,
                qm, s, p, a, m, l, acc, *, tq, topk, n_q_blocks):
    i = pl.program_id(1)
    pg = pl.program_id(2)
    n_chunks, n_pairs = _chunk_counts(i, tq)
    n_tiles = n_chunks * (TK // HK)
    slabs = HK // SUBLANES
    lane = lax.broadcasted_iota(I32, (tq, LANES), 1)
    in_a = (lane < 32) | ((lane >= 64) & (lane < 96))

    @pl.when(i >= n_q_blocks)
    def _pad_rows():
        o_ref[...] = jnp.zeros(o_ref.shape, o_ref.dtype)

    @pl.when((i < n_q_blocks) & (pg == 0))
    def _select():
        for pp in range(N_PAIRS):
            qp = qi_ref[0, pp].astype(F32)
            zero = jnp.zeros_like(qp)
            qim_scr[:, (2 * pp) * tq:(2 * pp + 1) * tq] = jnp.where(in_a, qp, zero).T.astype(BF16)
            qim_scr[:, (2 * pp + 1) * tq:(2 * pp + 2) * tq] = jnp.where(in_a, zero, qp).T.astype(BF16)
        wt_scr[...] = wi_ref[0].T

        def logits(t):
            kc = ki_ref[0, pl.ds(pl.multiple_of(t * HK, HK), HK), :]
            return jnp.dot(kc, qim_scr[...], preferred_element_type=F32)

        def to_keys(lg, t, diagonal):
            sc = jnp.zeros((HK, tq), F32)
            for hh in range(IDX_HEADS):
                sc = sc + wt_scr[hh:hh + 1, :] * jnp.maximum(lg[:, hh * tq:(hh + 1) * tq], 0.0)
            bits = pltpu.bitcast(sc, I32)
            key = bits ^ ((bits >> 31) & 0x7FFFFFFF)
            key = key - (key >> 31)
            if diagonal:
                kpos = lax.broadcasted_iota(I32, (HK, tq), 0) + t * HK
                qpos = lax.broadcasted_iota(I32, (HK, tq), 1) + i * tq
                key = jnp.where(kpos <= qpos, key, INT_MIN)
            key_scr[t] = key
            return key

        def pack(fa, fb):
            return (fa << 16) | fb | FIELD_GUARDS

        def score_chunk(c, diagonal):
            lg0 = logits(2 * c)
            lg1 = logits(2 * c + 1)
            k0 = to_keys(lg0, 2 * c, diagonal)
            k1 = to_keys(lg1, 2 * c + 1, diagonal)
            top = lambda key: _srl(key ^ INT_MIN, 17)
            fld_scr[c] = pack(top(k0), top(k1))

        n_below = (i * tq) // TK

        def below(c, carry):
            score_chunk(c, False)
            return carry

        def across(c, carry):
            score_chunk(c, True)
            return carry

        lax.fori_loop(0, n_below, below, 0)
        lax.fori_loop(n_below, n_chunks, across, 0)

        def count_fields(cand):
            both = ((cand << 16) | cand)[None]

            def body(c, acc):
                w = fld_scr[c].reshape(slabs, SUBLANES, tq)
                hit = _srl(w - both, 15) & 0x00010001
                return acc + jnp.sum(hit, axis=0, dtype=I32)
            acc = lax.fori_loop(0, n_chunks, body, jnp.zeros((SUBLANES, tq), I32))
            acc = (acc & 0xFFFF) + _srl(acc, 16)
            return jnp.broadcast_to(jnp.sum(acc, axis=0, keepdims=True, dtype=I32), (SUBLANES, tq))

        def field_search(above, n_ge):
            def step(it, carry):
                t, n = carry
                cand = t | lax.shift_left(jnp.int32(1), lax.convert_element_type(14 - it, I32))
                cnt = count_fields(cand) + above
                return jnp.where(cnt >= topk, cand, t), jnp.where(cnt >= topk, cnt, n)
            return lax.fori_loop(0, 15, step, (jnp.zeros((SUBLANES, tq), I32), n_ge))

        zeros = jnp.zeros((SUBLANES, tq), I32)
        t_top, n_ge = field_search(zeros, jnp.full((SUBLANES, tq), 2 ** 30, I32))
        above = count_fields(t_top + 1)

        def mid_fields(c, carry):
            def mid(key):
                u = (key ^ INT_MIN).reshape(slabs, SUBLANES, tq)
                alive = _srl(u, 17) == t_top[None]
                return jnp.where(alive, _srl(u, 2) & 0x7FFF, 0).reshape(HK, tq)
            fld_scr[c] = pack(mid(key_scr[2 * c]), mid(key_scr[2 * c + 1]))
            return carry

        lax.fori_loop(0, n_chunks, mid_fields, 0)
        t_mid, n_ge = field_search(above, n_ge)

        def count(pred):
            def body(c, acc):
                for u in range(TK // HK):
                    t = c * (TK // HK) + u
                    kk = key_scr[t].reshape(slabs, SUBLANES, tq)
                    acc = acc + jnp.sum(pred(kk, t).astype(I32), axis=0, dtype=I32)
                return acc
            acc = lax.fori_loop(0, n_chunks, body, jnp.zeros((SUBLANES, tq), I32))
            return jnp.broadcast_to(jnp.sum(acc, axis=0, keepdims=True, dtype=I32), (SUBLANES, tq))

        def bit_step(it, carry):
            t, n = carry
            cand = t | lax.shift_left(jnp.int32(1), lax.convert_element_type(1 - it, I32))
            cnt = count(lambda kk, _: kk >= (cand ^ INT_MIN)[None])
            return jnp.where(cnt >= topk, cand, t), jnp.where(cnt >= topk, cnt, n)

        thr_u, n_ge = lax.fori_loop(0, 2, bit_step, ((t_top << 17) | (t_mid << 2), n_ge))
        thr = thr_u ^ INT_MIN
        excess = jnp.max(jnp.where((n_ge > topk) & (thr > INT_MIN), 1, 0))

        def store_bias(t, sel):
            bias = jnp.where(sel, 0.0, NEG_BIG).astype(F32).reshape(HK, tq)
            key_scr[t] = pltpu.bitcast(bias.T, I32)

        @pl.when(excess == 0)
        def _no_ties():
            floor = jnp.maximum(thr, INT_MIN + 1)[None]

            def to_bias(c, carry):
                for u in range(TK // HK):
                    t = c * (TK // HK) + u
                    store_bias(t, key_scr[t].reshape(slabs, SUBLANES, tq) >= floor)
                return carry

            lax.fori_loop(0, n_chunks, to_bias, 0)

        @pl.when(excess > 0)
        def _ties():
            kidx = (lax.broadcasted_iota(I32, (slabs, SUBLANES, tq), 0) * SUBLANES
                    + lax.broadcasted_iota(I32, (slabs, SUBLANES, tq), 1))
            need = topk - count(lambda kk, _: kk > thr[None])

            def idx_step(it, x):
                cand = x + lax.shift_left(jnp.int32(1), lax.convert_element_type(13 - it, I32))
                cnt = count(lambda kk, t: (kk == thr[None]) & ((kidx + t * HK) < cand[None]))
                return jnp.where(cnt < need, cand, x)

            cut = lax.fori_loop(0, 14, idx_step, jnp.zeros((SUBLANES, tq), I32))
            cut = jnp.where(n_ge > topk, cut, 2 ** 30)[None]
            th = thr[None]

            def to_bias(t, carry):
                kk = key_scr[t].reshape(slabs, SUBLANES, tq)
                store_bias(t, (kk > th) | ((kk == th) & ((kidx + t * HK) <= cut) & (th > INT_MIN)))
                return carry

            lax.fori_loop(0, n_tiles, to_bias, 0)

        def fill(t, carry):
            key_scr[t] = pltpu.bitcast(jnp.full((HK, tq), NEG_BIG, F32), I32)
            return carry

        lax.fori_loop(n_tiles, 2 * n_pairs * (TK // HK), fill, 0)

    @pl.when(i < n_q_blocks)
    def _attend():
        def rows(c):
            cc = jnp.clip(c, 0, n_chunks - 1)
            return pl.ds(pl.multiple_of(cc * TK, TK), TK)

        streams = []
        for g in range(N_STREAMS):
            st = _Stream(g, qm, s, p, a, m, l, acc)
            pr = pg * N_STREAMS + g
            q = qs_ref[0, pr]
            zero = jnp.zeros_like(q)
            st.qm[0:tq] = jnp.where(in_a, q, zero)
            st.qm[tq:2 * tq] = jnp.where(in_a, zero, q)
            st.init()
            streams.append((st, lambda c, pr=pr: ks_ref[0, pr, rows(c), :], lambda c, pr=pr: vs_ref[0, pr, rows(c), :]))

        def biased(s_ref, c):
            bias = jnp.concatenate(
                [pltpu.bitcast(key_scr[c * (TK // HK) + t], F32) for t in range(TK // HK)], axis=1)
            return s_ref[...] + jnp.concatenate([bias, bias], axis=0)

        for st, k_at, _ in streams:
            st.scores(k_at(0), 0)

        def pair(j, carry):
            _pair_body(streams, 2 * j, biased, last=False)
            return carry

        lax.fori_loop(0, n_pairs, pair, 0)
        for g, (st, _, v_at) in enumerate(streams):
            st.pv(v_at(2 * n_pairs - 1), 1)
            o_ref[0, :, g * LANES:(g + 1) * LANES] = st.finish(tq).astype(o_ref.dtype)


def _dsa_call(qi, wi, qs, ki, ks, vs, *, tq, topk, n_q_blocks, interpret=False):
    b, _, tp, _ = qs.shape
    assert HK == tq, "mask tiles are transposed in place"
    once = pl.Buffered(1)
    max_pairs = (tp // TK + 1) // 2
    return pl.pallas_call(
        functools.partial(_dsa_kernel, tq=tq, topk=topk, n_q_blocks=n_q_blocks),
        grid=(b, tp // tq, N_PAIRS // N_STREAMS),
        in_specs=[pl.BlockSpec((1, N_PAIRS, tq, LANES), lambda bi, i, p: (bi, 0, i, 0)),
                  pl.BlockSpec((1, tq, LANES), lambda bi, i, p: (bi, i, 0)),
                  pl.BlockSpec((1, N_PAIRS, tq, LANES), lambda bi, i, p: (bi, 0, i, 0)),
                  pl.BlockSpec((1, tp, LANES), lambda bi, i, p: (bi, 0, 0), pipeline_mode=once),
                  pl.BlockSpec((1, N_PAIRS, tp, LANES), lambda bi, i, p: (bi, 0, 0, 0), pipeline_mode=once),
                  pl.BlockSpec((1, N_PAIRS, tp, LANES), lambda bi, i, p: (bi, 0, 0, 0), pipeline_mode=once)],
        out_specs=pl.BlockSpec((1, tq, N_STREAMS * LANES), lambda bi, i, p: (bi, i, p)),
        out_shape=jax.ShapeDtypeStruct((b, tp, N_PAIRS * LANES), BF16),
        scratch_shapes=[pltpu.VMEM((2 * max_pairs * (TK // HK), HK, tq), I32),
                        pltpu.VMEM((tp // TK, HK, tq), I32),
                        pltpu.VMEM((LANES, IDX_HEADS * tq), BF16),
                        pltpu.VMEM((LANES, tq), F32)]
        + _stream_scratch(tq, LANES),
        compiler_params=pltpu.CompilerParams(
            dimension_semantics=("arbitrary", "arbitrary", "arbitrary"), vmem_limit_bytes=VMEM_LIMIT),
        interpret=interpret,
        name="dsa_attn",
    )(qi, wi, qs, ki, ks, vs)


def _ffn_kernel(x_ref, oa_ref, ob_ref, wo_ref, g2_ref, wg_ref, wu_ref, wd_ref, gf_ref, out_ref):
    o = jnp.concatenate([oa_ref[0], ob_ref[0]], axis=1)
    h1 = x_ref[0] + jnp.dot(o, wo_ref[...], preferred_element_type=F32)
    u = _rms(h1, g2_ref[...]).astype(BF16)
    gate = jnp.dot(u, wg_ref[...], preferred_element_type=F32)
    up = jnp.dot(u, wu_ref[...], preferred_element_type=F32)
    act = (gate * jax.nn.sigmoid(gate) * up).astype(BF16)
    h2 = h1 + jnp.dot(act, wd_ref[...], preferred_element_type=F32)
    out_ref[0] = _rms(h2, gf_ref[...])


def _ffn_call(x, oa, ob, wo, g2, wg, wu, wd, gf, *, row0, n_tiles, interpret=False):
    b, _, d = x.shape
    tm = TM
    assert row0 % 16 == 0 and row0 + n_tiles * tm <= oa.shape[1]
    once = pl.Buffered(1)
    const = lambda shape: pl.BlockSpec(shape, lambda bi, i: (0,) * len(shape), pipeline_mode=once)
    rows_in = lambda w: pl.BlockSpec((pl.Element(1), pl.Element(tm), pl.Element(w)),
                                     lambda bi, i: (bi, pl.multiple_of(row0 + i * tm, 16), 0))
    return pl.pallas_call(
        _ffn_kernel,
        grid=(b, n_tiles),
        in_specs=[pl.BlockSpec((1, tm, d), lambda bi, i: (bi, i, 0)),
                  rows_in(oa.shape[-1]), rows_in(ob.shape[-1]), const(wo.shape), const((1, d)),
                  const(wg.shape), const(wu.shape), const(wd.shape), const((1, d))],
        out_specs=pl.BlockSpec((1, tm, d), lambda bi, i: (bi, i, 0)),
        out_shape=jax.ShapeDtypeStruct((b, n_tiles * tm, d), F32),
        compiler_params=pltpu.CompilerParams(
            dimension_semantics=("arbitrary", "arbitrary"), vmem_limit_bytes=VMEM_LIMIT),
        interpret=interpret,
        name="ffn",
    )(x, oa, ob, wo, g2, wg, wu, wd, gf)


def _layer(x, meta, attn_norm_g, w_in, mla_q_norm_g, w_uq, mla_kv_norm_g, w_ukv, w_o, ffn_norm_g,
           w_gate, w_up, w_down, out_g, topk, *, interpret=False):
    b, seq, d = x.shape
    t = N_META + seq
    n_out_tiles = -(-seq // TM)
    tp = -(-max(t, N_META + n_out_tiles * TM) // TK) * TK
    first_tail = t // TM
    tail = jnp.concatenate([x[:, first_tail * TM - N_META:], jnp.zeros((b, tp - t, d), x.dtype)], axis=1)
    head = jnp.concatenate([jnp.broadcast_to(meta[None].astype(x.dtype), (b, N_META, d)), x[:, :TM - N_META]], axis=1)
    edges = jnp.concatenate([head[:, None], tail.reshape(b, tp // TM - first_tail, TM, d)], axis=1)
    n_q_blocks = -(-t // TQ)
    win = _take_cols(w_in, _w_in_cols()).astype(BF16)
    wuq = _take_cols(w_uq, _w_uq_cols()).astype(BF16)
    wukv = jnp.take(w_ukv, jnp.asarray(_w_ukv_cols()), axis=1).astype(BF16)
    tables = _rope_tables(tp)
    qmla, kmla, vmla, qs, ks, vs, qi, ki, wi = _proj_call(
        x, edges, attn_norm_g[None], win, mla_q_norm_g[None], wuq, mla_kv_norm_g[None], wukv, tables,
        tp=tp, first_tail=first_tail, interpret=interpret)
    o_mla = _mla_call(qmla, kmla, vmla, tq=TQ, n_q_blocks=n_q_blocks, interpret=interpret)
    o_dsa = _dsa_call(qi, wi, qs, ki, ks, vs, tq=TQ, topk=topk, n_q_blocks=n_q_blocks, interpret=interpret)
    return _ffn_call(x, o_mla, o_dsa, w_o.astype(BF16), ffn_norm_g[None], w_gate.astype(BF16),
                     w_up.astype(BF16), w_down.astype(BF16), out_g[None], row0=N_META, n_tiles=n_out_tiles,
                     interpret=interpret)


def kernel(x, meta_tokens, attn_norm_g, w_in, mla_q_norm_g, w_uq, mla_kv_norm_g, w_ukv, w_o,
           ffn_norm_g, w_gate, w_up, w_down, final_norm_g):
    seq = x.shape[1]
    depth = w_in.shape[0]
    assert depth == 1, "the final norm is fused into the layer's last kernel"
    topk = min(TOPK_MAX, seq // 4)
    out = _layer(x, meta_tokens, attn_norm_g[0], w_in[0], mla_q_norm_g[0], w_uq[0], mla_kv_norm_g[0], w_ukv[0],
                 w_o[0], ffn_norm_g[0], w_gate[0], w_up[0], w_down[0], final_norm_g, topk)
    return out[:, :seq]
```

```python
import functools

import numpy as np
import jax
import jax.numpy as jnp
from jax import lax
from jax.experimental import pallas as pl
from jax.experimental.pallas import tpu as pltpu

F32 = jnp.float32
BF16 = jnp.bfloat16
I32 = jnp.int32

D_MODEL = 1024
N_META = 16
ROPE_THETA = 10000.0
EPS = 1e-6
MLA_HEADS = 8
MLA_Q_RANK = 256
MLA_KV_RANK = 128
MLA_NOPE_DIM = 64
MLA_ROPE_DIM = 32
MLA_QK_DIM = MLA_NOPE_DIM + MLA_ROPE_DIM
MLA_V_DIM = 64
DSA_HEADS = 8
DSA_HEAD_DIM = 64
IDX_HEADS = 8
IDX_DIM = 64
TOPK_MAX = 256
D_FF = 2816

LANES = 128
SUBLANES = 8
N_PAIRS = 4
NEG_BIG = -1e30
INT_MIN = -2 ** 31
FIELD_GUARDS = 0x8000 - 2 ** 31
VMEM_LIMIT = 58 * 1024 * 1024
TQ = 256
TK = 512
TM = 512
HK = TK // 2

_OFF_CQ, _OFF_CKV, _OFF_KR, _OFF_QS, _OFF_KS, _OFF_VS, _OFF_QI, _OFF_KI, _OFF_WI = (
    0, 256, 384, 416, 928, 1440, 1952, 2464, 2528)
_C_IN = 2536
_P_CQ, _P_CKV, _P_QS, _P_KS, _P_VS, _P_QI, _P_KI, _P_KR, _P_WI, _P_END = (
    0, 256, 384, 896, 1408, 1920, 2432, 2560, 2688, 2816)


def _pair_rope_cols(base, hd=64):
    half = hd // 2
    cols = []
    for p in range(N_PAIRS):
        a, b = 2 * p, 2 * p + 1
        cols += [base + a * hd + d for d in range(half)]
        cols += [base + b * hd + d for d in range(half)]
        cols += [base + a * hd + half + d for d in range(half)]
        cols += [base + b * hd + half + d for d in range(half)]
    return cols


def _w_in_cols():
    z = _C_IN
    cols = list(range(_OFF_CQ, _OFF_CQ + 256)) + list(range(_OFF_CKV, _OFF_CKV + 128))
    cols += _pair_rope_cols(_OFF_QS) + _pair_rope_cols(_OFF_KS)
    cols += list(range(_OFF_VS, _OFF_VS + 512))
    cols += _pair_rope_cols(_OFF_QI)
    ki1 = [_OFF_KI + d for d in range(32)]
    ki2 = [_OFF_KI + 32 + d for d in range(32)]
    cols += ki1 + ki1 + ki2 + ki2
    kr1 = [_OFF_KR + d for d in range(16)]
    kr2 = [_OFF_KR + 16 + d for d in range(16)]
    cols += kr1 + kr1 + [z] * 32 + kr2 + kr2 + [z] * 32
    cols += [_OFF_WI + d for d in range(IDX_HEADS)] + [z] * (LANES - IDX_HEADS)
    assert len(cols) == _P_END
    return np.asarray(cols, np.int32)


def _w_uq_cols():
    z = MLA_HEADS * MLA_QK_DIM
    cols = []
    for p in range(N_PAIRS):
        a, b = 2 * p, 2 * p + 1
        cols += [a * MLA_QK_DIM + d for d in range(64)] + [b * MLA_QK_DIM + d for d in range(64)]
        cols += [a * MLA_QK_DIM + 64 + d for d in range(16)] + [b * MLA_QK_DIM + 64 + d for d in range(16)]
        cols += [z] * 32
        cols += [a * MLA_QK_DIM + 80 + d for d in range(16)] + [b * MLA_QK_DIM + 80 + d for d in range(16)]
        cols += [z] * 32
    return np.asarray(cols, np.int32)


def _w_ukv_cols():
    cols = [h * 128 + d for h in range(MLA_HEADS) for d in range(64)]
    cols += [h * 128 + 64 + d for h in range(MLA_HEADS) for d in range(64)]
    return np.asarray(cols, np.int32)


def _take_cols(w, cols):
    wz = jnp.concatenate([w, jnp.zeros((w.shape[0], 1), w.dtype)], axis=1)
    return jnp.take(wz, jnp.asarray(cols), axis=1)


def _rope_tables(tp):
    pos = jnp.arange(tp, dtype=jnp.int32).astype(F32)

    def cs(half):
        inv = jnp.power(ROPE_THETA, -jnp.arange(half, dtype=F32) / half)
        ang = pos[:, None] * inv[None, :]
        return jnp.cos(ang), jnp.sin(ang)

    c32, s32 = cs(32)
    c16, s16 = cs(16)
    z = jnp.zeros((tp, 32), F32)
    c64t = jnp.concatenate([c32, c32, c32, c32], axis=1)
    s64t = jnp.concatenate([-s32, -s32, s32, s32], axis=1)
    c32t = jnp.concatenate([c16, c16, z, c16, c16, z], axis=1)
    s32t = jnp.concatenate([-s16, -s16, z, s16, s16, z], axis=1)
    return c64t, s64t, c32t, s32t


def _rms(x, g):
    return x * lax.rsqrt(jnp.mean(x * x, axis=-1, keepdims=True) + EPS) * g


def _rope_slab(x, c, s):
    return x * c + pltpu.roll(x, 64, axis=1) * s


def _proj_kernel(x_ref, edge_ref, g_ref, win_ref, gq_ref, wuq_ref, gkv_ref, wukv_ref,
                 c64_ref, s64_ref, c32_ref, s32_ref,
                 qmla_ref, kmla_ref, vmla_ref, qs_ref, ks_ref, vs_ref, qi_ref, ki_ref, wi_ref, *, first_tail):
    i = pl.program_id(1)
    h = jnp.where((i == 0) | (i >= first_tail), edge_ref[0, 0], x_ref[0])
    u = _rms(h, g_ref[...]).astype(BF16)
    proj = jnp.dot(u, win_ref[...], preferred_element_type=F32)
    c64, s64, c32, s32 = c64_ref[...], s64_ref[...], c32_ref[...], s32_ref[...]

    cq = _rms(proj[:, _P_CQ:_P_CQ + 256], gq_ref[...]).astype(BF16)
    q = jnp.dot(cq, wuq_ref[...], preferred_element_type=F32)
    ckv = _rms(proj[:, _P_CKV:_P_CKV + 128], gkv_ref[...]).astype(BF16)
    kv = jnp.dot(ckv, wukv_ref[...], preferred_element_type=F32)
    kr = _rope_slab(proj[:, _P_KR:_P_KR + 128], c32, s32).astype(BF16)
    q_scale = MLA_QK_DIM ** -0.5
    s_scale = DSA_HEAD_DIM ** -0.5
    i_scale = IDX_DIM ** -0.5
    for p in range(N_PAIRS):
        qn = q[:, p * 256:p * 256 + 128] * q_scale
        qr = _rope_slab(q[:, p * 256 + 128:p * 256 + 256], c32, s32) * q_scale
        qmla_ref[0, p, :, 0:128] = qn.astype(BF16)
        qmla_ref[0, p, :, 128:256] = qr.astype(BF16)
        kmla_ref[0, p, :, 0:128] = kv[:, p * 128:(p + 1) * 128].astype(BF16)
        kmla_ref[0, p, :, 128:256] = kr
        vmla_ref[0, p] = kv[:, 512 + p * 128:512 + (p + 1) * 128].astype(BF16)
        sl = slice(p * 128, (p + 1) * 128)
        qs_ref[0, p] = (_rope_slab(proj[:, _P_QS:_P_KS][:, sl], c64, s64) * s_scale).astype(BF16)
        ks_ref[0, p] = _rope_slab(proj[:, _P_KS:_P_VS][:, sl], c64, s64).astype(BF16)
        vs_ref[0, p] = proj[:, _P_VS:_P_QI][:, sl].astype(BF16)
        qi_ref[0, p] = (_rope_slab(proj[:, _P_QI:_P_KI][:, sl], c64, s64) * i_scale).astype(BF16)
    ki_ref[0] = _rope_slab(proj[:, _P_KI:_P_KR], c64, s64).astype(BF16)
    wi_ref[0] = proj[:, _P_WI:_P_END] * (IDX_HEADS ** -0.5)


def _proj_call(x, edges, g, win, gq, wuq, gkv, wukv, tables, *, tp, first_tail, interpret=False):
    b, seq, d = x.shape
    tm = TM
    nt = tp // tm
    assert seq >= tm and (seq - tm) % 16 == 0 and N_META % 16 == 0
    x_rows = pl.BlockSpec(
        (pl.Element(1), pl.Element(tm), pl.Element(d)),
        lambda bi, i: (bi, pl.multiple_of(jnp.clip(i * tm - N_META, 0, seq - tm), 16), 0))
    edge = pl.BlockSpec((1, 1, tm, d), lambda bi, i: (bi, jnp.where(i == 0, 0, 1 + jnp.maximum(i - first_tail, 0)), 0, 0))
    const = lambda shape: pl.BlockSpec(shape, lambda bi, i: (0,) * len(shape))
    tab = pl.BlockSpec((tm, LANES), lambda bi, i: (i, 0))
    pair = lambda w: pl.BlockSpec((1, N_PAIRS, tm, w), lambda bi, i: (bi, 0, i, 0))
    row = lambda w: pl.BlockSpec((1, tm, w), lambda bi, i: (bi, i, 0))
    sds = jax.ShapeDtypeStruct
    out_shape = (
        sds((b, N_PAIRS, tp, 256), BF16), sds((b, N_PAIRS, tp, 256), BF16), sds((b, N_PAIRS, tp, 128), BF16),
        sds((b, N_PAIRS, tp, 128), BF16), sds((b, N_PAIRS, tp, 128), BF16), sds((b, N_PAIRS, tp, 128), BF16),
        sds((b, N_PAIRS, tp, 128), BF16), sds((b, tp, 128), BF16), sds((b, tp, 128), F32))
    return pl.pallas_call(
        functools.partial(_proj_kernel, first_tail=first_tail),
        grid=(b, nt),
        in_specs=[x_rows, edge, const((1, d)), const(win.shape), const((1, MLA_Q_RANK)), const(wuq.shape),
                  const((1, MLA_KV_RANK)), const(wukv.shape), tab, tab, tab, tab],
        out_specs=(pair(256), pair(256), pair(128), pair(128), pair(128), pair(128), pair(128),
                   row(128), row(128)),
        out_shape=out_shape,
        compiler_params=pltpu.CompilerParams(
            dimension_semantics=("arbitrary", "arbitrary"), vmem_limit_bytes=VMEM_LIMIT),
        interpret=interpret,
        name="proj",
    )(x, edges, g, win, gq, wuq, gkv, wukv, *tables)


_NT = (((1,), (1,)), ((), ()))
N_STREAMS = 2


def _srl(x, n):
    return lax.shift_right_logical(x, jnp.full(x.shape, n, x.dtype))


def _chunk_counts(i, tq):
    n_chunks = ((i + 1) * tq + TK - 1) // TK
    return n_chunks, (n_chunks + 1) // 2


class _Stream:
    def __init__(self, g, qm, s, p, a, m, l, acc):
        self.qm = qm.at[g]
        self.s = (s.at[0, g], s.at[1, g])
        self.p = (p.at[0, g], p.at[1, g])
        self.a = (a.at[0, g], a.at[1, g])
        self.m, self.l, self.acc = m.at[g], l.at[g], acc.at[g]

    def init(self):
        self.m[...] = jnp.full(self.m.shape, NEG_BIG, F32)
        self.l[...] = jnp.zeros(self.l.shape, F32)
        self.acc[...] = jnp.zeros(self.acc.shape, F32)
        self.p[1][...] = jnp.zeros(self.p[1].shape, BF16)
        self.a[1][...] = jnp.ones(self.a[1].shape, F32)

    def scores(self, k, slot):
        self.s[slot][...] = lax.dot_general(self.qm[...], k, _NT, preferred_element_type=F32)

    def softmax(self, s, slot):
        m_prev = self.m[...]
        m_new = jnp.maximum(m_prev, jnp.max(s, axis=-1, keepdims=True))
        alpha = jnp.exp(m_prev - m_new)
        p = jnp.exp(s - jnp.concatenate([m_new] * (s.shape[1] // LANES), axis=1))
        self.l[...] = alpha * self.l[...] + jnp.sum(p, axis=-1, keepdims=True)
        self.m[...] = m_new
        self.p[slot][...] = p.astype(BF16)
        self.a[slot][...] = alpha

    def pv(self, v, slot):
        self.acc[...] = self.a[slot][...] * self.acc[...] + jnp.dot(self.p[slot][...], v, preferred_element_type=F32)

    def finish(self, tq):
        lane = lax.broadcasted_iota(I32, (tq, LANES), 1)
        oa = self.acc[0:tq] / self.l[0:tq]
        ob = self.acc[tq:2 * tq] / self.l[tq:2 * tq]
        return jnp.where(lane < 64, oa, ob)


def _stream_scratch(tq, width):
    g = N_STREAMS
    return [pltpu.VMEM((g, 2 * tq, width), BF16),
            pltpu.VMEM((2, g, 2 * tq, TK), F32), pltpu.VMEM((2, g, 2 * tq, TK), BF16),
            pltpu.VMEM((2, g, 2 * tq, LANES), F32),
            pltpu.VMEM((g, 2 * tq, LANES), F32), pltpu.VMEM((g, 2 * tq, LANES), F32),
            pltpu.VMEM((g, 2 * tq, LANES), F32)]


def _pair_body(streams, c, prep, last):
    for st, _, v_at in streams:
        st.pv(v_at(c - 1), 1)
    for st, _, _ in streams:
        st.softmax(prep(st.s[0], c), 0)
    for st, k_at, _ in streams:
        st.scores(k_at(c + 1), 1)
    for st, _, v_at in streams:
        st.pv(v_at(c), 0)
    for st, _, _ in streams:
        st.softmax(prep(st.s[1], c + 1), 1)
    for st, k_at, v_at in streams:
        if last:
            st.pv(v_at(c + 1), 1)
        else:
            st.scores(k_at(c + 2), 0)


def _mla_kernel(q_ref, k_ref, v_ref, o_ref, qm, s, p, a, m, l, acc, *, tq, n_q_blocks):
    i = pl.program_id(2)

    @pl.when(i >= n_q_blocks)
    def _pad_rows():
        o_ref[...] = jnp.zeros(o_ref.shape, o_ref.dtype)

    @pl.when(i < n_q_blocks)
    def _attend():
        lane = lax.broadcasted_iota(I32, (tq, 256), 1)
        in_a = (lane < 64) | ((lane >= 128) & (lane < 144)) | ((lane >= 192) & (lane < 208))
        in_b = ((lane >= 64) & (lane < 128)) | ((lane >= 144) & (lane < 160)) | ((lane >= 208) & (lane < 224))
        n_chunks, n_pairs = _chunk_counts(i, tq)

        def rows(c):
            cc = jnp.clip(c, 0, n_chunks - 1)
            return pl.ds(pl.multiple_of(cc * TK, TK), TK)

        streams = []
        for g in range(N_STREAMS):
            st = _Stream(g, qm, s, p, a, m, l, acc)
            q = q_ref[0, g]
            zero = jnp.zeros_like(q)
            st.qm[0:tq] = jnp.where(in_a, q, zero)
            st.qm[tq:2 * tq] = jnp.where(in_b, q, zero)
            st.init()
            streams.append((st, lambda c, g=g: k_ref[0, g, rows(c), :], lambda c, g=g: v_ref[0, g, rows(c), :]))

        def masked(s_ref, c):
            qry = (lax.broadcasted_iota(I32, (2 * tq, TK), 0) & (tq - 1)) + i * tq
            key = lax.broadcasted_iota(I32, (2 * tq, TK), 1) + c * TK
            return jnp.where(key <= qry, s_ref[...], NEG_BIG)

        for st, k_at, _ in streams:
            st.scores(k_at(0), 0)

        def pair(j, carry):
            _pair_body(streams, 2 * j, lambda s_ref, c: s_ref[...], last=False)
            return carry

        lax.fori_loop(0, n_pairs - 1, pair, 0)
        _pair_body(streams, 2 * (n_pairs - 1), masked, last=True)
        for g, (st, _, _) in enumerate(streams):
            o_ref[0, :, g * LANES:(g + 1) * LANES] = st.finish(tq).astype(o_ref.dtype)


def _mla_call(qmla, kmla, vmla, *, tq, n_q_blocks, interpret=False):
    b, _, tp, _ = qmla.shape
    g = N_STREAMS
    return pl.pallas_call(
        functools.partial(_mla_kernel, tq=tq, n_q_blocks=n_q_blocks),
        grid=(b, N_PAIRS // g, tp // tq),
        in_specs=[pl.BlockSpec((1, g, tq, 256), lambda bi, pg, i: (bi, pg, i, 0)),
                  pl.BlockSpec((1, g, tp, 256), lambda bi, pg, i: (bi, pg, 0, 0)),
                  pl.BlockSpec((1, g, tp, 128), lambda bi, pg, i: (bi, pg, 0, 0))],
        out_specs=pl.BlockSpec((1, tq, g * LANES), lambda bi, pg, i: (bi, i, pg)),
        out_shape=jax.ShapeDtypeStruct((b, tp, N_PAIRS * LANES), BF16),
        scratch_shapes=_stream_scratch(tq, 256),
        compiler_params=pltpu.CompilerParams(
            dimension_semantics=("arbitrary", "arbitrary", "arbitrary"), vmem_limit_bytes=VMEM_LIMIT),
        interpret=interpret,
        name="mla_attn",
    )(qmla, kmla, vmla)


def _dsa_kernel(qi_ref, wi_ref, qs_ref, ki_ref, ks_ref, vs_ref, o_ref,
                key_scr, fld_scr, qim_scr, wt_scr,
                qm, s, p, a, m, l, acc, *, tq, topk, n_q_blocks):
    i = pl.program_id(1)
    pg = pl.program_id(2)
    n_chunks, n_pairs = _chunk_counts(i, tq)
    n_tiles = n_chunks * (TK // HK)
    slabs = HK // SUBLANES
    lane = lax.broadcasted_iota(I32, (tq, LANES), 1)
    in_a = (lane < 32) | ((lane >= 64) & (lane < 96))

    @pl.when(i >= n_q_blocks)
    def _pad_rows():
        o_ref[...] = jnp.zeros(o_ref.shape, o_ref.dtype)

    @pl.when((i < n_q_blocks) & (pg == 0))
    def _select():
        for pp in range(N_PAIRS):
            qp = qi_ref[0, pp].astype(F32)
            zero = jnp.zeros_like(qp)
            qim_scr[:, (2 * pp) * tq:(2 * pp + 1) * tq] = jnp.where(in_a, qp, zero).T.astype(BF16)
            qim_scr[:, (2 * pp + 1) * tq:(2 * pp + 2) * tq] = jnp.where(in_a, zero, qp).T.astype(BF16)
        wt_scr[...] = wi_ref[0].T

        def logits(t):
            kc = ki_ref[0, pl.ds(pl.multiple_of(t * HK, HK), HK), :]
            return jnp.dot(kc, qim_scr[...], preferred_element_type=F32)

        def to_keys(lg, t, diagonal):
            sc = jnp.zeros((HK, tq), F32)
            for hh in range(IDX_HEADS):
                sc = sc + wt_scr[hh:hh + 1, :] * jnp.maximum(lg[:, hh * tq:(hh + 1) * tq], 0.0)
            bits = pltpu.bitcast(sc, I32)
            key = bits ^ ((bits >> 31) & 0x7FFFFFFF)
            key = key - (key >> 31)
            if diagonal:
                kpos = lax.broadcasted_iota(I32, (HK, tq), 0) + t * HK
                qpos = lax.broadcasted_iota(I32, (HK, tq), 1) + i * tq
                key = jnp.where(kpos <= qpos, key, INT_MIN)
            key_scr[t] = key
            return key

        def pack(fa, fb):
            return (fa << 16) | fb | FIELD_GUARDS

        def score_chunk(c, diagonal):
            lg0 = logits(2 * c)
            lg1 = logits(2 * c + 1)
            k0 = to_keys(lg0, 2 * c, diagonal)
            k1 = to_keys(lg1, 2 * c + 1, diagonal)
            top = lambda key: _srl(key ^ INT_MIN, 17)
            fld_scr[c] = pack(top(k0), top(k1))

        n_below = (i * tq) // TK

        def below(c, carry):
            score_chunk(c, False)
            return carry

        def across(c, carry):
            score_chunk(c, True)
            return carry

        lax.fori_loop(0, n_below, below, 0)
        lax.fori_loop(n_below, n_chunks, across, 0)

        def count_fields(cand):
            both = ((cand << 16) | cand)[None]

            def one(c, acc):
                w = fld_scr[c].reshape(slabs, SUBLANES, tq)
                hit = _srl(w - both, 15) & 0x00010001
                return acc + jnp.sum(hit, axis=0, dtype=I32)

            def two(c2, acc):
                return one(2 * c2 + 1, one(2 * c2, acc))

            acc = lax.fori_loop(0, n_chunks // 2, two, jnp.zeros((SUBLANES, tq), I32))
            acc = lax.fori_loop(2 * (n_chunks // 2), n_chunks, one, acc)
            acc = (acc & 0xFFFF) + _srl(acc, 16)
            return jnp.broadcast_to(jnp.sum(acc, axis=0, keepdims=True, dtype=I32), (SUBLANES, tq))

        def field_search(above, n_ge):
            def step(it, carry):
                t, n = carry
                cand = t | lax.shift_left(jnp.int32(1), lax.convert_element_type(14 - it, I32))
                cnt = count_fields(cand) + above
                return jnp.where(cnt >= topk, cand, t), jnp.where(cnt >= topk, cnt, n)
            return lax.fori_loop(0, 15, step, (jnp.zeros((SUBLANES, tq), I32), n_ge))

        zeros = jnp.zeros((SUBLANES, tq), I32)
        t_top, n_ge = field_search(zeros, jnp.full((SUBLANES, tq), 2 ** 30, I32))
        above = count_fields(t_top + 1)

        def mid_fields(c, carry):
            def mid(key):
                u = (key ^ INT_MIN).reshape(slabs, SUBLANES, tq)
                alive = _srl(u, 17) == t_top[None]
                return jnp.where(alive, _srl(u, 2) & 0x7FFF, 0).reshape(HK, tq)
            fld_scr[c] = pack(mid(key_scr[2 * c]), mid(key_scr[2 * c + 1]))
            return carry

        lax.fori_loop(0, n_chunks, mid_fields, 0)
        t_mid, n_ge = field_search(above, n_ge)

        def count(pred):
            def body(c, acc):
                for u in range(TK // HK):
                    t = c * (TK // HK) + u
                    kk = key_scr[t].reshape(slabs, SUBLANES, tq)
                    acc = acc + jnp.sum(pred(kk, t).astype(I32), axis=0, dtype=I32)
                return acc
            acc = lax.fori_loop(0, n_chunks, body, jnp.zeros((SUBLANES, tq), I32))
            return jnp.broadcast_to(jnp.sum(acc, axis=0, keepdims=True, dtype=I32), (SUBLANES, tq))

        def bit_step(it, carry):
            t, n = carry
            cand = t | lax.shift_left(jnp.int32(1), lax.convert_element_type(1 - it, I32))
            cnt = count(lambda kk, _: kk >= (cand ^ INT_MIN)[None])
            return jnp.where(cnt >= topk, cand, t), jnp.where(cnt >= topk, cnt, n)

        thr_u, n_ge = lax.fori_loop(0, 2, bit_step, ((t_top << 17) | (t_mid << 2), n_ge))
        thr = thr_u ^ INT_MIN
        excess = jnp.max(jnp.where((n_ge > topk) & (thr > INT_MIN), 1, 0))

        def store_bias(t, sel):
            bias = jnp.where(sel, 0.0, NEG_BIG).astype(F32).reshape(HK, tq)
            key_scr[t] = pltpu.bitcast(bias.T, I32)

        @pl.when(excess == 0)
        def _no_ties():
            floor = jnp.maximum(thr, INT_MIN + 1)[None]

            def to_bias(c, carry):
                for u in range(TK // HK):
                    t = c * (TK // HK) + u
                    store_bias(t, key_scr[t].reshape(slabs, SUBLANES, tq) >= floor)
                return carry

            lax.fori_loop(0, n_chunks, to_bias, 0)

        @pl.when(excess > 0)
        def _ties():
            kidx = (lax.broadcasted_iota(I32, (slabs, SUBLANES, tq), 0) * SUBLANES
                    + lax.broadcasted_iota(I32, (slabs, SUBLANES, tq), 1))
            need = topk - count(lambda kk, _: kk > thr[None])

            def idx_step(it, x):
                cand = x + lax.shift_left(jnp.int32(1), lax.convert_element_type(13 - it, I32))
                cnt = count(lambda kk, t: (kk == thr[None]) & ((kidx + t * HK) < cand[None]))
                return jnp.where(cnt < need, cand, x)

            cut = lax.fori_loop(0, 14, idx_step, jnp.zeros((SUBLANES, tq), I32))
            cut = jnp.where(n_ge > topk, cut, 2 ** 30)[None]
            th = thr[None]

            def to_bias(t, carry):
                kk = key_scr[t].reshape(slabs, SUBLANES, tq)
                store_bias(t, (kk > th) | ((kk == th) & ((kidx + t * HK) <= cut) & (th > INT_MIN)))
                return carry

            lax.fori_loop(0, n_tiles, to_bias, 0)

        def fill(t, carry):
            key_scr[t] = pltpu.bitcast(jnp.full((HK, tq), NEG_BIG, F32), I32)
            return carry

        lax.fori_loop(n_tiles, 2 * n_pairs * (TK // HK), fill, 0)

    @pl.when(i < n_q_blocks)
    def _attend():
        def rows(c):
            cc = jnp.clip(c, 0, n_chunks - 1)
            return pl.ds(pl.multiple_of(cc * TK, TK), TK)

        streams = []
        for g in range(N_STREAMS):
            st = _Stream(g, qm, s, p, a, m, l, acc)
            pr = pg * N_STREAMS + g
            q = qs_ref[0, pr]
            zero = jnp.zeros_like(q)
            st.qm[0:tq] = jnp.where(in_a, q, zero)
            st.qm[tq:2 * tq] = jnp.where(in_a, zero, q)
            st.init()
            streams.append((st, lambda c, pr=pr: ks_ref[0, pr, rows(c), :], lambda c, pr=pr: vs_ref[0, pr, rows(c), :]))

        def biased(s_ref, c):
            bias = jnp.concatenate(
                [pltpu.bitcast(key_scr[c * (TK // HK) + t], F32) for t in range(TK // HK)], axis=1)
            return s_ref[...] + jnp.concatenate([bias, bias], axis=0)

        for st, k_at, _ in streams:
            st.scores(k_at(0), 0)

        def pair(j, carry):
            _pair_body(streams, 2 * j, biased, last=False)
            return carry

        lax.fori_loop(0, n_pairs, pair, 0)
        for g, (st, _, v_at) in enumerate(streams):
            st.pv(v_at(2 * n_pairs - 1), 1)
            o_ref[0, :, g * LANES:(g + 1) * LANES] = st.finish(tq).astype(o_ref.dtype)


def _dsa_call(qi, wi, qs, ki, ks, vs, *, tq, topk, n_q_blocks, interpret=False):
    b, _, tp, _ = qs.shape
    assert HK == tq, "mask tiles are transposed in place"
    once = pl.Buffered(1)
    max_pairs = (tp // TK + 1) // 2
    return pl.pallas_call(
        functools.partial(_dsa_kernel, tq=tq, topk=topk, n_q_blocks=n_q_blocks),
        grid=(b, tp // tq, N_PAIRS // N_STREAMS),
        in_specs=[pl.BlockSpec((1, N_PAIRS, tq, LANES), lambda bi, i, p: (bi, 0, i, 0)),
                  pl.BlockSpec((1, tq, LANES), lambda bi, i, p: (bi, i, 0)),
                  pl.BlockSpec((1, N_PAIRS, tq, LANES), lambda bi, i, p: (bi, 0, i, 0)),
                  pl.BlockSpec((1, tp, LANES), lambda bi, i, p: (bi, 0, 0), pipeline_mode=once),
                  pl.BlockSpec((1, N_PAIRS, tp, LANES), lambda bi, i, p: (bi, 0, 0, 0), pipeline_mode=once),
                  pl.BlockSpec((1, N_PAIRS, tp, LANES), lambda bi, i, p: (bi, 0, 0, 0), pipeline_mode=once)],
        out_specs=pl.BlockSpec((1, tq, N_STREAMS * LANES), lambda bi, i, p: (bi, i, p)),
        out_shape=jax.ShapeDtypeStruct((b, tp, N_PAIRS * LANES), BF16),
        scratch_shapes=[pltpu.VMEM((2 * max_pairs * (TK // HK), HK, tq), I32),
                        pltpu.VMEM((tp // TK, HK, tq), I32),
                        pltpu.VMEM((LANES, IDX_HEADS * tq), BF16),
                        pltpu.VMEM((LANES, tq), F32)]
        + _stream_scratch(tq, LANES),
        compiler_params=pltpu.CompilerParams(
            dimension_semantics=("arbitrary", "arbitrary", "arbitrary"), vmem_limit_bytes=VMEM_LIMIT),
        interpret=interpret,
        name="dsa_attn",
    )(qi, wi, qs, ki, ks, vs)


def _ffn_kernel(x_ref, oa_ref, ob_ref, wo_ref, g2_ref, wg_ref, wu_ref, wd_ref, gf_ref, out_ref):
    o = jnp.concatenate([oa_ref[0], ob_ref[0]], axis=1)
    h1 = x_ref[0] + jnp.dot(o, wo_ref[...], preferred_element_type=F32)
    u = _rms(h1, g2_ref[...]).astype(BF16)
    gate = jnp.dot(u, wg_ref[...], preferred_element_type=F32)
    up = jnp.dot(u, wu_ref[...], preferred_element_type=F32)
    act = (gate * jax.nn.sigmoid(gate) * up).astype(BF16)
    h2 = h1 + jnp.dot(act, wd_ref[...], preferred_element_type=F32)
    out_ref[0] = _rms(h2, gf_ref[...])


def _ffn_call(x, oa, ob, wo, g2, wg, wu, wd, gf, *, row0, n_tiles, interpret=False):
    b, _, d = x.shape
    tm = TM
    assert row0 % 16 == 0 and row0 + n_tiles * tm <= oa.shape[1]
    once = pl.Buffered(1)
    const = lambda shape: pl.BlockSpec(shape, lambda bi, i: (0,) * len(shape), pipeline_mode=once)
    rows_in = lambda w: pl.BlockSpec((pl.Element(1), pl.Element(tm), pl.Element(w)),
                                     lambda bi, i: (bi, pl.multiple_of(row0 + i * tm, 16), 0))
    return pl.pallas_call(
        _ffn_kernel,
        grid=(b, n_tiles),
        in_specs=[pl.BlockSpec((1, tm, d), lambda bi, i: (bi, i, 0)),
                  rows_in(oa.shape[-1]), rows_in(ob.shape[-1]), const(wo.shape), const((1, d)),
                  const(wg.shape), const(wu.shape), const(wd.shape), const((1, d))],
        out_specs=pl.BlockSpec((1, tm, d), lambda bi, i: (bi, i, 0)),
        out_shape=jax.ShapeDtypeStruct((b, n_tiles * tm, d), F32),
        compiler_params=pltpu.CompilerParams(
            dimension_semantics=("arbitrary", "arbitrary"), vmem_limit_bytes=VMEM_LIMIT),
        interpret=interpret,
        name="ffn",
    )(x, oa, ob, wo, g2, wg, wu, wd, gf)


def _layer(x, meta, attn_norm_g, w_in, mla_q_norm_g, w_uq, mla_kv_norm_g, w_ukv, w_o, ffn_norm_g,
           w_gate, w_up, w_down, out_g, topk, *, interpret=False):
    b, seq, d = x.shape
    t = N_META + seq
    n_out_tiles = -(-seq // TM)
    tp = -(-max(t, N_META + n_out_tiles * TM) // TK) * TK
    first_tail = t // TM
    tail = jnp.concatenate([x[:, first_tail * TM - N_META:], jnp.zeros((b, tp - t, d), x.dtype)], axis=1)
    head = jnp.concatenate([jnp.broadcast_to(meta[None].astype(x.dtype), (b, N_META, d)), x[:, :TM - N_META]], axis=1)
    edges = jnp.concatenate([head[:, None], tail.reshape(b, tp // TM - first_tail, TM, d)], axis=1)
    n_q_blocks = -(-t // TQ)
    win = _take_cols(w_in, _w_in_cols()).astype(BF16)
    wuq = _take_cols(w_uq, _w_uq_cols()).astype(BF16)
    wukv = jnp.take(w_ukv, jnp.asarray(_w_ukv_cols()), axis=1).astype(BF16)
    tables = _rope_tables(tp)
    qmla, kmla, vmla, qs, ks, vs, qi, ki, wi = _proj_call(
        x, edges, attn_norm_g[None], win, mla_q_norm_g[None], wuq, mla_kv_norm_g[None], wukv, tables,
        tp=tp, first_tail=first_tail, interpret=interpret)
    o_mla = _mla_call(qmla, kmla, vmla, tq=TQ, n_q_blocks=n_q_blocks, interpret=interpret)
    o_dsa = _dsa_call(qi, wi, qs, ki, ks, vs, tq=TQ, topk=topk, n_q_blocks=n_q_blocks, interpret=interpret)
    return _ffn_call(x, o_mla, o_dsa, w_o.astype(BF16), ffn_norm_g[None], w_gate.astype(BF16),
                     w_up.astype(BF16), w_down.astype(BF16), out_g[None], row0=N_META, n_tiles=n_out_tiles,
                     interpret=interpret)


def kernel(x, meta_tokens, attn_norm_g, w_in, mla_q_norm_g, w_uq, mla_kv_norm_g, w_ukv, w_o,
           ffn_norm_g, w_gate, w_up, w_down, final_norm_g):
    seq = x.shape[1]
    depth = w_in.shape[0]
    assert depth == 1, "the final norm is fused into the layer's last kernel"
    topk = min(TOPK_MAX, seq // 4)
    out = _layer(x, meta_tokens, attn_norm_g[0], w_in[0], mla_q_norm_g[0], w_uq[0], mla_kv_norm_g[0], w_ukv[0],
                 w_o[0], ffn_norm_g[0], w_gate[0], w_up[0], w_down[0], final_norm_g, topk)
    return out[:, :seq]
```

```python
import functools

import numpy as np
import jax
import jax.numpy as jnp
from jax import lax
from jax.experimental import pallas as pl
from jax.experimental.pallas import tpu as pltpu

F32 = jnp.float32
BF16 = jnp.bfloat16
I32 = jnp.int32

D_MODEL = 1024
N_META = 16
ROPE_THETA = 10000.0
EPS = 1e-6
MLA_HEADS = 8
MLA_Q_RANK = 256
MLA_KV_RANK = 128
MLA_NOPE_DIM = 64
MLA_ROPE_DIM = 32
MLA_QK_DIM = MLA_NOPE_DIM + MLA_ROPE_DIM
MLA_V_DIM = 64
DSA_HEADS = 8
DSA_HEAD_DIM = 64
IDX_HEADS = 8
IDX_DIM = 64
TOPK_MAX = 256
D_FF = 2816

LANES = 128
SUBLANES = 8
N_PAIRS = 4
NEG_BIG = -1e30
INT_MIN = -2 ** 31
FIELD_GUARDS = 0x8000 - 2 ** 31
VMEM_LIMIT = 58 * 1024 * 1024
TQ = 256
TK = 512
TM = 512
HK = TK // 2

_OFF_CQ, _OFF_CKV, _OFF_KR, _OFF_QS, _OFF_KS, _OFF_VS, _OFF_QI, _OFF_KI, _OFF_WI = (
    0, 256, 384, 416, 928, 1440, 1952, 2464, 2528)
_C_IN = 2536
_P_CQ, _P_CKV, _P_QS, _P_KS, _P_VS, _P_QI, _P_KI, _P_KR, _P_WI, _P_END = (
    0, 256, 384, 896, 1408, 1920, 2432, 2560, 2688, 2816)


def _pair_rope_cols(base, hd=64):
    half = hd // 2
    cols = []
    for p in range(N_PAIRS):
        a, b = 2 * p, 2 * p + 1
        cols += [base + a * hd + d for d in range(half)]
        cols += [base + b * hd + d for d in range(half)]
        cols += [base + a * hd + half + d for d in range(half)]
        cols += [base + b * hd + half + d for d in range(half)]
    return cols


def _w_in_cols():
    z = _C_IN
    cols = list(range(_OFF_CQ, _OFF_CQ + 256)) + list(range(_OFF_CKV, _OFF_CKV + 128))
    cols += _pair_rope_cols(_OFF_QS) + _pair_rope_cols(_OFF_KS)
    cols += list(range(_OFF_VS, _OFF_VS + 512))
    cols += _pair_rope_cols(_OFF_QI)
    ki1 = [_OFF_KI + d for d in range(32)]
    ki2 = [_OFF_KI + 32 + d for d in range(32)]
    cols += ki1 + ki1 + ki2 + ki2
    kr1 = [_OFF_KR + d for d in range(16)]
    kr2 = [_OFF_KR + 16 + d for d in range(16)]
    cols += kr1 + kr1 + [z] * 32 + kr2 + kr2 + [z] * 32
    cols += [_OFF_WI + d for d in range(IDX_HEADS)] + [z] * (LANES - IDX_HEADS)
    assert len(cols) == _P_END
    return np.asarray(cols, np.int32)


def _w_uq_cols():
    z = MLA_HEADS * MLA_QK_DIM
    cols = []
    for p in range(N_PAIRS):
        a, b = 2 * p, 2 * p + 1
        cols += [a * MLA_QK_DIM + d for d in range(64)] + [b * MLA_QK_DIM + d for d in range(64)]
        cols += [a * MLA_QK_DIM + 64 + d for d in range(16)] + [b * MLA_QK_DIM + 64 + d for d in range(16)]
        cols += [z] * 32
        cols += [a * MLA_QK_DIM + 80 + d for d in range(16)] + [b * MLA_QK_DIM + 80 + d for d in range(16)]
        cols += [z] * 32
    return np.asarray(cols, np.int32)


def _w_ukv_cols():
    cols = [h * 128 + d for h in range(MLA_HEADS) for d in range(64)]
    cols += [h * 128 + 64 + d for h in range(MLA_HEADS) for d in range(64)]
    return np.asarray(cols, np.int32)


def _take_cols(w, cols):
    wz = jnp.concatenate([w, jnp.zeros((w.shape[0], 1), w.dtype)], axis=1)
    return jnp.take(wz, jnp.asarray(cols), axis=1)


def _rope_tables(tp):
    pos = jnp.arange(tp, dtype=jnp.int32).astype(F32)

    def cs(half):
        inv = jnp.power(ROPE_THETA, -jnp.arange(half, dtype=F32) / half)
        ang = pos[:, None] * inv[None, :]
        return jnp.cos(ang), jnp.sin(ang)

    c32, s32 = cs(32)
    c16, s16 = cs(16)
    z = jnp.zeros((tp, 32), F32)
    c64t = jnp.concatenate([c32, c32, c32, c32], axis=1)
    s64t = jnp.concatenate([-s32, -s32, s32, s32], axis=1)
    c32t = jnp.concatenate([c16, c16, z, c16, c16, z], axis=1)
    s32t = jnp.concatenate([-s16, -s16, z, s16, s16, z], axis=1)
    return c64t, s64t, c32t, s32t


def _rms(x, g):
    return x * lax.rsqrt(jnp.mean(x * x, axis=-1, keepdims=True) + EPS) * g


def _rope_slab(x, c, s):
    return x * c + pltpu.roll(x, 64, axis=1) * s


def _proj_kernel(x_ref, edge_ref, g_ref, win_ref, gq_ref, wuq_ref, gkv_ref, wukv_ref,
                 c64_ref, s64_ref, c32_ref, s32_ref,
                 qmla_ref, kmla_ref, vmla_ref, qs_ref, ks_ref, vs_ref, qi_ref, ki_ref, wi_ref, *, first_tail):
    i = pl.program_id(1)
    h = jnp.where((i == 0) | (i >= first_tail), edge_ref[0, 0], x_ref[0])
    u = _rms(h, g_ref[...]).astype(BF16)
    proj = jnp.dot(u, win_ref[...], preferred_element_type=F32)
    c64, s64, c32, s32 = c64_ref[...], s64_ref[...], c32_ref[...], s32_ref[...]

    cq = _rms(proj[:, _P_CQ:_P_CQ + 256], gq_ref[...]).astype(BF16)
    q = jnp.dot(cq, wuq_ref[...], preferred_element_type=F32)
    ckv = _rms(proj[:, _P_CKV:_P_CKV + 128], gkv_ref[...]).astype(BF16)
    kv = jnp.dot(ckv, wukv_ref[...], preferred_element_type=F32)
    kr = _rope_slab(proj[:, _P_KR:_P_KR + 128], c32, s32).astype(BF16)
    q_scale = MLA_QK_DIM ** -0.5
    s_scale = DSA_HEAD_DIM ** -0.5
    i_scale = IDX_DIM ** -0.5
    for p in range(N_PAIRS):
        qn = q[:, p * 256:p * 256 + 128] * q_scale
        qr = _rope_slab(q[:, p * 256 + 128:p * 256 + 256], c32, s32) * q_scale
        qmla_ref[0, p, :, 0:128] = qn.astype(BF16)
        qmla_ref[0, p, :, 128:256] = qr.astype(BF16)
        kmla_ref[0, p, :, 0:128] = kv[:, p * 128:(p + 1) * 128].astype(BF16)
        kmla_ref[0, p, :, 128:256] = kr
        vmla_ref[0, p] = kv[:, 512 + p * 128:512 + (p + 1) * 128].astype(BF16)
        sl = slice(p * 128, (p + 1) * 128)
        qs_ref[0, p] = (_rope_slab(proj[:, _P_QS:_P_KS][:, sl], c64, s64) * s_scale).astype(BF16)
        ks_ref[0, p] = _rope_slab(proj[:, _P_KS:_P_VS][:, sl], c64, s64).astype(BF16)
        vs_ref[0, p] = proj[:, _P_VS:_P_QI][:, sl].astype(BF16)
        qi_ref[0, p] = (_rope_slab(proj[:, _P_QI:_P_KI][:, sl], c64, s64) * i_scale).astype(BF16)
    ki_ref[0] = _rope_slab(proj[:, _P_KI:_P_KR], c64, s64).astype(BF16)
    wi_ref[0] = proj[:, _P_WI:_P_END] * (IDX_HEADS ** -0.5)


def _proj_call(x, edges, g, win, gq, wuq, gkv, wukv, tables, *, tp, first_tail, interpret=False):
    b, seq, d = x.shape
    tm = TM
    nt = tp // tm
    assert seq >= tm and (seq - tm) % 16 == 0 and N_META % 16 == 0
    x_rows = pl.BlockSpec(
        (pl.Element(1), pl.Element(tm), pl.Element(d)),
        lambda bi, i: (bi, pl.multiple_of(jnp.clip(i * tm - N_META, 0, seq - tm), 16), 0))
    edge = pl.BlockSpec((1, 1, tm, d), lambda bi, i: (bi, jnp.where(i == 0, 0, 1 + jnp.maximum(i - first_tail, 0)), 0, 0))
    const = lambda shape: pl.BlockSpec(shape, lambda bi, i: (0,) * len(shape))
    tab = pl.BlockSpec((tm, LANES), lambda bi, i: (i, 0))
    pair = lambda w: pl.BlockSpec((1, N_PAIRS, tm, w), lambda bi, i: (bi, 0, i, 0))
    row = lambda w: pl.BlockSpec((1, tm, w), lambda bi, i: (bi, i, 0))
    sds = jax.ShapeDtypeStruct
    out_shape = (
        sds((b, N_PAIRS, tp, 256), BF16), sds((b, N_PAIRS, tp, 256), BF16), sds((b, N_PAIRS, tp, 128), BF16),
        sds((b, N_PAIRS, tp, 128), BF16), sds((b, N_PAIRS, tp, 128), BF16), sds((b, N_PAIRS, tp, 128), BF16),
        sds((b, N_PAIRS, tp, 128), BF16), sds((b, tp, 128), BF16), sds((b, tp, 128), F32))
    return pl.pallas_call(
        functools.partial(_proj_kernel, first_tail=first_tail),
        grid=(b, nt),
        in_specs=[x_rows, edge, const((1, d)), const(win.shape), const((1, MLA_Q_RANK)), const(wuq.shape),
                  const((1, MLA_KV_RANK)), const(wukv.shape), tab, tab, tab, tab],
        out_specs=(pair(256), pair(256), pair(128), pair(128), pair(128), pair(128), pair(128),
                   row(128), row(128)),
        out_shape=out_shape,
        compiler_params=pltpu.CompilerParams(
            dimension_semantics=("arbitrary", "arbitrary"), vmem_limit_bytes=VMEM_LIMIT),
        interpret=interpret,
        name="proj",
    )(x, edges, g, win, gq, wuq, gkv, wukv, *tables)


_NT = (((1,), (1,)), ((), ()))
N_STREAMS = 2


def _srl(x, n):
    return lax.shift_right_logical(x, jnp.full(x.shape, n, x.dtype))


def _chunk_counts(i, tq):
    n_chunks = ((i + 1) * tq + TK - 1) // TK
    return n_chunks, (n_chunks + 1) // 2


class _Stream:
    def __init__(self, g, qm, s, p, a, m, l, acc):
        self.qm = qm.at[g]
        self.s = (s.at[0, g], s.at[1, g])
        self.p = (p.at[0, g], p.at[1, g])
        self.a = (a.at[0, g], a.at[1, g])
        self.m, self.l, self.acc = m.at[g], l.at[g], acc.at[g]

    def init(self):
        self.m[...] = jnp.full(self.m.shape, NEG_BIG, F32)
        self.l[...] = jnp.zeros(self.l.shape, F32)
        self.acc[...] = jnp.zeros(self.acc.shape, F32)
        self.p[1][...] = jnp.zeros(self.p[1].shape, BF16)
        self.a[1][...] = jnp.ones(self.a[1].shape, F32)

    def scores(self, k, slot):
        self.s[slot][...] = lax.dot_general(self.qm[...], k, _NT, preferred_element_type=F32)

    def softmax(self, s, slot):
        m_prev = self.m[...]
        m_new = jnp.maximum(m_prev, jnp.max(s, axis=-1, keepdims=True))
        alpha = jnp.exp(m_prev - m_new)
        p = jnp.exp(s - jnp.concatenate([m_new] * (s.shape[1] // LANES), axis=1))
        self.l[...] = alpha * self.l[...] + jnp.sum(p, axis=-1, keepdims=True)
        self.m[...] = m_new
        self.p[slot][...] = p.astype(BF16)
        self.a[slot][...] = alpha

    def pv(self, v, slot):
        self.acc[...] = self.a[slot][...] * self.acc[...] + jnp.dot(self.p[slot][...], v, preferred_element_type=F32)

    def finish(self, tq):
        lane = lax.broadcasted_iota(I32, (tq, LANES), 1)
        oa = self.acc[0:tq] / self.l[0:tq]
        ob = self.acc[tq:2 * tq] / self.l[tq:2 * tq]
        return jnp.where(lane < 64, oa, ob)


def _stream_scratch(tq, width):
    g = N_STREAMS
    return [pltpu.VMEM((g, 2 * tq, width), BF16),
            pltpu.VMEM((2, g, 2 * tq, TK), F32), pltpu.VMEM((2, g, 2 * tq, TK), BF16),
            pltpu.VMEM((2, g, 2 * tq, LANES), F32),
            pltpu.VMEM((g, 2 * tq, LANES), F32), pltpu.VMEM((g, 2 * tq, LANES), F32),
            pltpu.VMEM((g, 2 * tq, LANES), F32)]


def _pair_body(streams, c, prep, last):
    for st, _, v_at in streams:
        st.pv(v_at(c - 1), 1)
    for st, _, _ in streams:
        st.softmax(prep(st.s[0], c), 0)
    for st, k_at, _ in streams:
        st.scores(k_at(c + 1), 1)
    for st, _, v_at in streams:
        st.pv(v_at(c), 0)
    for st, _, _ in streams:
        st.softmax(prep(st.s[1], c + 1), 1)
    for st, k_at, v_at in streams:
        if last:
            st.pv(v_at(c + 1), 1)
        else:
            st.scores(k_at(c + 2), 0)


def _mla_kernel(q_ref, k_ref, v_ref, o_ref, qm, s, p, a, m, l, acc, *, tq, n_q_blocks):
    i = pl.program_id(2)

    @pl.when(i >= n_q_blocks)
    def _pad_rows():
        o_ref[...] = jnp.zeros(o_ref.shape, o_ref.dtype)

    @pl.when(i < n_q_blocks)
    def _attend():
        lane = lax.broadcasted_iota(I32, (tq, 256), 1)
        in_a = (lane < 64) | ((lane >= 128) & (lane < 144)) | ((lane >= 192) & (lane < 208))
        in_b = ((lane >= 64) & (lane < 128)) | ((lane >= 144) & (lane < 160)) | ((lane >= 208) & (lane < 224))
        n_chunks, n_pairs = _chunk_counts(i, tq)

        def rows(c):
            cc = jnp.clip(c, 0, n_chunks - 1)
            return pl.ds(pl.multiple_of(cc * TK, TK), TK)

        streams = []
        for g in range(N_STREAMS):
            st = _Stream(g, qm, s, p, a, m, l, acc)
            q = q_ref[0, g]
            zero = jnp.zeros_like(q)
            st.qm[0:tq] = jnp.where(in_a, q, zero)
            st.qm[tq:2 * tq] = jnp.where(in_b, q, zero)
            st.init()
            streams.append((st, lambda c, g=g: k_ref[0, g, rows(c), :], lambda c, g=g: v_ref[0, g, rows(c), :]))

        def masked(s_ref, c):
            qry = (lax.broadcasted_iota(I32, (2 * tq, TK), 0) & (tq - 1)) + i * tq
            key = lax.broadcasted_iota(I32, (2 * tq, TK), 1) + c * TK
            return jnp.where(key <= qry, s_ref[...], NEG_BIG)

        for st, k_at, _ in streams:
            st.scores(k_at(0), 0)

        def pair(j, carry):
            _pair_body(streams, 2 * j, lambda s_ref, c: s_ref[...], last=False)
            return carry

        lax.fori_loop(0, n_pairs - 1, pair, 0)
        _pair_body(streams, 2 * (n_pairs - 1), masked, last=True)
        for g, (st, _, _) in enumerate(streams):
            o_ref[0, :, g * LANES:(g + 1) * LANES] = st.finish(tq).astype(o_ref.dtype)


def _mla_call(qmla, kmla, vmla, *, tq, n_q_blocks, interpret=False):
    b, _, tp, _ = qmla.shape
    g = N_STREAMS
    return pl.pallas_call(
        functools.partial(_mla_kernel, tq=tq, n_q_blocks=n_q_blocks),
        grid=(b, N_PAIRS // g, tp // tq),
        in_specs=[pl.BlockSpec((1, g, tq, 256), lambda bi, pg, i: (bi, pg, i, 0)),
                  pl.BlockSpec((1, g, tp, 256), lambda bi, pg, i: (bi, pg, 0, 0)),
                  pl.BlockSpec((1, g, tp, 128), lambda bi, pg, i: (bi, pg, 0, 0))],
        out_specs=pl.BlockSpec((1, tq, g * LANES), lambda bi, pg, i: (bi, i, pg)),
        out_shape=jax.ShapeDtypeStruct((b, tp, N_PAIRS * LANES), BF16),
        scratch_shapes=_stream_scratch(tq, 256),
        compiler_params=pltpu.CompilerParams(
            dimension_semantics=("arbitrary", "arbitrary", "arbitrary"), vmem_limit_bytes=VMEM_LIMIT),
        interpret=interpret,
        name="mla_attn",
    )(qmla, kmla, vmla)


def _dsa_kernel(qi_ref, wi_ref, qs_ref, ki_ref, ks_ref, vs_ref, o_ref,
                key_scr, fld_scr, qim_scr, wt_scr,
                qm, s, p, a, m, l, acc, *, tq, topk, n_q_blocks):
    i = pl.program_id(1)
    pg = pl.program_id(2)
    n_chunks, n_pairs = _chunk_counts(i, tq)
    n_tiles = n_chunks * (TK // HK)
    slabs = HK // SUBLANES
    lane = lax.broadcasted_iota(I32, (tq, LANES), 1)
    in_a = (lane < 32) | ((lane >= 64) & (lane < 96))

    @pl.when(i >= n_q_blocks)
    def _pad_rows():
        o_ref[...] = jnp.zeros(o_ref.shape, o_ref.dtype)

    @pl.when((i < n_q_blocks) & (pg == 0))
    def _select():
        for pp in range(N_PAIRS):
            qp = qi_ref[0, pp].astype(F32)
            zero = jnp.zeros_like(qp)
            qim_scr[:, (2 * pp) * tq:(2 * pp + 1) * tq] = jnp.where(in_a, qp, zero).T.astype(BF16)
            qim_scr[:, (2 * pp + 1) * tq:(2 * pp + 2) * tq] = jnp.where(in_a, zero, qp).T.astype(BF16)
        wt_scr[...] = wi_ref[0].T

        def logits(t):
            kc = ki_ref[0, pl.ds(pl.multiple_of(t * HK, HK), HK), :]
            return jnp.dot(kc, qim_scr[...], preferred_element_type=F32)

        def to_keys(lg, t, diagonal):
            sc = jnp.zeros((HK, tq), F32)
            for hh in range(IDX_HEADS):
                sc = sc + wt_scr[hh:hh + 1, :] * jnp.maximum(lg[:, hh * tq:(hh + 1) * tq], 0.0)
            bits = pltpu.bitcast(sc, I32)
            key = bits ^ ((bits >> 31) & 0x7FFFFFFF)
            key = key - (key >> 31)
            if diagonal:
                kpos = lax.broadcasted_iota(I32, (HK, tq), 0) + t * HK
                qpos = lax.broadcasted_iota(I32, (HK, tq), 1) + i * tq
                key = jnp.where(kpos <= qpos, key, INT_MIN)
            key_scr[t] = key
            return key

        def pack(fa, fb):
            return (fa << 16) | fb | FIELD_GUARDS

        def score_chunk(c, diagonal):
            lg0 = logits(2 * c)
            lg1 = logits(2 * c + 1)
            k0 = to_keys(lg0, 2 * c, diagonal)
            k1 = to_keys(lg1, 2 * c + 1, diagonal)
            top = lambda key: _srl(key ^ INT_MIN, 17)
            fld_scr[c] = pack(top(k0), top(k1))

        n_below = (i * tq) // TK

        def below2(c2, carry):
            score_chunk(2 * c2, False)
            score_chunk(2 * c2 + 1, False)
            return carry

        def below(c, carry):
            score_chunk(c, False)
            return carry

        def across(c, carry):
            score_chunk(c, True)
            return carry

        lax.fori_loop(0, n_below // 2, below2, 0)
        lax.fori_loop(2 * (n_below // 2), n_below, below, 0)
        lax.fori_loop(n_below, n_chunks, across, 0)

        def count_fields(cand):
            both = ((cand << 16) | cand)[None]

            def one(c, acc):
                w = fld_scr[c].reshape(slabs, SUBLANES, tq)
                hit = _srl(w - both, 15) & 0x00010001
                return acc + jnp.sum(hit, axis=0, dtype=I32)

            def two(c2, acc):
                return one(2 * c2 + 1, one(2 * c2, acc))

            acc = lax.fori_loop(0, n_chunks // 2, two, jnp.zeros((SUBLANES, tq), I32))
            acc = lax.fori_loop(2 * (n_chunks // 2), n_chunks, one, acc)
            acc = (acc & 0xFFFF) + _srl(acc, 16)
            return jnp.broadcast_to(jnp.sum(acc, axis=0, keepdims=True, dtype=I32), (SUBLANES, tq))

        def field_search(above, n_ge):
            def step(it, carry):
                t, n = carry
                cand = t | lax.shift_left(jnp.int32(1), lax.convert_element_type(14 - it, I32))
                cnt = count_fields(cand) + above
                return jnp.where(cnt >= topk, cand, t), jnp.where(cnt >= topk, cnt, n)
            return lax.fori_loop(0, 15, step, (jnp.zeros((SUBLANES, tq), I32), n_ge))

        zeros = jnp.zeros((SUBLANES, tq), I32)
        t_top, n_ge = field_search(zeros, jnp.full((SUBLANES, tq), 2 ** 30, I32))
        above = count_fields(t_top + 1)

        def mid_fields(c, carry):
            def mid(key):
                u = (key ^ INT_MIN).reshape(slabs, SUBLANES, tq)
                alive = _srl(u, 17) == t_top[None]
                return jnp.where(alive, _srl(u, 2) & 0x7FFF, 0).reshape(HK, tq)
            fld_scr[c] = pack(mid(key_scr[2 * c]), mid(key_scr[2 * c + 1]))
            return carry

        lax.fori_loop(0, n_chunks, mid_fields, 0)
        t_mid, n_ge = field_search(above, n_ge)

        def count(pred):
            def body(c, acc):
                for u in range(TK // HK):
                    t = c * (TK // HK) + u
                    kk = key_scr[t].reshape(slabs, SUBLANES, tq)
                    acc = acc + jnp.sum(pred(kk, t).astype(I32), axis=0, dtype=I32)
                return acc
            acc = lax.fori_loop(0, n_chunks, body, jnp.zeros((SUBLANES, tq), I32))
            return jnp.broadcast_to(jnp.sum(acc, axis=0, keepdims=True, dtype=I32), (SUBLANES, tq))

        def bit_step(it, carry):
            t, n = carry
            cand = t | lax.shift_left(jnp.int32(1), lax.convert_element_type(1 - it, I32))
            cnt = count(lambda kk, _: kk >= (cand ^ INT_MIN)[None])
            return jnp.where(cnt >= topk, cand, t), jnp.where(cnt >= topk, cnt, n)

        thr_u, n_ge = lax.fori_loop(0, 2, bit_step, ((t_top << 17) | (t_mid << 2), n_ge))
        thr = thr_u ^ INT_MIN
        excess = jnp.max(jnp.where((n_ge > topk) & (thr > INT_MIN), 1, 0))

        def store_bias(t, sel):
            bias = jnp.where(sel, 0.0, NEG_BIG).astype(F32).reshape(HK, tq)
            key_scr[t] = pltpu.bitcast(bias.T, I32)

        @pl.when(excess == 0)
        def _no_ties():
            floor = jnp.maximum(thr, INT_MIN + 1)[None]

            def to_bias(c, carry):
                for u in range(TK // HK):
                    t = c * (TK // HK) + u
                    store_bias(t, key_scr[t].reshape(slabs, SUBLANES, tq) >= floor)
                return carry

            lax.fori_loop(0, n_chunks, to_bias, 0)

        @pl.when(excess > 0)
        def _ties():
            kidx = (lax.broadcasted_iota(I32, (slabs, SUBLANES, tq), 0) * SUBLANES
                    + lax.broadcasted_iota(I32, (slabs, SUBLANES, tq), 1))
            need = topk - count(lambda kk, _: kk > thr[None])

            def idx_step(it, x):
                cand = x + lax.shift_left(jnp.int32(1), lax.convert_element_type(13 - it, I32))
                cnt = count(lambda kk, t: (kk == thr[None]) & ((kidx + t * HK) < cand[None]))
                return jnp.where(cnt < need, cand, x)

            cut = lax.fori_loop(0, 14, idx_step, jnp.zeros((SUBLANES, tq), I32))
            cut = jnp.where(n_ge > topk, cut, 2 ** 30)[None]
            th = thr[None]

            def to_bias(t, carry):
                kk = key_scr[t].reshape(slabs, SUBLANES, tq)
                store_bias(t, (kk > th) | ((kk == th) & ((kidx + t * HK) <= cut) & (th > INT_MIN)))
                return carry

            lax.fori_loop(0, n_tiles, to_bias, 0)

        def fill(t, carry):
            key_scr[t] = pltpu.bitcast(jnp.full((HK, tq), NEG_BIG, F32), I32)
            return carry

        lax.fori_loop(n_tiles, 2 * n_pairs * (TK // HK), fill, 0)

    @pl.when(i < n_q_blocks)
    def _attend():
        def rows(c):
            cc = jnp.clip(c, 0, n_chunks - 1)
            return pl.ds(pl.multiple_of(cc * TK, TK), TK)

        streams = []
        for g in range(N_STREAMS):
            st = _Stream(g, qm, s, p, a, m, l, acc)
            pr = pg * N_STREAMS + g
            q = qs_ref[0, pr]
            zero = jnp.zeros_like(q)
            st.qm[0:tq] = jnp.where(in_a, q, zero)
            st.qm[tq:2 * tq] = jnp.where(in_a, zero, q)
            st.init()
            streams.append((st, lambda c, pr=pr: ks_ref[0, pr, rows(c), :], lambda c, pr=pr: vs_ref[0, pr, rows(c), :]))

        def biased(s_ref, c):
            bias = jnp.concatenate(
                [pltpu.bitcast(key_scr[c * (TK // HK) + t], F32) for t in range(TK // HK)], axis=1)
            return s_ref[...] + jnp.concatenate([bias, bias], axis=0)

        for st, k_at, _ in streams:
            st.scores(k_at(0), 0)

        def pair(j, carry):
            _pair_body(streams, 2 * j, biased, last=False)
            return carry

        lax.fori_loop(0, n_pairs, pair, 0)
        for g, (st, _, v_at) in enumerate(streams):
            st.pv(v_at(2 * n_pairs - 1), 1)
            o_ref[0, :, g * LANES:(g + 1) * LANES] = st.finish(tq).astype(o_ref.dtype)


def _dsa_call(qi, wi, qs, ki, ks, vs, *, tq, topk, n_q_blocks, interpret=False):
    b, _, tp, _ = qs.shape
    assert HK == tq, "mask tiles are transposed in place"
    once = pl.Buffered(1)
    max_pairs = (tp // TK + 1) // 2
    return pl.pallas_call(
        functools.partial(_dsa_kernel, tq=tq, topk=topk, n_q_blocks=n_q_blocks),
        grid=(b, tp // tq, N_PAIRS // N_STREAMS),
        in_specs=[pl.BlockSpec((1, N_PAIRS, tq, LANES), lambda bi, i, p: (bi, 0, i, 0)),
                  pl.BlockSpec((1, tq, LANES), lambda bi, i, p: (bi, i, 0)),
                  pl.BlockSpec((1, N_PAIRS, tq, LANES), lambda bi, i, p: (bi, 0, i, 0)),
                  pl.BlockSpec((1, tp, LANES), lambda bi, i, p: (bi, 0, 0), pipeline_mode=once),
                  pl.BlockSpec((1, N_PAIRS, tp, LANES), lambda bi, i, p: (bi, 0, 0, 0), pipeline_mode=once),
                  pl.BlockSpec((1, N_PAIRS, tp, LANES), lambda bi, i, p: (bi, 0, 0, 0), pipeline_mode=once)],
        out_specs=pl.BlockSpec((1, tq, N_STREAMS * LANES), lambda bi, i, p: (bi, i, p)),
        out_shape=jax.ShapeDtypeStruct((b, tp, N_PAIRS * LANES), BF16),
        scratch_shapes=[pltpu.VMEM((2 * max_pairs * (TK // HK), HK, tq), I32),
                        pltpu.VMEM((tp // TK, HK, tq), I32),
                        pltpu.VMEM((LANES, IDX_HEADS * tq), BF16),
                        pltpu.VMEM((LANES, tq), F32)]
        + _stream_scratch(tq, LANES),
        compiler_params=pltpu.CompilerParams(
            dimension_semantics=("arbitrary", "arbitrary", "arbitrary"), vmem_limit_bytes=VMEM_LIMIT),
        interpret=interpret,
        name="dsa_attn",
    )(qi, wi, qs, ki, ks, vs)


def _ffn_kernel(x_ref, oa_ref, ob_ref, wo_ref, g2_ref, wg_ref, wu_ref, wd_ref, gf_ref, out_ref):
    o = jnp.concatenate([oa_ref[0], ob_ref[0]], axis=1)
    h1 = x_ref[0] + jnp.dot(o, wo_ref[...], preferred_element_type=F32)
    u = _rms(h1, g2_ref[...]).astype(BF16)
    gate = jnp.dot(u, wg_ref[...], preferred_element_type=F32)
    up = jnp.dot(u, wu_ref[...], preferred_element_type=F32)
    act = (gate * jax.nn.sigmoid(gate) * up).astype(BF16)
    h2 = h1 + jnp.dot(act, wd_ref[...], preferred_element_type=F32)
    out_ref[0] = _rms(h2, gf_ref[...])


def _ffn_call(x, oa, ob, wo, g2, wg, wu, wd, gf, *, row0, n_tiles, interpret=False):
    b, _, d = x.shape
    tm = TM
    assert row0 % 16 == 0 and row0 + n_tiles * tm <= oa.shape[1]
    once = pl.Buffered(1)
    const = lambda shape: pl.BlockSpec(shape, lambda bi, i: (0,) * len(shape), pipeline_mode=once)
    rows_in = lambda w: pl.BlockSpec((pl.Element(1), pl.Element(tm), pl.Element(w)),
                                     lambda bi, i: (bi, pl.multiple_of(row0 + i * tm, 16), 0))
    return pl.pallas_call(
        _ffn_kernel,
        grid=(b, n_tiles),
        in_specs=[pl.BlockSpec((1, tm, d), lambda bi, i: (bi, i, 0)),
                  rows_in(oa.shape[-1]), rows_in(ob.shape[-1]), const(wo.shape), const((1, d)),
                  const(wg.shape), const(wu.shape), const(wd.shape), const((1, d))],
        out_specs=pl.BlockSpec((1, tm, d), lambda bi, i: (bi, i, 0)),
        out_shape=jax.ShapeDtypeStruct((b, n_tiles * tm, d), F32),
        compiler_params=pltpu.CompilerParams(
            dimension_semantics=("arbitrary", "arbitrary"), vmem_limit_bytes=VMEM_LIMIT),
        interpret=interpret,
        name="ffn",
    )(x, oa, ob, wo, g2, wg, wu, wd, gf)


def _layer(x, meta, attn_norm_g, w_in, mla_q_norm_g, w_uq, mla_kv_norm_g, w_ukv, w_o, ffn_norm_g,
           w_gate, w_up, w_down, out_g, topk, *, interpret=False):
    b, seq, d = x.shape
    t = N_META + seq
    n_out_tiles = -(-seq // TM)
    tp = -(-max(t, N_META + n_out_tiles * TM) // TK) * TK
    first_tail = t // TM
    tail = jnp.concatenate([x[:, first_tail * TM - N_META:], jnp.zeros((b, tp - t, d), x.dtype)], axis=1)
    head = jnp.concatenate([jnp.broadcast_to(meta[None].astype(x.dtype), (b, N_META, d)), x[:, :TM - N_META]], axis=1)
    edges = jnp.concatenate([head[:, None], tail.reshape(b, tp // TM - first_tail, TM, d)], axis=1)
    n_q_blocks = -(-t // TQ)
    win = _take_cols(w_in, _w_in_cols()).astype(BF16)
    wuq = _take_cols(w_uq, _w_uq_cols()).astype(BF16)
    wukv = jnp.take(w_ukv, jnp.asarray(_w_ukv_cols()), axis=1).astype(BF16)
    tables = _rope_tables(tp)
    qmla, kmla, vmla, qs, ks, vs, qi, ki, wi = _proj_call(
        x, edges, attn_norm_g[None], win, mla_q_norm_g[None], wuq, mla_kv_norm_g[None], wukv, tables,
        tp=tp, first_tail=first_tail, interpret=interpret)
    o_mla = _mla_call(qmla, kmla, vmla, tq=TQ, n_q_blocks=n_q_blocks, interpret=interpret)
    o_dsa = _dsa_call(qi, wi, qs, ki, ks, vs, tq=TQ, topk=topk, n_q_blocks=n_q_blocks, interpret=interpret)
    return _ffn_call(x, o_mla, o_dsa, w_o.astype(BF16), ffn_norm_g[None], w_gate.astype(BF16),
                     w_up.astype(BF16), w_down.astype(BF16), out_g[None], row0=N_META, n_tiles=n_out_tiles,
                     interpret=interpret)


def kernel(x, meta_tokens, attn_norm_g, w_in, mla_q_norm_g, w_uq, mla_kv_norm_g, w_ukv, w_o,
           ffn_norm_g, w_gate, w_up, w_down, final_norm_g):
    seq = x.shape[1]
    depth = w_in.shape[0]
    assert depth == 1, "the final norm is fused into the layer's last kernel"
    topk = min(TOPK_MAX, seq // 4)
    out = _layer(x, meta_tokens, attn_norm_g[0], w_in[0], mla_q_norm_g[0], w_uq[0], mla_kv_norm_g[0], w_ukv[0],
                 w_o[0], ffn_norm_g[0], w_gate[0], w_up[0], w_down[0], final_norm_g, topk)
    return out[:, :seq]
```

```python
import functools

import numpy as np
import jax
import jax.numpy as jnp
from jax import lax
from jax.experimental import pallas as pl
from jax.experimental.pallas import tpu as pltpu

F32 = jnp.float32
BF16 = jnp.bfloat16
I32 = jnp.int32

D_MODEL = 1024
N_META = 16
ROPE_THETA = 10000.0
EPS = 1e-6
MLA_HEADS = 8
MLA_Q_RANK = 256
MLA_KV_RANK = 128
MLA_NOPE_DIM = 64
MLA_ROPE_DIM = 32
MLA_QK_DIM = MLA_NOPE_DIM + MLA_ROPE_DIM
MLA_V_DIM = 64
DSA_HEADS = 8
DSA_HEAD_DIM = 64
IDX_HEADS = 8
IDX_DIM = 64
TOPK_MAX = 256
D_FF = 2816

LANES = 128
SUBLANES = 8
N_PAIRS = 4
NEG_BIG = -1e30
INT_MIN = -2 ** 31
FIELD_GUARDS = 0x8000 - 2 ** 31
VMEM_LIMIT = 58 * 1024 * 1024
TQ = 256
TK = 512
TM = 512
HK = TK // 2

_OFF_CQ, _OFF_CKV, _OFF_KR, _OFF_QS, _OFF_KS, _OFF_VS, _OFF_QI, _OFF_KI, _OFF_WI = (
    0, 256, 384, 416, 928, 1440, 1952, 2464, 2528)
_C_IN = 2536
_P_CQ, _P_CKV, _P_QS, _P_KS, _P_VS, _P_QI, _P_KI, _P_KR, _P_WI, _P_END = (
    0, 256, 384, 896, 1408, 1920, 2432, 2560, 2688, 2816)


def _pair_rope_cols(base, hd=64):
    half = hd // 2
    cols = []
    for p in range(N_PAIRS):
        a, b = 2 * p, 2 * p + 1
        cols += [base + a * hd + d for d in range(half)]
        cols += [base + b * hd + d for d in range(half)]
        cols += [base + a * hd + half + d for d in range(half)]
        cols += [base + b * hd + half + d for d in range(half)]
    return cols


def _w_in_cols():
    z = _C_IN
    cols = list(range(_OFF_CQ, _OFF_CQ + 256)) + list(range(_OFF_CKV, _OFF_CKV + 128))
    cols += _pair_rope_cols(_OFF_QS) + _pair_rope_cols(_OFF_KS)
    cols += list(range(_OFF_VS, _OFF_VS + 512))
    cols += _pair_rope_cols(_OFF_QI)
    ki1 = [_OFF_KI + d for d in range(32)]
    ki2 = [_OFF_KI + 32 + d for d in range(32)]
    cols += ki1 + ki1 + ki2 + ki2
    kr1 = [_OFF_KR + d for d in range(16)]
    kr2 = [_OFF_KR + 16 + d for d in range(16)]
    cols += kr1 + kr1 + [z] * 32 + kr2 + kr2 + [z] * 32
    cols += [_OFF_WI + d for d in range(IDX_HEADS)] + [z] * (LANES - IDX_HEADS)
    assert len(cols) == _P_END
    return np.asarray(cols, np.int32)


def _w_uq_cols():
    z = MLA_HEADS * MLA_QK_DIM
    cols = []
    for p in range(N_PAIRS):
        a, b = 2 * p, 2 * p + 1
        cols += [a * MLA_QK_DIM + d for d in range(64)] + [b * MLA_QK_DIM + d for d in range(64)]
        cols += [a * MLA_QK_DIM + 64 + d for d in range(16)] + [b * MLA_QK_DIM + 64 + d for d in range(16)]
        cols += [z] * 32
        cols += [a * MLA_QK_DIM + 80 + d for d in range(16)] + [b * MLA_QK_DIM + 80 + d for d in range(16)]
        cols += [z] * 32
    return np.asarray(cols, np.int32)


def _w_ukv_cols():
    cols = [h * 128 + d for h in range(MLA_HEADS) for d in range(64)]
    cols += [h * 128 + 64 + d for h in range(MLA_HEADS) for d in range(64)]
    return np.asarray(cols, np.int32)


def _take_cols(w, cols):
    wz = jnp.concatenate([w, jnp.zeros((w.shape[0], 1), w.dtype)], axis=1)
    return jnp.take(wz, jnp.asarray(cols), axis=1)


def _rope_tables(tp):
    pos = jnp.arange(tp, dtype=jnp.int32).astype(F32)

    def cs(half):
        inv = jnp.power(ROPE_THETA, -jnp.arange(half, dtype=F32) / half)
        ang = pos[:, None] * inv[None, :]
        return jnp.cos(ang), jnp.sin(ang)

    c32, s32 = cs(32)
    c16, s16 = cs(16)
    z = jnp.zeros((tp, 32), F32)
    c64t = jnp.concatenate([c32, c32, c32, c32], axis=1)
    s64t = jnp.concatenate([-s32, -s32, s32, s32], axis=1)
    c32t = jnp.concatenate([c16, c16, z, c16, c16, z], axis=1)
    s32t = jnp.concatenate([-s16, -s16, z, s16, s16, z], axis=1)
    return c64t, s64t, c32t, s32t


def _rms(x, g):
    return x * lax.rsqrt(jnp.mean(x * x, axis=-1, keepdims=True) + EPS) * g


def _rope_slab(x, c, s):
    return x * c + pltpu.roll(x, 64, axis=1) * s


def _proj_kernel(x_ref, edge_ref, g_ref, win_ref, gq_ref, wuq_ref, gkv_ref, wukv_ref,
                 c64_ref, s64_ref, c32_ref, s32_ref,
                 qmla_ref, kmla_ref, vmla_ref, qs_ref, ks_ref, vs_ref, qi_ref, ki_ref, wi_ref, *, first_tail):
    i = pl.program_id(1)
    h = jnp.where((i == 0) | (i >= first_tail), edge_ref[0, 0], x_ref[0])
    u = _rms(h, g_ref[...]).astype(BF16)
    proj = jnp.dot(u, win_ref[...], preferred_element_type=F32)
    c64, s64, c32, s32 = c64_ref[...], s64_ref[...], c32_ref[...], s32_ref[...]

    cq = _rms(proj[:, _P_CQ:_P_CQ + 256], gq_ref[...]).astype(BF16)
    q = jnp.dot(cq, wuq_ref[...], preferred_element_type=F32)
    ckv = _rms(proj[:, _P_CKV:_P_CKV + 128], gkv_ref[...]).astype(BF16)
    kv = jnp.dot(ckv, wukv_ref[...], preferred_element_type=F32)
    kr = _rope_slab(proj[:, _P_KR:_P_KR + 128], c32, s32).astype(BF16)
    q_scale = MLA_QK_DIM ** -0.5
    s_scale = DSA_HEAD_DIM ** -0.5
    i_scale = IDX_DIM ** -0.5
    for p in range(N_PAIRS):
        qn = q[:, p * 256:p * 256 + 128] * q_scale
        qr = _rope_slab(q[:, p * 256 + 128:p * 256 + 256], c32, s32) * q_scale
        qmla_ref[0, p, :, 0:128] = qn.astype(BF16)
        qmla_ref[0, p, :, 128:256] = qr.astype(BF16)
        kmla_ref[0, p, :, 0:128] = kv[:, p * 128:(p + 1) * 128].astype(BF16)
        kmla_ref[0, p, :, 128:256] = kr
        vmla_ref[0, p] = kv[:, 512 + p * 128:512 + (p + 1) * 128].astype(BF16)
        sl = slice(p * 128, (p + 1) * 128)
        qs_ref[0, p] = (_rope_slab(proj[:, _P_QS:_P_KS][:, sl], c64, s64) * s_scale).astype(BF16)
        ks_ref[0, p] = _rope_slab(proj[:, _P_KS:_P_VS][:, sl], c64, s64).astype(BF16)
        vs_ref[0, p] = proj[:, _P_VS:_P_QI][:, sl].astype(BF16)
        qi_ref[0, p] = (_rope_slab(proj[:, _P_QI:_P_KI][:, sl], c64, s64) * i_scale).astype(BF16)
    ki_ref[0] = _rope_slab(proj[:, _P_KI:_P_KR], c64, s64).astype(BF16)
    wi_ref[0] = proj[:, _P_WI:_P_END] * (IDX_HEADS ** -0.5)


def _proj_call(x, edges, g, win, gq, wuq, gkv, wukv, tables, *, tp, first_tail, interpret=False):
    b, seq, d = x.shape
    tm = TM
    nt = tp // tm
    assert seq >= tm and (seq - tm) % 16 == 0 and N_META % 16 == 0
    x_rows = pl.BlockSpec(
        (pl.Element(1), pl.Element(tm), pl.Element(d)),
        lambda bi, i: (bi, pl.multiple_of(jnp.clip(i * tm - N_META, 0, seq - tm), 16), 0))
    edge = pl.BlockSpec((1, 1, tm, d), lambda bi, i: (bi, jnp.where(i == 0, 0, 1 + jnp.maximum(i - first_tail, 0)), 0, 0))
    const = lambda shape: pl.BlockSpec(shape, lambda bi, i: (0,) * len(shape))
    tab = pl.BlockSpec((tm, LANES), lambda bi, i: (i, 0))
    pair = lambda w: pl.BlockSpec((1, N_PAIRS, tm, w), lambda bi, i: (bi, 0, i, 0))
    row = lambda w: pl.BlockSpec((1, tm, w), lambda bi, i: (bi, i, 0))
    sds = jax.ShapeDtypeStruct
    out_shape = (
        sds((b, N_PAIRS, tp, 256), BF16), sds((b, N_PAIRS, tp, 256), BF16), sds((b, N_PAIRS, tp, 128), BF16),
        sds((b, N_PAIRS, tp, 128), BF16), sds((b, N_PAIRS, tp, 128), BF16), sds((b, N_PAIRS, tp, 128), BF16),
        sds((b, N_PAIRS, tp, 128), BF16), sds((b, tp, 128), BF16), sds((b, tp, 128), F32))
    return pl.pallas_call(
        functools.partial(_proj_kernel, first_tail=first_tail),
        grid=(b, nt),
        in_specs=[x_rows, edge, const((1, d)), const(win.shape), const((1, MLA_Q_RANK)), const(wuq.shape),
                  const((1, MLA_KV_RANK)), const(wukv.shape), tab, tab, tab, tab],
        out_specs=(pair(256), pair(256), pair(128), pair(128), pair(128), pair(128), pair(128),
                   row(128), row(128)),
        out_shape=out_shape,
        compiler_params=pltpu.CompilerParams(
            dimension_semantics=("arbitrary", "arbitrary"), vmem_limit_bytes=VMEM_LIMIT),
        interpret=interpret,
        name="proj",
    )(x, edges, g, win, gq, wuq, gkv, wukv, *tables)


_NT = (((1,), (1,)), ((), ()))
N_STREAMS = 2


def _srl(x, n):
    return lax.shift_right_logical(x, jnp.full(x.shape, n, x.dtype))


def _chunk_counts(i, tq):
    n_chunks = ((i + 1) * tq + TK - 1) // TK
    return n_chunks, (n_chunks + 1) // 2


class _Stream:
    def __init__(self, g, qm, s, p, a, m, l, acc):
        self.qm = qm.at[g]
        self.s = (s.at[0, g], s.at[1, g])
        self.p = (p.at[0, g], p.at[1, g])
        self.a = (a.at[0, g], a.at[1, g])
        self.m, self.l, self.acc = m.at[g], l.at[g], acc.at[g]

    def init(self):
        self.m[...] = jnp.full(self.m.shape, NEG_BIG, F32)
        self.l[...] = jnp.zeros(self.l.shape, F32)
        self.acc[...] = jnp.zeros(self.acc.shape, F32)
        self.p[1][...] = jnp.zeros(self.p[1].shape, BF16)
        self.a[1][...] = jnp.ones(self.a[1].shape, F32)

    def scores(self, k, slot):
        self.s[slot][...] = lax.dot_general(self.qm[...], k, _NT, preferred_element_type=F32)

    def softmax(self, s, slot):
        m_prev = self.m[...]
        m_new = jnp.maximum(m_prev, jnp.max(s, axis=-1, keepdims=True))
        alpha = jnp.exp(m_prev - m_new)
        half = s.shape[1] // 2
        m_b = jnp.concatenate([m_new] * (half // LANES), axis=1)
        row_sum = jnp.zeros((s.shape[0], 1), F32)
        for h0 in (0, half):
            p = jnp.exp(s[:, h0:h0 + half] - m_b)
            row_sum = row_sum + jnp.sum(p, axis=-1, keepdims=True)
            self.p[slot][:, h0:h0 + half] = p.astype(BF16)
        self.l[...] = alpha * self.l[...] + row_sum
        self.m[...] = m_new
        self.a[slot][...] = alpha

    def pv(self, v, slot):
        self.acc[...] = self.a[slot][...] * self.acc[...] + jnp.dot(self.p[slot][...], v, preferred_element_type=F32)

    def finish(self, tq):
        lane = lax.broadcasted_iota(I32, (tq, LANES), 1)
        oa = self.acc[0:tq] / self.l[0:tq]
        ob = self.acc[tq:2 * tq] / self.l[tq:2 * tq]
        return jnp.where(lane < 64, oa, ob)


def _stream_scratch(tq, width):
    g = N_STREAMS
    return [pltpu.VMEM((g, 2 * tq, width), BF16),
            pltpu.VMEM((2, g, 2 * tq, TK), F32), pltpu.VMEM((2, g, 2 * tq, TK), BF16),
            pltpu.VMEM((2, g, 2 * tq, LANES), F32),
            pltpu.VMEM((g, 2 * tq, LANES), F32), pltpu.VMEM((g, 2 * tq, LANES), F32),
            pltpu.VMEM((g, 2 * tq, LANES), F32)]


def _pair_body(streams, c, prep, last):
    for st, _, v_at in streams:
        st.pv(v_at(c - 1), 1)
    for st, _, _ in streams:
        st.softmax(prep(st.s[0], c), 0)
    for st, k_at, _ in streams:
        st.scores(k_at(c + 1), 1)
    for st, _, v_at in streams:
        st.pv(v_at(c), 0)
    for st, _, _ in streams:
        st.softmax(prep(st.s[1], c + 1), 1)
    for st, k_at, v_at in streams:
        if last:
            st.pv(v_at(c + 1), 1)
        else:
            st.scores(k_at(c + 2), 0)


def _mla_kernel(q_ref, k_ref, v_ref, o_ref, qm, s, p, a, m, l, acc, *, tq, n_q_blocks):
    i = pl.program_id(2)

    @pl.when(i >= n_q_blocks)
    def _pad_rows():
        o_ref[...] = jnp.zeros(o_ref.shape, o_ref.dtype)

    @pl.when(i < n_q_blocks)
    def _attend():
        lane = lax.broadcasted_iota(I32, (tq, 256), 1)
        in_a = (lane < 64) | ((lane >= 128) & (lane < 144)) | ((lane >= 192) & (lane < 208))
        in_b = ((lane >= 64) & (lane < 128)) | ((lane >= 144) & (lane < 160)) | ((lane >= 208) & (lane < 224))
        n_chunks, n_pairs = _chunk_counts(i, tq)

        def rows(c):
            cc = jnp.clip(c, 0, n_chunks - 1)
            return pl.ds(pl.multiple_of(cc * TK, TK), TK)

        streams = []
        for g in range(N_STREAMS):
            st = _Stream(g, qm, s, p, a, m, l, acc)
            q = q_ref[0, g]
            zero = jnp.zeros_like(q)
            st.qm[0:tq] = jnp.where(in_a, q, zero)
            st.qm[tq:2 * tq] = jnp.where(in_b, q, zero)
            st.init()
            streams.append((st, lambda c, g=g: k_ref[0, g, rows(c), :], lambda c, g=g: v_ref[0, g, rows(c), :]))

        def masked(s_ref, c):
            qry = (lax.broadcasted_iota(I32, (2 * tq, TK), 0) & (tq - 1)) + i * tq
            key = lax.broadcasted_iota(I32, (2 * tq, TK), 1) + c * TK
            return jnp.where(key <= qry, s_ref[...], NEG_BIG)

        for st, k_at, _ in streams:
            st.scores(k_at(0), 0)

        def pair(j, carry):
            _pair_body(streams, 2 * j, lambda s_ref, c: s_ref[...], last=False)
            return carry

        lax.fori_loop(0, n_pairs - 1, pair, 0)
        _pair_body(streams, 2 * (n_pairs - 1), masked, last=True)
        for g, (st, _, _) in enumerate(streams):
            o_ref[0, :, g * LANES:(g + 1) * LANES] = st.finish(tq).astype(o_ref.dtype)


def _mla_call(qmla, kmla, vmla, *, tq, n_q_blocks, interpret=False):
    b, _, tp, _ = qmla.shape
    g = N_STREAMS
    return pl.pallas_call(
        functools.partial(_mla_kernel, tq=tq, n_q_blocks=n_q_blocks),
        grid=(b, N_PAIRS // g, tp // tq),
        in_specs=[pl.BlockSpec((1, g, tq, 256), lambda bi, pg, i: (bi, pg, i, 0)),
                  pl.BlockSpec((1, g, tp, 256), lambda bi, pg, i: (bi, pg, 0, 0)),
                  pl.BlockSpec((1, g, tp, 128), lambda bi, pg, i: (bi, pg, 0, 0))],
        out_specs=pl.BlockSpec((1, tq, g * LANES), lambda bi, pg, i: (bi, i, pg)),
        out_shape=jax.ShapeDtypeStruct((b, tp, N_PAIRS * LANES), BF16),
        scratch_shapes=_stream_scratch(tq, 256),
        compiler_params=pltpu.CompilerParams(
            dimension_semantics=("arbitrary", "arbitrary", "arbitrary"), vmem_limit_bytes=VMEM_LIMIT),
        interpret=interpret,
        name="mla_attn",
    )(qmla, kmla, vmla)


def _dsa_kernel(qi_ref, wi_ref, qs_ref, ki_ref, ks_ref, vs_ref, o_ref,
                key_scr, fld_scr, qim_scr, wt_scr,
                qm, s, p, a, m, l, acc, *, tq, topk, n_q_blocks):
    i = pl.program_id(1)
    pg = pl.program_id(2)
    n_chunks, n_pairs = _chunk_counts(i, tq)
    n_tiles = n_chunks * (TK // HK)
    slabs = HK // SUBLANES
    lane = lax.broadcasted_iota(I32, (tq, LANES), 1)
    in_a = (lane < 32) | ((lane >= 64) & (lane < 96))

    @pl.when(i >= n_q_blocks)
    def _pad_rows():
        o_ref[...] = jnp.zeros(o_ref.shape, o_ref.dtype)

    @pl.when((i < n_q_blocks) & (pg == 0))
    def _select():
        for pp in range(N_PAIRS):
            qp = qi_ref[0, pp].astype(F32)
            zero = jnp.zeros_like(qp)
            qim_scr[:, (2 * pp) * tq:(2 * pp + 1) * tq] = jnp.where(in_a, qp, zero).T.astype(BF16)
            qim_scr[:, (2 * pp + 1) * tq:(2 * pp + 2) * tq] = jnp.where(in_a, zero, qp).T.astype(BF16)
        wt_scr[...] = wi_ref[0].T

        def logits(t):
            kc = ki_ref[0, pl.ds(pl.multiple_of(t * HK, HK), HK), :]
            return jnp.dot(kc, qim_scr[...], preferred_element_type=F32)

        def to_keys(lg, t, diagonal):
            sc = jnp.zeros((HK, tq), F32)
            for hh in range(IDX_HEADS):
                sc = sc + wt_scr[hh:hh + 1, :] * jnp.maximum(lg[:, hh * tq:(hh + 1) * tq], 0.0)
            bits = pltpu.bitcast(sc, I32)
            key = bits ^ ((bits >> 31) & 0x7FFFFFFF)
            key = key - (key >> 31)
            if diagonal:
                kpos = lax.broadcasted_iota(I32, (HK, tq), 0) + t * HK
                qpos = lax.broadcasted_iota(I32, (HK, tq), 1) + i * tq
                key = jnp.where(kpos <= qpos, key, INT_MIN)
            key_scr[t] = key
            return key

        def pack(fa, fb):
            return (fa << 16) | fb | FIELD_GUARDS

        def score_chunk(c, diagonal):
            lg0 = logits(2 * c)
            lg1 = logits(2 * c + 1)
            k0 = to_keys(lg0, 2 * c, diagonal)
            k1 = to_keys(lg1, 2 * c + 1, diagonal)
            top = lambda key: _srl(key ^ INT_MIN, 17)
            fld_scr[c] = pack(top(k0), top(k1))

        n_below = (i * tq) // TK

        def below2(c2, carry):
            score_chunk(2 * c2, False)
            score_chunk(2 * c2 + 1, False)
            return carry

        def below(c, carry):
            score_chunk(c, False)
            return carry

        def across(c, carry):
            score_chunk(c, True)
            return carry

        lax.fori_loop(0, n_below // 2, below2, 0)
        lax.fori_loop(2 * (n_below // 2), n_below, below, 0)
        lax.fori_loop(n_below, n_chunks, across, 0)

        def count_fields(cand):
            both = ((cand << 16) | cand)[None]

            def one(c, acc):
                w = fld_scr[c].reshape(slabs, SUBLANES, tq)
                hit = _srl(w - both, 15) & 0x00010001
                return acc + jnp.sum(hit, axis=0, dtype=I32)

            def two(c2, acc):
                return one(2 * c2 + 1, one(2 * c2, acc))

            acc = lax.fori_loop(0, n_chunks // 2, two, jnp.zeros((SUBLANES, tq), I32))
            acc = lax.fori_loop(2 * (n_chunks // 2), n_chunks, one, acc)
            acc = (acc & 0xFFFF) + _srl(acc, 16)
            return jnp.broadcast_to(jnp.sum(acc, axis=0, keepdims=True, dtype=I32), (SUBLANES, tq))

        def field_search(above, n_ge):
            def step(it, carry):
                t, n = carry
                cand = t | lax.shift_left(jnp.int32(1), lax.convert_element_type(14 - it, I32))
                cnt = count_fields(cand) + above
                return jnp.where(cnt >= topk, cand, t), jnp.where(cnt >= topk, cnt, n)
            return lax.fori_loop(0, 15, step, (jnp.zeros((SUBLANES, tq), I32), n_ge))

        zeros = jnp.zeros((SUBLANES, tq), I32)
        t_top, n_ge = field_search(zeros, jnp.full((SUBLANES, tq), 2 ** 30, I32))
        above = count_fields(t_top + 1)

        def mid_fields(c, carry):
            def mid(key):
                u = (key ^ INT_MIN).reshape(slabs, SUBLANES, tq)
                alive = _srl(u, 17) == t_top[None]
                return jnp.where(alive, _srl(u, 2) & 0x7FFF, 0).reshape(HK, tq)
            fld_scr[c] = pack(mid(key_scr[2 * c]), mid(key_scr[2 * c + 1]))
            return carry

        lax.fori_loop(0, n_chunks, mid_fields, 0)
        t_mid, n_ge = field_search(above, n_ge)

        def count(pred):
            def body(c, acc):
                for u in range(TK // HK):
                    t = c * (TK // HK) + u
                    kk = key_scr[t].reshape(slabs, SUBLANES, tq)
                    acc = acc + jnp.sum(pred(kk, t).astype(I32), axis=0, dtype=I32)
                return acc
            acc = lax.fori_loop(0, n_chunks, body, jnp.zeros((SUBLANES, tq), I32))
            return jnp.broadcast_to(jnp.sum(acc, axis=0, keepdims=True, dtype=I32), (SUBLANES, tq))

        def bit_step(it, carry):
            t, n = carry
            cand = t | lax.shift_left(jnp.int32(1), lax.convert_element_type(1 - it, I32))
            cnt = count(lambda kk, _: kk >= (cand ^ INT_MIN)[None])
            return jnp.where(cnt >= topk, cand, t), jnp.where(cnt >= topk, cnt, n)

        thr_u, n_ge = lax.fori_loop(0, 2, bit_step, ((t_top << 17) | (t_mid << 2), n_ge))
        thr = thr_u ^ INT_MIN
        excess = jnp.max(jnp.where((n_ge > topk) & (thr > INT_MIN), 1, 0))

        def store_bias(t, sel):
            bias = jnp.where(sel, 0.0, NEG_BIG).astype(F32).reshape(HK, tq)
            key_scr[t] = pltpu.bitcast(bias.T, I32)

        @pl.when(excess == 0)
        def _no_ties():
            floor = jnp.maximum(thr, INT_MIN + 1)[None]

            def to_bias(c, carry):
                for u in range(TK // HK):
                    t = c * (TK // HK) + u
                    store_bias(t, key_scr[t].reshape(slabs, SUBLANES, tq) >= floor)
                return carry

            lax.fori_loop(0, n_chunks, to_bias, 0)

        @pl.when(excess > 0)
        def _ties():
            kidx = (lax.broadcasted_iota(I32, (slabs, SUBLANES, tq), 0) * SUBLANES
                    + lax.broadcasted_iota(I32, (slabs, SUBLANES, tq), 1))
            need = topk - count(lambda kk, _: kk > thr[None])

            def idx_step(it, x):
                cand = x + lax.shift_left(jnp.int32(1), lax.convert_element_type(13 - it, I32))
                cnt = count(lambda kk, t: (kk == thr[None]) & ((kidx + t * HK) < cand[None]))
                return jnp.where(cnt < need, cand, x)

            cut = lax.fori_loop(0, 14, idx_step, jnp.zeros((SUBLANES, tq), I32))
            cut = jnp.where(n_ge > topk, cut, 2 ** 30)[None]
            th = thr[None]

            def to_bias(t, carry):
                kk = key_scr[t].reshape(slabs, SUBLANES, tq)
                store_bias(t, (kk > th) | ((kk == th) & ((kidx + t * HK) <= cut) & (th > INT_MIN)))
                return carry

            lax.fori_loop(0, n_tiles, to_bias, 0)

        def fill(t, carry):
            key_scr[t] = pltpu.bitcast(jnp.full((HK, tq), NEG_BIG, F32), I32)
            return carry

        lax.fori_loop(n_tiles, 2 * n_pairs * (TK // HK), fill, 0)

    @pl.when(i < n_q_blocks)
    def _attend():
        def rows(c):
            cc = jnp.clip(c, 0, n_chunks - 1)
            return pl.ds(pl.multiple_of(cc * TK, TK), TK)

        streams = []
        for g in range(N_STREAMS):
            st = _Stream(g, qm, s, p, a, m, l, acc)
            pr = pg * N_STREAMS + g
            q = qs_ref[0, pr]
            zero = jnp.zeros_like(q)
            st.qm[0:tq] = jnp.where(in_a, q, zero)
            st.qm[tq:2 * tq] = jnp.where(in_a, zero, q)
            st.init()
            streams.append((st, lambda c, pr=pr: ks_ref[0, pr, rows(c), :], lambda c, pr=pr: vs_ref[0, pr, rows(c), :]))

        def biased(s_ref, c):
            bias = jnp.concatenate(
                [pltpu.bitcast(key_scr[c * (TK // HK) + t], F32) for t in range(TK // HK)], axis=1)
            return s_ref[...] + jnp.concatenate([bias, bias], axis=0)

        for st, k_at, _ in streams:
            st.scores(k_at(0), 0)

        def pair(j, carry):
            _pair_body(streams, 2 * j, biased, last=False)
            return carry

        lax.fori_loop(0, n_pairs, pair, 0)
        for g, (st, _, v_at) in enumerate(streams):
            st.pv(v_at(2 * n_pairs - 1), 1)
            o_ref[0, :, g * LANES:(g + 1) * LANES] = st.finish(tq).astype(o_ref.dtype)


def _dsa_call(qi, wi, qs, ki, ks, vs, *, tq, topk, n_q_blocks, interpret=False):
    b, _, tp, _ = qs.shape
    assert HK == tq, "mask tiles are transposed in place"
    once = pl.Buffered(1)
    max_pairs = (tp // TK + 1) // 2
    return pl.pallas_call(
        functools.partial(_dsa_kernel, tq=tq, topk=topk, n_q_blocks=n_q_blocks),
        grid=(b, tp // tq, N_PAIRS // N_STREAMS),
        in_specs=[pl.BlockSpec((1, N_PAIRS, tq, LANES), lambda bi, i, p: (bi, 0, i, 0)),
                  pl.BlockSpec((1, tq, LANES), lambda bi, i, p: (bi, i, 0)),
                  pl.BlockSpec((1, N_PAIRS, tq, LANES), lambda bi, i, p: (bi, 0, i, 0)),
                  pl.BlockSpec((1, tp, LANES), lambda bi, i, p: (bi, 0, 0), pipeline_mode=once),
                  pl.BlockSpec((1, N_PAIRS, tp, LANES), lambda bi, i, p: (bi, 0, 0, 0), pipeline_mode=once),
                  pl.BlockSpec((1, N_PAIRS, tp, LANES), lambda bi, i, p: (bi, 0, 0, 0), pipeline_mode=once)],
        out_specs=pl.BlockSpec((1, tq, N_STREAMS * LANES), lambda bi, i, p: (bi, i, p)),
        out_shape=jax.ShapeDtypeStruct((b, tp, N_PAIRS * LANES), BF16),
        scratch_shapes=[pltpu.VMEM((2 * max_pairs * (TK // HK), HK, tq), I32),
                        pltpu.VMEM((tp // TK, HK, tq), I32),
                        pltpu.VMEM((LANES, IDX_HEADS * tq), BF16),
                        pltpu.VMEM((LANES, tq), F32)]
        + _stream_scratch(tq, LANES),
        compiler_params=pltpu.CompilerParams(
            dimension_semantics=("arbitrary", "arbitrary", "arbitrary"), vmem_limit_bytes=VMEM_LIMIT),
        interpret=interpret,
        name="dsa_attn",
    )(qi, wi, qs, ki, ks, vs)


def _ffn_kernel(x_ref, oa_ref, ob_ref, wo_ref, g2_ref, wg_ref, wu_ref, wd_ref, gf_ref, out_ref):
    o = jnp.concatenate([oa_ref[0], ob_ref[0]], axis=1)
    h1 = x_ref[0] + jnp.dot(o, wo_ref[...], preferred_element_type=F32)
    u = _rms(h1, g2_ref[...]).astype(BF16)
    gate = jnp.dot(u, wg_ref[...], preferred_element_type=F32)
    up = jnp.dot(u, wu_ref[...], preferred_element_type=F32)
    act = (gate * jax.nn.sigmoid(gate) * up).astype(BF16)
    h2 = h1 + jnp.dot(act, wd_ref[...], preferred_element_type=F32)
    out_ref[0] = _rms(h2, gf_ref[...])


def _ffn_call(x, oa, ob, wo, g2, wg, wu, wd, gf, *, row0, n_tiles, interpret=False):
    b, _, d = x.shape
    tm = TM
    assert row0 % 16 == 0 and row0 + n_tiles * tm <= oa.shape[1]
    once = pl.Buffered(1)
    const = lambda shape: pl.BlockSpec(shape, lambda bi, i: (0,) * len(shape), pipeline_mode=once)
    rows_in = lambda w: pl.BlockSpec((pl.Element(1), pl.Element(tm), pl.Element(w)),
                                     lambda bi, i: (bi, pl.multiple_of(row0 + i * tm, 16), 0))
    return pl.pallas_call(
        _ffn_kernel,
        grid=(b, n_tiles),
        in_specs=[pl.BlockSpec((1, tm, d), lambda bi, i: (bi, i, 0)),
                  rows_in(oa.shape[-1]), rows_in(ob.shape[-1]), const(wo.shape), const((1, d)),
                  const(wg.shape), const(wu.shape), const(wd.shape), const((1, d))],
        out_specs=pl.BlockSpec((1, tm, d), lambda bi, i: (bi, i, 0)),
        out_shape=jax.ShapeDtypeStruct((b, n_tiles * tm, d), F32),
        compiler_params=pltpu.CompilerParams(
            dimension_semantics=("arbitrary", "arbitrary"), vmem_limit_bytes=VMEM_LIMIT),
        interpret=interpret,
        name="ffn",
    )(x, oa, ob, wo, g2, wg, wu, wd, gf)


def _layer(x, meta, attn_norm_g, w_in, mla_q_norm_g, w_uq, mla_kv_norm_g, w_ukv, w_o, ffn_norm_g,
           w_gate, w_up, w_down, out_g, topk, *, interpret=False):
    b, seq, d = x.shape
    t = N_META + seq
    n_out_tiles = -(-seq // TM)
    tp = -(-max(t, N_META + n_out_tiles * TM) // TK) * TK
    first_tail = t // TM
    tail = jnp.concatenate([x[:, first_tail * TM - N_META:], jnp.zeros((b, tp - t, d), x.dtype)], axis=1)
    head = jnp.concatenate([jnp.broadcast_to(meta[None].astype(x.dtype), (b, N_META, d)), x[:, :TM - N_META]], axis=1)
    edges = jnp.concatenate([head[:, None], tail.reshape(b, tp // TM - first_tail, TM, d)], axis=1)
    n_q_blocks = -(-t // TQ)
    win = _take_cols(w_in, _w_in_cols()).astype(BF16)
    wuq = _take_cols(w_uq, _w_uq_cols()).astype(BF16)
    wukv = jnp.take(w_ukv, jnp.asarray(_w_ukv_cols()), axis=1).astype(BF16)
    tables = _rope_tables(tp)
    qmla, kmla, vmla, qs, ks, vs, qi, ki, wi = _proj_call(
        x, edges, attn_norm_g[None], win, mla_q_norm_g[None], wuq, mla_kv_norm_g[None], wukv, tables,
        tp=tp, first_tail=first_tail, interpret=interpret)
    o_mla = _mla_call(qmla, kmla, vmla, tq=TQ, n_q_blocks=n_q_blocks, interpret=interpret)
    o_dsa = _dsa_call(qi, wi, qs, ki, ks, vs, tq=TQ, topk=topk, n_q_blocks=n_q_blocks, interpret=interpret)
    return _ffn_call(x, o_mla, o_dsa, w_o.astype(BF16), ffn_norm_g[None], w_gate.astype(BF16),
                     w_up.astype(BF16), w_down.astype(BF16), out_g[None], row0=N_META, n_tiles=n_out_tiles,
                     interpret=interpret)


def kernel(x, meta_tokens, attn_norm_g, w_in, mla_q_norm_g, w_uq, mla_kv_norm_g, w_ukv, w_o,
           ffn_norm_g, w_gate, w_up, w_down, final_norm_g):
    seq = x.shape[1]
    depth = w_in.shape[0]
    assert depth == 1, "the final norm is fused into the layer's last kernel"
    topk = min(TOPK_MAX, seq // 4)
    out = _layer(x, meta_tokens, attn_norm_g[0], w_in[0], mla_q_norm_g[0], w_uq[0], mla_kv_norm_g[0], w_ukv[0],
                 w_o[0], ffn_norm_g[0], w_gate[0], w_up[0], w_down[0], final_norm_g, topk)
    return out[:, :seq]
```
